```python
import functools
import jax, jax.numpy as jnp
from jax import lax
import numpy as np

D_MODEL = 2048
BATCH = 4
SEQ = 2048
DEPTH = 4
DEC_BATCH = 8
DEC_SEQ = 1
PAST_LEN = 16384
PAGE_SIZE = 128

N_A_LAYERS = DEPTH // 2
N_B_LAYERS = DEPTH - N_A_LAYERS
RMS_EPS = 1e-6
D_POOL = D_MODEL
POOL_WINDOWS = (2, 4, 8, 16)
N_POOL_GROUPS = len(POOL_WINDOWS)
POOL_GROUP_DIM = D_POOL // N_POOL_GROUPS
POOL_STATE = max(POOL_WINDOWS) - 1
HEAD_DIM = 128
DIL_PATTERNS = ((128, 1), (512, 4), (2048, 16))
N_DIL_GROUPS = len(DIL_PATTERNS)
HEADS_PER_GROUP = D_MODEL // 256
N_KV_HEADS = HEADS_PER_GROUP // 2
KV_DIM = N_KV_HEADS * HEAD_DIM
DIL_BLOCK = 128
MAX_WINDOW = max(w for w, _ in DIL_PATTERNS)
N_MEM = 256
MEM_HEADS = 4
MEM_HEAD_DIM = 128
MEM_DIM = MEM_HEADS * MEM_HEAD_DIM
D_FF = -(-8 * D_MODEL // (3 * 256)) * 256

kernel_name = 'yoco_pool_dilated_mem_decoder_step'


def _rmsnorm(x, g):
    xf = x.astype(jnp.float32)
    y = xf * lax.rsqrt(jnp.mean(xf * xf, axis=-1, keepdims=True) + RMS_EPS)
    return (y * g.astype(jnp.float32)).astype(x.dtype)


def _swiglu(h, w_gate, w_up, w_down):
    return (jax.nn.silu(h @ w_gate) * (h @ w_up)) @ w_down


def _pool_mix(u_ext, n_q, w_group, scale):
    n, length, _ = u_ext.shape
    uf = u_ext.astype(jnp.float32)
    cs = jnp.concatenate([jnp.zeros((n, 1, D_POOL), jnp.float32), jnp.cumsum(uf, axis=1)], axis=1)
    e = jnp.arange(length - n_q, length)
    hi = jnp.take(cs, e + 1, axis=1)
    u_q = uf[:, length - n_q:]
    parts = []
    for g, w in enumerate(POOL_WINDOWS):
        sl = slice(g * POOL_GROUP_DIM, (g + 1) * POOL_GROUP_DIM)
        lo = jnp.take(cs[:, :, sl], jnp.maximum(e + 1 - w, 0), axis=1)
        cnt = jnp.minimum(e + 1, w).astype(jnp.float32)[None, :, None]
        parts.append((hi[:, :, sl] - lo) / cnt - u_q[:, :, sl])
    p = jnp.stack(parts, axis=2).astype(u_ext.dtype)
    z = jnp.einsum('nlgc,gcd->nlgd', p, w_group).reshape(n, n_q, D_POOL)
    return z * scale


def _dilated_prompt(q, window, dil, k, v):
    n, s, h, dh = q.shape
    rep = h // N_KV_HEADS
    span = dil * DIL_BLOCK
    s_pad = -(-s // span) * span
    nb = s_pad // span
    padw = ((0, 0), (0, s_pad - s), (0, 0), (0, 0))
    qb = jnp.pad(q, padw).reshape(n, nb, DIL_BLOCK, dil, N_KV_HEADS, rep, dh)
    kb = jnp.pad(k, padw).reshape(n, nb, DIL_BLOCK, dil, N_KV_HEADS, dh)
    vb = jnp.pad(v, padw).reshape(n, nb, DIL_BLOCK, dil, N_KV_HEADS, dh)

    def band(t):
        prev = jnp.concatenate([jnp.zeros_like(t[:, :1]), t[:, :-1]], axis=1)
        return jnp.concatenate([prev, t], axis=2)

    kk, vv = band(kb), band(vb)
    sc = jnp.einsum('nbqrkgd,nbjrkd->nbrkgqj', qb, kk).astype(jnp.float32) * (dh ** -0.5)
    qi = jnp.arange(DIL_BLOCK)[:, None]
    kj = jnp.arange(2 * DIL_BLOCK)[None, :]
    dist = qi + DIL_BLOCK - kj
    blk = jnp.arange(nb)[:, None, None]
    valid = (dist >= 0) & (dist <= window // dil) & ((blk > 0) | (kj >= DIL_BLOCK))
    sc = jnp.where(valid[None, :, None, None, None], sc, -jnp.inf)
    m = jnp.max(sc, axis=-1, keepdims=True)
    pr = jnp.exp(sc - m)
    den = jnp.sum(pr, axis=-1, keepdims=True)
    o = jnp.einsum('nbrkgqj,nbjrkd->nbqrkgd', (pr / den).astype(v.dtype), vv)
    o = o.reshape(n, s_pad, h, dh)[:, :s]
    lse = (m + jnp.log(den))[..., 0].transpose(0, 1, 5, 2, 3, 4).reshape(n, s_pad, h)[:, :s]
    return o, lse


def _dilated_sample(q, window, dil, k_ext, v_ext):
    n, t, h, dh = q.shape
    rep = h // N_KV_HEADS
    lx = k_ext.shape[1]
    nk = window // dil
    idx = (lx - t) + jnp.arange(t)[:, None] - dil * jnp.arange(nk + 1)[None, :]
    valid = idx >= 0
    idc = jnp.maximum(idx, 0)
    kg = jnp.take(k_ext, idc, axis=1)
    vg = jnp.take(v_ext, idc, axis=1)
    qg = q.reshape(n, t, N_KV_HEADS, rep, dh)
    sc = jnp.einsum('ntkgd,ntjkd->ntkgj', qg, kg).astype(jnp.float32) * (dh ** -0.5)
    sc = jnp.where(valid[None, :, None, None, :], sc, -jnp.inf)
    m = jnp.max(sc, axis=-1, keepdims=True)
    pr = jnp.exp(sc - m)
    den = jnp.sum(pr, axis=-1, keepdims=True)
    o = jnp.einsum('ntkgj,ntjkd->ntkgd', (pr / den).astype(v_ext.dtype), vg).reshape(n, t, h, dh)
    lse = (m + jnp.log(den))[..., 0].reshape(n, t, h)
    return o, lse


def _dilated_mix(h, w_q, w_o, attend):
    n, length, _ = h.shape
    q = (h @ w_q).reshape(n, length, N_DIL_GROUPS, HEADS_PER_GROUP, HEAD_DIM)
    outs, lses = [], []
    for g, (w, d) in enumerate(DIL_PATTERNS):
        o, lse = attend(q[:, :, g], w, d)
        outs.append(o.astype(jnp.float32))
        lses.append(lse)
    wts = jax.nn.softmax(jnp.stack(lses), axis=0)[..., None]
    o = jnp.sum(wts * jnp.stack(outs), axis=0).astype(h.dtype)
    return o.reshape(n, length, HEADS_PER_GROUP * HEAD_DIM) @ w_o


def _mem_kv(mem, g, w_k, w_v):
    n, m, _ = mem.shape
    hm = _rmsnorm(mem, g)
    return ((hm @ w_k).reshape(n, m, MEM_HEADS, MEM_HEAD_DIM),
            (hm @ w_v).reshape(n, m, MEM_HEADS, MEM_HEAD_DIM))


def _mem_attend(h, mk, mv, w_q, w_o):
    n, length, _ = h.shape
    q = (h @ w_q).reshape(n, length, MEM_HEADS, MEM_HEAD_DIM)
    sc = jnp.einsum('nlhd,nmhd->nhlm', q, mk).astype(jnp.float32) * (MEM_HEAD_DIM ** -0.5)
    p = jax.nn.softmax(sc, axis=-1).astype(mv.dtype)
    o = jnp.einsum('nhlm,nmhd->nlhd', p, mv).reshape(n, length, MEM_DIM)
    return o @ w_o


def setup_inputs(seed: int = 0) -> dict:
    key = jax.random.key(seed)
    ks = jax.random.split(key, 32)
    f32 = jnp.float32
    w_buf = min(MAX_WINDOW, PAST_LEN)

    def normal(k, shape):
        return jax.random.normal(k, shape, f32)

    def dense(k, shape, fan_in):
        return jax.random.normal(k, shape, f32) * (fan_in ** -0.5)

    def gain(k, shape):
        return 1.0 + 0.02 * jax.random.normal(k, shape, f32)

    return {
        'x_prompt': normal(ks[0], (BATCH, SEQ, D_MODEL)),
        'x_sample': normal(ks[1], (DEC_BATCH, DEC_SEQ, D_MODEL)),
        'mem_prompt': normal(ks[2], (BATCH, N_MEM, D_MODEL)),
        'state_pool': normal(ks[3], (N_A_LAYERS, DEC_BATCH, POOL_STATE, D_POOL)),
        'cache_win_k': normal(ks[4], (DEC_BATCH, w_buf, N_KV_HEADS, HEAD_DIM)),
        'cache_win_v': normal(ks[5], (DEC_BATCH, w_buf, N_KV_HEADS, HEAD_DIM)),
        'cache_mem_k': normal(ks[6], (DEPTH, DEC_BATCH, N_MEM, MEM_HEADS, MEM_HEAD_DIM)),
        'cache_mem_v': normal(ks[7], (DEPTH, DEC_BATCH, N_MEM, MEM_HEADS, MEM_HEAD_DIM)),
        'norm_mix': gain(ks[8], (DEPTH, D_MODEL)),
        'norm_mem_q': gain(ks[9], (DEPTH, D_MODEL)),
        'norm_mem_kv': gain(ks[10], (DEPTH, D_MODEL)),
        'norm_ffn': gain(ks[11], (DEPTH, D_MODEL)),
        'pool_w_in': dense(ks[12], (N_A_LAYERS, D_MODEL, D_POOL), D_MODEL),
        'pool_w_group': dense(ks[13], (N_A_LAYERS, N_POOL_GROUPS, POOL_GROUP_DIM, POOL_GROUP_DIM), POOL_GROUP_DIM),
        'pool_scale': 1.0 + 0.1 * normal(ks[14], (N_A_LAYERS, D_POOL)),
        'pool_w_out': dense(ks[15], (N_A_LAYERS, D_POOL, D_MODEL), D_POOL),
        'norm_kv': gain(ks[16], (D_MODEL,)),
        'w_k_shared': dense(ks[17], (D_MODEL, KV_DIM), D_MODEL),
        'w_v_shared': dense(ks[18], (D_MODEL, KV_DIM), D_MODEL),
        'dil_w_q': dense(ks[19], (N_B_LAYERS, D_MODEL, N_DIL_GROUPS * HEADS_PER_GROUP * HEAD_DIM), D_MODEL),
        'dil_w_o': dense(ks[20], (N_B_LAYERS, HEADS_PER_GROUP * HEAD_DIM, D_MODEL), HEADS_PER_GROUP * HEAD_DIM),
        'mem_w_q': dense(ks[21], (DEPTH, D_MODEL, MEM_DIM), D_MODEL),
        'mem_w_k': dense(ks[22], (DEPTH, D_MODEL, MEM_DIM), D_MODEL),
        'mem_w_v': dense(ks[23], (DEPTH, D_MODEL, MEM_DIM), D_MODEL),
        'mem_w_o': dense(ks[24], (DEPTH, MEM_DIM, D_MODEL), MEM_DIM),
        'ffn_w_gate': dense(ks[25], (DEPTH, D_MODEL, D_FF), D_MODEL),
        'ffn_w_up': dense(ks[26], (DEPTH, D_MODEL, D_FF), D_MODEL),
        'ffn_w_down': dense(ks[27], (DEPTH, D_FF, D_MODEL), D_FF),
        'norm_final': gain(ks[28], (D_MODEL,)),
    }


def reference(x_prompt, x_sample, mem_prompt, state_pool, cache_win_k, cache_win_v, cache_mem_k, cache_mem_v,
              norm_mix, norm_mem_q, norm_mem_kv, norm_ffn, pool_w_in, pool_w_group, pool_scale, pool_w_out,
              norm_kv, w_k_shared, w_v_shared, dil_w_q, dil_w_o, mem_w_q, mem_w_k, mem_w_v, mem_w_o,
              ffn_w_gate, ffn_w_up, ffn_w_down, norm_final):

    def trunk(x, mem_k, mem_v, pool_buf, win_k, win_v):
        n, t, _ = x.shape
        new_pool = []
        k_new = v_new = None
        attend = None
        for l in range(DEPTH):
            if l == N_A_LAYERS:
                hkv = _rmsnorm(x, norm_kv)
                k_new = (hkv @ w_k_shared).reshape(n, t, N_KV_HEADS, HEAD_DIM)
                v_new = (hkv @ w_v_shared).reshape(n, t, N_KV_HEADS, HEAD_DIM)
                if win_k is None:
                    attend = functools.partial(_dilated_prompt, k=k_new, v=v_new)
                else:
                    attend = functools.partial(_dilated_sample,
                                               k_ext=jnp.concatenate([win_k, k_new], axis=1),
                                               v_ext=jnp.concatenate([win_v, v_new], axis=1))
            h = _rmsnorm(x, norm_mix[l])
            if l < N_A_LAYERS:
                u = h @ pool_w_in[l]
                u_ext = u if pool_buf is None else jnp.concatenate([pool_buf[l], u], axis=1)
                new_pool.append(u_ext[:, -POOL_STATE:])
                x = x + _pool_mix(u_ext, t, pool_w_group[l], pool_scale[l]) @ pool_w_out[l]
            else:
                j = l - N_A_LAYERS
                x = x + _dilated_mix(h, dil_w_q[j], dil_w_o[j], attend)
            x = x + _mem_attend(_rmsnorm(x, norm_mem_q[l]), mem_k[l], mem_v[l], mem_w_q[l], mem_w_o[l])
            x = x + _swiglu(_rmsnorm(x, norm_ffn[l]), ffn_w_gate[l], ffn_w_up[l], ffn_w_down[l])
        return _rmsnorm(x, norm_final), jnp.stack(new_pool), k_new, v_new

    mkv = [_mem_kv(mem_prompt, norm_mem_kv[l], mem_w_k[l], mem_w_v[l]) for l in range(DEPTH)]
    mem_k_p = jnp.stack([a for a, _ in mkv])
    mem_v_p = jnp.stack([b for _, b in mkv])

    y_prompt, pool_p, k_p, v_p = trunk(x_prompt, mem_k_p, mem_v_p, None, None, None)
    y_sample, pool_s, k_s, v_s = trunk(x_sample, cache_mem_k, cache_mem_v, state_pool, cache_win_k, cache_win_v)

    keep_from = max(0, k_p.shape[1] - MAX_WINDOW)
    win_k_p = k_p[:, keep_from:]
    win_v_p = v_p[:, keep_from:]
    return (y_prompt, y_sample, pool_p, pool_s, win_k_p, win_v_p, k_s, v_s, mem_k_p, mem_v_p)
```

```python
import functools

import jax
import jax.numpy as jnp
from jax import lax
from jax.experimental import pallas as pl
from jax.experimental.pallas import tpu as pltpu

F32 = jnp.float32
BF16 = jnp.bfloat16

RMS_EPS = 1e-6
POOL_WINDOWS = (2, 4, 8, 16)
POOL_STATE = max(POOL_WINDOWS) - 1
POOL_HALO = 16
DIL_PATTERNS = ((128, 1), (512, 4), (2048, 16))
DIL_BLOCK = 128
HEAD_DIM = 128
N_KV_HEADS = 4
HEADS_PER_GROUP = 8
MEM_HEADS = 4
MEM_HEAD_DIM = 128
NEG = -1e30
VMEM_LIMIT = 56 * 1024 * 1024
MIN_ROWS = 8

NT_DIMS = (((1,), (1,)), ((), ()))


def _params(*sem):
    return pltpu.CompilerParams(dimension_semantics=sem, vmem_limit_bytes=VMEM_LIMIT)


def _rms(x, g):
    ms = jnp.mean(x * x, axis=-1, keepdims=True)
    return x * lax.rsqrt(ms + RMS_EPS) * g


def _lhs_scratch_dtype(rows):
    return BF16 if rows % 16 == 0 else F32


def _softmax_rows(s):
    m = jnp.max(s, axis=-1, keepdims=True)
    e = jnp.exp(s - m)
    den = jnp.sum(e, axis=-1, keepdims=True)
    return e / den, m, den


def _norm_matmul_kernel(x_ref, g_ref, *refs, n_w):
    w_refs, o_refs, h_ref = refs[:n_w], refs[n_w:2 * n_w], refs[2 * n_w]

    @pl.when(pl.program_id(1) == 0)
    def _():
        h_ref[...] = _rms(x_ref[...], g_ref[...]).astype(h_ref.dtype)

    h = h_ref[...].astype(BF16)
    for w_ref, o_ref in zip(w_refs, o_refs):
        o_ref[...] = jnp.dot(h, w_ref[...], preferred_element_type=F32).astype(o_ref.dtype)


def norm_matmul(x, g, ws, out_dtypes, *, tm, tn, name):
    m, d = x.shape
    n = ws[0].shape[1]
    tm, tn = min(tm, m), min(tn, n)
    n_w = len(ws)
    return pl.pallas_call(
        functools.partial(_norm_matmul_kernel, n_w=n_w),
        grid=(m // tm, n // tn),
        in_specs=[pl.BlockSpec((tm, d), lambda i, j: (i, 0)),
                  pl.BlockSpec((1, d), lambda i, j: (0, 0))]
                 + [pl.BlockSpec((d, tn), lambda i, j: (0, j))] * n_w,
        out_specs=[pl.BlockSpec((tm, tn), lambda i, j: (i, j))] * n_w,
        out_shape=[jax.ShapeDtypeStruct((m, n), dt) for dt in out_dtypes],
        scratch_shapes=[pltpu.VMEM((tm, d), _lhs_scratch_dtype(tm))],
        compiler_params=_params("parallel", "arbitrary"),
        name=name,
    )(x, g.reshape(1, d), *ws)


def _norm_kernel(x_ref, g_ref, o_ref):
    o_ref[...] = _rms(x_ref[...], g_ref[...])


def rmsnorm(x, g, *, tm, name):
    m, d = x.shape
    tm = min(tm, m)
    return pl.pallas_call(
        _norm_kernel,
        grid=(m // tm,),
        in_specs=[pl.BlockSpec((tm, d), lambda i: (i, 0)), pl.BlockSpec((1, d), lambda i: (0, 0))],
        out_specs=pl.BlockSpec((tm, d), lambda i: (i, 0)),
        out_shape=jax.ShapeDtypeStruct((m, d), F32),
        compiler_params=_params("parallel"),
        name=name,
    )(x, g.reshape(1, d))


def _matmul_res_kernel(a_ref, w_ref, x_ref, o_ref):
    a = a_ref[...].astype(BF16)
    o_ref[...] = x_ref[...] + jnp.dot(a, w_ref[...], preferred_element_type=F32)


def matmul_residual(a, w, x, *, tm, tn, name):
    m, k = a.shape
    n = w.shape[1]
    tm, tn = min(tm, m), min(tn, n)
    return pl.pallas_call(
        _matmul_res_kernel,
        grid=(m // tm, n // tn),
        in_specs=[pl.BlockSpec((tm, k), lambda i, j: (i, 0)),
                  pl.BlockSpec((k, tn), lambda i, j: (0, j)),
                  pl.BlockSpec((tm, tn), lambda i, j: (i, j))],
        out_specs=pl.BlockSpec((tm, tn), lambda i, j: (i, j)),
        out_shape=jax.ShapeDtypeStruct((m, n), F32),
        compiler_params=_params("parallel", "parallel"),
        name=name,
    )(a, w, x)


def _ffn_kernel(x_ref, g_ref, wg_ref, wu_ref, wd_ref, o_ref, h_ref):
    @pl.when(pl.program_id(1) == 0)
    def _():
        x = x_ref[...]
        h_ref[...] = _rms(x, g_ref[...]).astype(h_ref.dtype)
        o_ref[...] = x

    h = h_ref[...].astype(BF16)
    gate = jnp.dot(h, wg_ref[...], preferred_element_type=F32)
    up = jnp.dot(h, wu_ref[...], preferred_element_type=F32)
    act = (gate * jax.nn.sigmoid(gate) * up).astype(BF16)
    o_ref[...] += jnp.dot(act, wd_ref[...], preferred_element_type=F32)


def ffn(x, g, w_gate, w_up, w_down, *, tm, tf, name):
    m, d = x.shape
    f = w_gate.shape[1]
    tm = min(tm, m)
    return pl.pallas_call(
        _ffn_kernel,
        grid=(m // tm, f // tf),
        in_specs=[pl.BlockSpec((tm, d), lambda i, j: (i, 0)),
                  pl.BlockSpec((1, d), lambda i, j: (0, 0)),
                  pl.BlockSpec((d, tf), lambda i, j: (0, j)),
                  pl.BlockSpec((d, tf), lambda i, j: (0, j)),
                  pl.BlockSpec((tf, d), lambda i, j: (j, 0))],
        out_specs=pl.BlockSpec((tm, d), lambda i, j: (i, 0)),
        out_shape=jax.ShapeDtypeStruct((m, d), F32),
        scratch_shapes=[pltpu.VMEM((tm, d), _lhs_scratch_dtype(tm))],
        compiler_params=_params("parallel", "arbitrary"),
        name=name,
    )(x, g.reshape(1, d), w_gate, w_up, w_down)


POOL_CHUNK = 64


def _pool_prompt_kernel(u_ref, uprev_ref, wg_ref, scale_ref, z_ref, ext_ref, p_ref, *, tm, tiles_per_seq):
    d = u_ref.shape[1]
    gdim = d // len(POOL_WINDOWS)
    t_in_seq = pl.program_id(0) % tiles_per_seq
    @pl.when(t_in_seq == 0)
    def _():
        ext_ref[0:POOL_HALO, :] = jnp.zeros((POOL_HALO, d), F32)

    @pl.when(t_in_seq != 0)
    def _():
        ext_ref[0:POOL_HALO, :] = uprev_ref[...]

    ext_ref[POOL_HALO:, :] = u_ref[...]
    for c in range(tm // POOL_CHUNK):
        r0 = POOL_HALO + c * POOL_CHUNK
        pos = t_in_seq * tm + c * POOL_CHUNK + lax.broadcasted_iota(jnp.int32, (POOL_CHUNK, 1), 0)
        for gi, w in enumerate(POOL_WINDOWS):
            cols = slice(gi * gdim, (gi + 1) * gdim)
            u_q = ext_ref[r0:r0 + POOL_CHUNK, cols]
            acc = u_q
            for k in range(1, w):
                acc = acc + ext_ref[r0 - k:r0 - k + POOL_CHUNK, cols]
            inv_cnt = 1.0 / jnp.minimum(pos + 1, w).astype(F32)
            p_ref[c * POOL_CHUNK:(c + 1) * POOL_CHUNK, cols] = (acc * inv_cnt - u_q).astype(BF16)
    for gi in range(len(POOL_WINDOWS)):
        cols = slice(gi * gdim, (gi + 1) * gdim)
        z = jnp.dot(p_ref[:, cols], wg_ref[gi], preferred_element_type=F32)
        z_ref[:, cols] = (z * scale_ref[:, cols]).astype(z_ref.dtype)


def pool_prompt(u, w_group, scale, *, seq, tm, name):
    m, d = u.shape
    tiles_per_seq = seq // tm
    halo_blocks = tm // POOL_HALO
    return pl.pallas_call(
        functools.partial(_pool_prompt_kernel, tm=tm, tiles_per_seq=tiles_per_seq),
        grid=(m // tm,),
        in_specs=[pl.BlockSpec((tm, d), lambda i: (i, 0)),
                  pl.BlockSpec((POOL_HALO, d), lambda i: (jnp.maximum(i * halo_blocks - 1, 0), 0)),
                  pl.BlockSpec(w_group.shape, lambda i: (0, 0, 0)),
                  pl.BlockSpec((1, d), lambda i: (0, 0))],
        out_specs=pl.BlockSpec((tm, d), lambda i: (i, 0)),
        out_shape=jax.ShapeDtypeStruct((m, d), BF16),
        scratch_shapes=[pltpu.VMEM((tm + POOL_HALO, d), F32), pltpu.VMEM((tm, d), BF16)],
        compiler_params=_params("parallel"),
        name=name,
    )(u, u, w_group, scale.reshape(1, d))


def _pool_sample_kernel(state_ref, u_ref, wg_ref, scale_ref, z_ref):
    n_state, _, d = state_ref.shape
    gdim = d // len(POOL_WINDOWS)
    u = u_ref[...]
    for gi, w in enumerate(POOL_WINDOWS):
        cols = slice(gi * gdim, (gi + 1) * gdim)
        u_q = u[:, cols]
        acc = u_q
        for k in range(1, w):
            acc = acc + state_ref[n_state - k, :, cols]
        cnt = float(min(n_state + 1, w))
        p = (acc / cnt - u_q).astype(BF16)
        z = jnp.dot(p, wg_ref[gi], preferred_element_type=F32)
        z_ref[:, cols] = z * scale_ref[:, cols]


def pool_sample(state_t, u, w_group, scale, *, name):
    b, d = u.shape
    return pl.pallas_call(
        _pool_sample_kernel,
        out_shape=jax.ShapeDtypeStruct((b, d), F32),
        compiler_params=pltpu.CompilerParams(vmem_limit_bytes=VMEM_LIMIT),
        name=name,
    )(state_t, u, w_group, scale.reshape(1, d))


def _mem_attn_kernel(q_ref, k_ref, v_ref, o_ref):
    rows = q_ref.shape[0]
    q = q_ref[...]
    if rows < MIN_ROWS:
        q = jnp.broadcast_to(q, (MIN_ROWS, q.shape[1]))
    q = q.astype(BF16)
    kb = k_ref[...].astype(BF16)
    vb = v_ref[...].astype(BF16)
    scale = MEM_HEAD_DIM ** -0.5
    outs = []
    for h in range(MEM_HEADS):
        sl = slice(h * MEM_HEAD_DIM, (h + 1) * MEM_HEAD_DIM)
        s = lax.dot_general(q[:, sl], kb[:, sl], NT_DIMS, preferred_element_type=F32) * scale
        p, _, _ = _softmax_rows(s)
        outs.append(jnp.dot(p.astype(BF16), vb[:, sl], preferred_element_type=F32))
    o = jnp.concatenate(outs, axis=1)
    o_ref[...] = o[:rows].astype(o_ref.dtype)


def mem_attention(q, mk, mv, *, tm, out_dtype, name):
    n, length, dm = q.shape
    n_mem = mk.shape[1]
    tm = min(tm, length)
    return pl.pallas_call(
        _mem_attn_kernel,
        grid=(n, length // tm),
        in_specs=[pl.BlockSpec((None, tm, dm), lambda b, i: (b, i, 0)),
                  pl.BlockSpec((None, n_mem, dm), lambda b, i: (b, 0, 0)),
                  pl.BlockSpec((None, n_mem, dm), lambda b, i: (b, 0, 0))],
        out_specs=pl.BlockSpec((None, tm, dm), lambda b, i: (b, i, 0)),
        out_shape=jax.ShapeDtypeStruct((n, length, dm), out_dtype),
        compiler_params=_params("parallel", "parallel"),
        name=name,
    )(q, mk, mv)


def _dil_prompt_kernel(q_ref, kp_ref, kc_ref, vp_ref, vc_ref, o_ref, lse_ref):
    blk = pl.program_id(2)
    q = q_ref[...]
    kcat = jnp.concatenate([kp_ref[...], kc_ref[...]], axis=0).astype(BF16)
    vcat = jnp.concatenate([vp_ref[...], vc_ref[...]], axis=0).astype(BF16)
    qi = lax.broadcasted_iota(jnp.int32, (DIL_BLOCK, 2 * DIL_BLOCK), 0)
    kj = lax.broadcasted_iota(jnp.int32, (DIL_BLOCK, 2 * DIL_BLOCK), 1)
    dist = qi + DIL_BLOCK - kj
    first_key = jnp.where(blk > 0, 0, DIL_BLOCK)
    valid = (dist >= 0) & (dist <= DIL_BLOCK) & (kj >= first_key)
    scale = HEAD_DIM ** -0.5
    lane = lax.broadcasted_iota(jnp.int32, (DIL_BLOCK, 128), 1)
    lse_tile = jnp.zeros((DIL_BLOCK, 128), F32)
    rep = HEADS_PER_GROUP // N_KV_HEADS
    valid2 = jnp.concatenate([valid] * rep, axis=0)
    for kv in range(N_KV_HEADS):
        ksl = slice(kv * HEAD_DIM, (kv + 1) * HEAD_DIM)
        heads = [kv * rep + r for r in range(rep)]
        q2 = jnp.concatenate([q[:, h * HEAD_DIM:(h + 1) * HEAD_DIM] for h in heads], axis=0)
        s = lax.dot_general(q2, kcat[:, ksl], NT_DIMS, preferred_element_type=F32) * scale
        s = jnp.where(valid2, s, NEG)
        p, m, den = _softmax_rows(s)
        o2 = jnp.dot(p.astype(BF16), vcat[:, ksl], preferred_element_type=F32)
        lse2 = m + jnp.log(den)
        for r, h in enumerate(heads):
            rows = slice(r * DIL_BLOCK, (r + 1) * DIL_BLOCK)
            o_ref[:, h * HEAD_DIM:(h + 1) * HEAD_DIM] = o2[rows]
            lse_tile = jnp.where(lane == h, lse2[rows], lse_tile)
    lse_ref[...] = lse_tile


def dil_attention_prompt(q, k, v, group, dil, *, name):
    n, s, qd = q.shape
    kd = k.shape[2]
    od = HEADS_PER_GROUP * HEAD_DIM
    sub = s // dil
    n_groups = qd // od
    qv = q.reshape(n, sub, dil * qd)
    kv = k.reshape(n, sub, dil * kd)
    vv = v.reshape(n, sub, dil * kd)
    cur = lambda b, r, i: (b, i, r)
    prev = lambda b, r, i: (b, jnp.maximum(i - 1, 0), r)
    o, lse = pl.pallas_call(
        _dil_prompt_kernel,
        grid=(n, dil, sub // DIL_BLOCK),
        in_specs=[pl.BlockSpec((None, DIL_BLOCK, od), lambda b, r, i: (b, i, r * n_groups + group)),
                  pl.BlockSpec((None, DIL_BLOCK, kd), prev), pl.BlockSpec((None, DIL_BLOCK, kd), cur),
                  pl.BlockSpec((None, DIL_BLOCK, kd), prev), pl.BlockSpec((None, DIL_BLOCK, kd), cur)],
        out_specs=[pl.BlockSpec((None, DIL_BLOCK, od), cur), pl.BlockSpec((None, DIL_BLOCK, 128), cur)],
        out_shape=[jax.ShapeDtypeStruct((n, sub, dil * od), F32),
                   jax.ShapeDtypeStruct((n, sub, dil * 128), F32)],
        compiler_params=_params("parallel", "parallel", "arbitrary"),
        name=name,
    )(qv, kv, kv, vv, vv)
    return o.reshape(n, s, od), lse.reshape(n, s, 128)


def _dil_merge_kernel(o0_ref, o1_ref, o2_ref, l0_ref, l1_ref, l2_ref, out_ref):
    o_refs = (o0_ref, o1_ref, o2_ref)
    lses = [l0_ref[...], l1_ref[...], l2_ref[...]]
    m = jnp.maximum(jnp.maximum(lses[0], lses[1]), lses[2])
    es = [jnp.exp(l - m) for l in lses]
    den = es[0] + es[1] + es[2]
    ws = [e / den for e in es]
    for h in range(HEADS_PER_GROUP):
        sl = slice(h * HEAD_DIM, (h + 1) * HEAD_DIM)
        acc = ws[0][:, h:h + 1] * o_refs[0][:, sl]
        for gi in (1, 2):
            acc = acc + ws[gi][:, h:h + 1] * o_refs[gi][:, sl]
        out_ref[:, sl] = acc.astype(out_ref.dtype)


def dil_merge(os_, lses, *, tm, name):
    m, od = os_[0].shape
    tm = min(tm, m)
    row = lambda i: (i, 0)
    return pl.pallas_call(
        _dil_merge_kernel,
        grid=(m // tm,),
        in_specs=[pl.BlockSpec((tm, od), row)] * 3 + [pl.BlockSpec((tm, 128), row)] * 3,
        out_specs=pl.BlockSpec((tm, od), row),
        out_shape=jax.ShapeDtypeStruct((m, od), BF16),
        compiler_params=_params("parallel"),
        name=name,
    )(*os_, *lses)


def _dil_sample_kernel(q_ref, kn_ref, vn_ref, k0_ref, k1_ref, k2_ref, v0_ref, v1_ref, v2_ref, o_ref):
    k_refs = (k0_ref, k1_ref, k2_ref)
    v_refs = (v0_ref, v1_ref, v2_ref)
    scale = HEAD_DIM ** -0.5
    rep = HEADS_PER_GROUP // N_KV_HEADS
    head_row = lax.broadcasted_iota(jnp.int32, (HEADS_PER_GROUP, 1), 0)
    kn = kn_ref[...].astype(BF16).astype(F32)
    vn = vn_ref[...].astype(BF16).astype(F32)
    outs, lses = [], []
    for gi in range(len(DIL_PATTERNS)):
        qg = q_ref[gi * HEADS_PER_GROUP:(gi + 1) * HEADS_PER_GROUP, :].astype(BF16)
        kc = k_refs[gi][...].astype(BF16)
        vc = v_refs[gi][...].astype(BF16)
        qf = qg.astype(F32)
        s = jnp.zeros((HEADS_PER_GROUP, DIL_BLOCK), F32)
        s_new = jnp.zeros((HEADS_PER_GROUP, 1), F32)
        for kv in range(N_KV_HEADS):
            sl = slice(kv * HEAD_DIM, (kv + 1) * HEAD_DIM)
            mine = (head_row >= kv * rep) & (head_row < (kv + 1) * rep)
            s_kv = lax.dot_general(qg, kc[:, sl], NT_DIMS, preferred_element_type=F32)
            s = jnp.where(mine, s_kv, s)
            s_new = jnp.where(mine, jnp.sum(qf * kn[kv:kv + 1, :], axis=-1, keepdims=True), s_new)
        s = s * scale
        s_new = s_new * scale
        m = jnp.maximum(jnp.max(s, axis=-1, keepdims=True), s_new)
        e = jnp.exp(s - m)
        e_new = jnp.exp(s_new - m)
        den = jnp.sum(e, axis=-1, keepdims=True) + e_new
        p = (e / den).astype(BF16)
        p_new = (e_new / den).astype(BF16).astype(F32)
        o = jnp.zeros((HEADS_PER_GROUP, HEAD_DIM), F32)
        for kv in range(N_KV_HEADS):
            sl = slice(kv * HEAD_DIM, (kv + 1) * HEAD_DIM)
            mine = (head_row >= kv * rep) & (head_row < (kv + 1) * rep)
            o_kv = jnp.dot(p, vc[:, sl], preferred_element_type=F32) + p_new * vn[kv:kv + 1, :]
            o = jnp.where(mine, o_kv, o)
        outs.append(o)
        lses.append(m + jnp.log(den))
    mm = jnp.maximum(jnp.maximum(lses[0], lses[1]), lses[2])
    es = [jnp.exp(l - mm) for l in lses]
    den = es[0] + es[1] + es[2]
    o_ref[...] = (es[0] / den) * outs[0] + (es[1] / den) * outs[1] + (es[2] / den) * outs[2]


def dil_attention_sample(q, k_new, v_new, cache_k, cache_v, *, name):
    b, qd = q.shape
    wbuf, kd = cache_k.shape[1], cache_k.shape[2]
    n_heads = qd // HEAD_DIM
    q3 = q.reshape(b, n_heads, HEAD_DIM)
    kn3 = k_new.reshape(b, N_KV_HEADS, HEAD_DIM)
    vn3 = v_new.reshape(b, N_KV_HEADS, HEAD_DIM)
    views_k, views_v, specs = [], [], []
    for window, dil in DIL_PATTERNS:
        assert window // dil == DIL_BLOCK and wbuf % dil == 0 and window <= wbuf
        sub = wbuf // dil
        blk = sub // DIL_BLOCK - 1
        views_k.append(cache_k.reshape(b, sub, dil * kd))
        views_v.append(cache_v.reshape(b, sub, dil * kd))
        specs.append(pl.BlockSpec((None, DIL_BLOCK, kd), functools.partial(lambda i, blk: (i, blk, 0), blk=blk)))
    small = lambda rows: pl.BlockSpec((None, rows, HEAD_DIM), lambda i: (i, 0, 0))
    o = pl.pallas_call(
        _dil_sample_kernel,
        grid=(b,),
        in_specs=[small(n_heads), small(N_KV_HEADS), small(N_KV_HEADS)] + specs + specs,
        out_specs=small(HEADS_PER_GROUP),
        out_shape=jax.ShapeDtypeStruct((b, HEADS_PER_GROUP, HEAD_DIM), F32),
        compiler_params=_params("parallel"),
        name=name,
    )(q3, kn3, vn3, *views_k, *views_v)
    return o.reshape(b, HEADS_PER_GROUP * HEAD_DIM)


def _trunk(x, mem_k, mem_v, w, *, n_seq, seq, pool_state, win_k, win_v, tag):
    depth = len(w["ffn_w_gate"])
    n_a = len(w["pool_w_in"])
    m, d = x.shape
    sample = pool_state is not None
    new_pool = []
    k_new = v_new = None
    for l in range(depth):
        if l == n_a:
            k_new, v_new = norm_matmul(x, w["norm_kv"], [w["w_k_shared"], w["w_v_shared"]], [F32, F32],
                                       tm=1024, tn=512, name=f"{tag}_kv")
        if l < n_a:
            (u,) = norm_matmul(x, w["norm_mix"][l], [w["pool_w_in"][l]], [F32], tm=1024, tn=512,
                               name=f"{tag}_pool_in{l}")
            if sample:
                state = pool_state[l]
                new_pool.append(jnp.concatenate([state[:, 1:], u[:, None, :]], axis=1))
                z = pool_sample(jnp.swapaxes(state, 0, 1), u, w["pool_w_group"][l], w["pool_scale"][l],
                                name=f"{tag}_pool{l}")
            else:
                new_pool.append(u.reshape(n_seq, seq, d)[:, seq - POOL_STATE:])
                z = pool_prompt(u, w["pool_w_group"][l], w["pool_scale"][l], seq=seq, tm=512,
                                name=f"{tag}_pool{l}")
            x = matmul_residual(z, w["pool_w_out"][l], x, tm=1024, tn=1024, name=f"{tag}_pool_out{l}")
        else:
            j = l - n_a
            (q,) = norm_matmul(x, w["norm_mix"][l], [w["dil_w_q"][j]], [F32 if sample else BF16],
                               tm=1024, tn=512, name=f"{tag}_dil_q{j}")
            if sample:
                o = dil_attention_sample(q, k_new, v_new, win_k, win_v, name=f"{tag}_dil_attn{j}")
            else:
                q3 = q.reshape(n_seq, seq, -1)
                k3 = k_new.reshape(n_seq, seq, -1)
                v3 = v_new.reshape(n_seq, seq, -1)
                os_, lses = [], []
                for gi, (_, dil) in enumerate(DIL_PATTERNS):
                    o_g, lse_g = dil_attention_prompt(q3, k3, v3, gi, dil, name=f"{tag}_dil_attn{j}_{gi}")
                    os_.append(o_g.reshape(m, -1))
                    lses.append(lse_g.reshape(m, -1))
                o = dil_merge(os_, lses, tm=512, name=f"{tag}_dil_merge{j}")
            x = matmul_residual(o, w["dil_w_o"][j], x, tm=1024, tn=1024, name=f"{tag}_dil_out{j}")
        (q,) = norm_matmul(x, w["norm_mem_q"][l], [w["mem_w_q"][l]], [F32 if sample else BF16],
                           tm=1024, tn=512, name=f"{tag}_mem_q{l}")
        o = mem_attention(q.reshape(n_seq, seq, -1), mem_k[l], mem_v[l], tm=512,
                          out_dtype=F32 if sample else BF16, name=f"{tag}_mem_attn{l}")
        x = matmul_residual(o.reshape(m, -1), w["mem_w_o"][l], x, tm=1024, tn=1024, name=f"{tag}_mem_out{l}")
        x = ffn(x, w["norm_ffn"][l], w["ffn_w_gate"][l], w["ffn_w_up"][l], w["ffn_w_down"][l],
                tm=512, tf=512, name=f"{tag}_ffn{l}")
    y = rmsnorm(x, w["norm_final"], tm=512, name=f"{tag}_final")
    return y, jnp.stack(new_pool), k_new, v_new


def kernel(x_prompt, x_sample, mem_prompt, state_pool, cache_win_k, cache_win_v, cache_mem_k, cache_mem_v,
           norm_mix, norm_mem_q, norm_mem_kv, norm_ffn, pool_w_in, pool_w_group, pool_scale, pool_w_out,
           norm_kv, w_k_shared, w_v_shared, dil_w_q, dil_w_o, mem_w_q, mem_w_k, mem_w_v, mem_w_o,
           ffn_w_gate, ffn_w_up, ffn_w_down, norm_final):
    batch, seq, d = x_prompt.shape
    dec_batch, dec_seq, _ = x_sample.shape
    depth = norm_mix.shape[0]
    n_mem = mem_prompt.shape[1]
    bf = lambda a: a.astype(BF16)
    w = dict(norm_mix=norm_mix, norm_mem_q=norm_mem_q, norm_ffn=norm_ffn, norm_kv=norm_kv, norm_final=norm_final,
             pool_scale=pool_scale, pool_w_in=bf(pool_w_in), pool_w_group=bf(pool_w_group), pool_w_out=bf(pool_w_out),
             w_k_shared=bf(w_k_shared), w_v_shared=bf(w_v_shared), dil_w_q=bf(dil_w_q), dil_w_o=bf(dil_w_o),
             mem_w_q=bf(mem_w_q), mem_w_o=bf(mem_w_o), ffn_w_gate=bf(ffn_w_gate), ffn_w_up=bf(ffn_w_up),
             ffn_w_down=bf(ffn_w_down))

    mem_rows = mem_prompt.reshape(batch * n_mem, d)
    mk_p, mv_p = [], []
    for l in range(depth):
        mk, mv = norm_matmul(mem_rows, norm_mem_kv[l], [bf(mem_w_k[l]), bf(mem_w_v[l])], [F32, F32],
                             tm=1024, tn=512, name=f"mem_kv{l}")
        mk_p.append(mk.reshape(batch, n_mem, -1))
        mv_p.append(mv.reshape(batch, n_mem, -1))

    y_p, pool_p, k_p, v_p = _trunk(x_prompt.reshape(batch * seq, d), mk_p, mv_p, w, n_seq=batch, seq=seq,
                                   pool_state=None, win_k=None, win_v=None, tag="p")
    mk_s = cache_mem_k.reshape(depth, dec_batch, n_mem, -1)
    mv_s = cache_mem_v.reshape(depth, dec_batch, n_mem, -1)
    wbuf = cache_win_k.shape[1]
    y_s, pool_s, k_s, v_s = _trunk(x_sample.reshape(dec_batch * dec_seq, d), mk_s, mv_s, w, n_seq=dec_batch,
                                   seq=dec_seq, pool_state=state_pool,
                                   win_k=cache_win_k.reshape(dec_batch, wbuf, -1),
                                   win_v=cache_win_v.reshape(dec_batch, wbuf, -1), tag="s")

    max_window = max(wd for wd, _ in DIL_PATTERNS)
    keep_from = max(0, seq - max_window)
    kv_shape = (N_KV_HEADS, HEAD_DIM)
    mem_shape = (depth, batch, n_mem, MEM_HEADS, MEM_HEAD_DIM)
    return (y_p.reshape(batch, seq, d), y_s.reshape(dec_batch, dec_seq, d), pool_p, pool_s,
            k_p.reshape(batch, seq, *kv_shape)[:, keep_from:], v_p.reshape(batch, seq, *kv_shape)[:, keep_from:],
            k_s.reshape(dec_batch, dec_seq, *kv_shape), v_s.reshape(dec_batch, dec_seq, *kv_shape),
            jnp.stack(mk_p).reshape(mem_shape), jnp.stack(mv_p).reshape(mem_shape))
```

```python
import functools

import jax
import jax.numpy as jnp
from jax import lax
from jax.experimental import pallas as pl
from jax.experimental.pallas import tpu as pltpu

F32 = jnp.float32
BF16 = jnp.bfloat16

RMS_EPS = 1e-6
POOL_WINDOWS = (2, 4, 8, 16)
POOL_STATE = max(POOL_WINDOWS) - 1
POOL_HALO = 16
POOL_CHUNK = 64
DIL_PATTERNS = ((128, 1), (512, 4), (2048, 16))
DIL_BLOCK = 128
DIL_GROUP = 4
HEAD_DIM = 128
N_KV_HEADS = 4
HEADS_PER_GROUP = 8
KV_REP = HEADS_PER_GROUP // N_KV_HEADS
MEM_HEADS = 4
MEM_HEAD_DIM = 128
LANES = 128
NEG = -1e30
VMEM_LIMIT = 56 * 1024 * 1024
MIN_ROWS = 8

NT_DIMS = (((1,), (1,)), ((), ()))


def _params(*sem):
    return pltpu.CompilerParams(dimension_semantics=sem, vmem_limit_bytes=VMEM_LIMIT)


def _layer_spec(layer, block, index_map):
    return pl.BlockSpec((None,) + tuple(block), lambda *g: (layer,) + tuple(index_map(*g)))


def _rms(x, g):
    ms = jnp.mean(x * x, axis=-1, keepdims=True)
    return x * lax.rsqrt(ms + RMS_EPS) * g


def _lhs_scratch_dtype(rows):
    return BF16 if rows % 16 == 0 else F32


def _softmax_rows(s):
    m = jnp.max(s, axis=-1, keepdims=True)
    e = jnp.exp(s - m)
    den = jnp.sum(e, axis=-1, keepdims=True)
    return e / den, m, den


def _norm_matmul_kernel(x_ref, g_ref, *refs, n_w):
    w_refs, o_refs, h_ref = refs[:n_w], refs[n_w:2 * n_w], refs[2 * n_w]

    @pl.when(pl.program_id(1) == 0)
    def _():
        h_ref[...] = _rms(x_ref[...], g_ref[...]).astype(h_ref.dtype)

    h = h_ref[...].astype(BF16)
    for w_ref, o_ref in zip(w_refs, o_refs):
        o_ref[...] = jnp.dot(h, w_ref[...].astype(BF16), preferred_element_type=F32).astype(o_ref.dtype)


def norm_matmul(x, g, ws, out_dtypes, *, tm, tn, name):
    m, d = x.shape
    gains, lg = g
    n = ws[0][0].shape[2]
    tm, tn = min(tm, m), min(tn, n)
    n_w = len(ws)
    return pl.pallas_call(
        functools.partial(_norm_matmul_kernel, n_w=n_w),
        grid=(m // tm, n // tn),
        in_specs=[pl.BlockSpec((tm, d), lambda i, j: (i, 0)),
                  _layer_spec(lg, (1, d), lambda i, j: (0, 0))]
                 + [_layer_spec(lw, (d, tn), lambda i, j: (0, j)) for _, lw in ws],
        out_specs=[pl.BlockSpec((tm, tn), lambda i, j: (i, j))] * n_w,
        out_shape=[jax.ShapeDtypeStruct((m, n), dt) for dt in out_dtypes],
        scratch_shapes=[pltpu.VMEM((tm, d), _lhs_scratch_dtype(tm))],
        compiler_params=_params("parallel", "arbitrary"),
        name=name,
    )(x, gains, *[w for w, _ in ws])


def _norm_kernel(x_ref, g_ref, o_ref):
    o_ref[...] = _rms(x_ref[...], g_ref[...])


def rmsnorm(x, g, *, tm, name):
    m, d = x.shape
    tm = min(tm, m)
    return pl.pallas_call(
        _norm_kernel,
        grid=(m // tm,),
        in_specs=[pl.BlockSpec((tm, d), lambda i: (i, 0)), pl.BlockSpec((1, d), lambda i: (0, 0))],
        out_specs=pl.BlockSpec((tm, d), lambda i: (i, 0)),
        out_shape=jax.ShapeDtypeStruct((m, d), F32),
        compiler_params=_params("parallel"),
        name=name,
    )(x, g.reshape(1, d))


def _matmul_res_kernel(a_ref, w_ref, x_ref, o_ref):
    a = a_ref[...].astype(BF16)
    o_ref[...] = x_ref[...] + jnp.dot(a, w_ref[...].astype(BF16), preferred_element_type=F32)


def matmul_residual(a, w, x, *, tm, tn, name):
    m, k = a.shape
    w, lw = w
    n = w.shape[2]
    tm, tn = min(tm, m), min(tn, n)
    return pl.pallas_call(
        _matmul_res_kernel,
        grid=(m // tm, n // tn),
        in_specs=[pl.BlockSpec((tm, k), lambda i, j: (i, 0)),
                  _layer_spec(lw, (k, tn), lambda i, j: (0, j)),
                  pl.BlockSpec((tm, tn), lambda i, j: (i, j))],
        out_specs=pl.BlockSpec((tm, tn), lambda i, j: (i, j)),
        out_shape=jax.ShapeDtypeStruct((m, n), F32),
        compiler_params=_params("parallel", "parallel"),
        name=name,
    )(a, w, x)


def _ffn_kernel(x_ref, g_ref, wg_ref, wu_ref, wd_ref, o_ref, h_ref):
    @pl.when(pl.program_id(1) == 0)
    def _():
        x = x_ref[...]
        h_ref[...] = _rms(x, g_ref[...]).astype(h_ref.dtype)
        o_ref[...] = x

    h = h_ref[...].astype(BF16)
    gate = jnp.dot(h, wg_ref[...], preferred_element_type=F32)
    up = jnp.dot(h, wu_ref[...], preferred_element_type=F32)
    act = (gate * jax.nn.sigmoid(gate) * up).astype(BF16)
    o_ref[...] += jnp.dot(act, wd_ref[...], preferred_element_type=F32)


def ffn(x, g, w_gate, w_up, w_down, layer, *, tm, tf, name):
    m, d = x.shape
    gains, lg = g
    f = w_gate.shape[2]
    tm = min(tm, m)
    return pl.pallas_call(
        _ffn_kernel,
        grid=(m // tm, f // tf),
        in_specs=[pl.BlockSpec((tm, d), lambda i, j: (i, 0)),
                  _layer_spec(lg, (1, d), lambda i, j: (0, 0)),
                  _layer_spec(layer, (d, tf), lambda i, j: (0, j)),
                  _layer_spec(layer, (d, tf), lambda i, j: (0, j)),
                  _layer_spec(layer, (tf, d), lambda i, j: (j, 0))],
        out_specs=pl.BlockSpec((tm, d), lambda i, j: (i, 0)),
        out_shape=jax.ShapeDtypeStruct((m, d), F32),
        scratch_shapes=[pltpu.VMEM((tm, d), _lhs_scratch_dtype(tm))],
        compiler_params=_params("parallel", "arbitrary"),
        name=name,
    )(x, gains, w_gate, w_up, w_down)


def _pool_prompt_kernel(u_ref, uprev_ref, wg_ref, scale_ref, z_ref, ext_ref, p_ref, *, tm, tiles_per_seq):
    d = u_ref.shape[1]
    gdim = d // len(POOL_WINDOWS)
    t_in_seq = pl.program_id(0) % tiles_per_seq

    @pl.when(t_in_seq == 0)
    def _():
        ext_ref[0:POOL_HALO, :] = jnp.zeros((POOL_HALO, d), F32)

    @pl.when(t_in_seq != 0)
    def _():
        ext_ref[0:POOL_HALO, :] = uprev_ref[...]

    ext_ref[POOL_HALO:, :] = u_ref[...]
    for c in range(tm // POOL_CHUNK):
        r0 = POOL_HALO + c * POOL_CHUNK
        pos = t_in_seq * tm + c * POOL_CHUNK + lax.broadcasted_iota(jnp.int32, (POOL_CHUNK, 1), 0)
        for gi, w in enumerate(POOL_WINDOWS):
            cols = slice(gi * gdim, (gi + 1) * gdim)
            u_q = ext_ref[r0:r0 + POOL_CHUNK, cols]
            acc = u_q
            for k in range(1, w):
                acc = acc + ext_ref[r0 - k:r0 - k + POOL_CHUNK, cols]
            inv_cnt = 1.0 / jnp.minimum(pos + 1, w).astype(F32)
            p_ref[c * POOL_CHUNK:(c + 1) * POOL_CHUNK, cols] = (acc * inv_cnt - u_q).astype(BF16)
    for gi in range(len(POOL_WINDOWS)):
        cols = slice(gi * gdim, (gi + 1) * gdim)
        z = jnp.dot(p_ref[:, cols], wg_ref[gi].astype(BF16), preferred_element_type=F32)
        z_ref[:, cols] = (z * scale_ref[:, cols]).astype(z_ref.dtype)


def pool_prompt(u, w_group, scale, layer, *, seq, tm, name):
    m, d = u.shape
    tiles_per_seq = seq // tm
    halo_blocks = tm // POOL_HALO
    return pl.pallas_call(
        functools.partial(_pool_prompt_kernel, tm=tm, tiles_per_seq=tiles_per_seq),
        grid=(m // tm,),
        in_specs=[pl.BlockSpec((tm, d), lambda i: (i, 0)),
                  pl.BlockSpec((POOL_HALO, d), lambda i: (jnp.maximum(i * halo_blocks - 1, 0), 0)),
                  _layer_spec(layer, w_group.shape[1:], lambda i: (0, 0, 0)),
                  _layer_spec(layer, (1, d), lambda i: (0, 0))],
        out_specs=pl.BlockSpec((tm, d), lambda i: (i, 0)),
        out_shape=jax.ShapeDtypeStruct((m, d), BF16),
        scratch_shapes=[pltpu.VMEM((tm + POOL_HALO, d), F32), pltpu.VMEM((tm, d), BF16)],
        compiler_params=_params("parallel"),
        name=name,
    )(u, u, w_group, scale)


def _pool_sample_kernel(state_ref, u_ref, wg_ref, scale_ref, z_ref):
    n_state, _, d = state_ref.shape
    gdim = d // len(POOL_WINDOWS)
    u = u_ref[...]
    for gi, w in enumerate(POOL_WINDOWS):
        cols = slice(gi * gdim, (gi + 1) * gdim)
        u_q = u[:, cols]
        acc = u_q
        for k in range(1, w):
            acc = acc + state_ref[n_state - k, :, cols]
        cnt = float(min(n_state + 1, w))
        p = (acc / cnt - u_q).astype(BF16)
        z = jnp.dot(p, wg_ref[gi].astype(BF16), preferred_element_type=F32)
        z_ref[:, cols] = z * scale_ref[:, cols]


def pool_sample(state_t, u, w_group, scale, layer, *, name):
    b, d = u.shape
    return pl.pallas_call(
        _pool_sample_kernel,
        grid=(1,),
        in_specs=[pl.BlockSpec(state_t.shape, lambda i: (0, 0, 0)),
                  pl.BlockSpec((b, d), lambda i: (0, 0)),
                  _layer_spec(layer, w_group.shape[1:], lambda i: (0, 0, 0)),
                  _layer_spec(layer, (1, d), lambda i: (0, 0))],
        out_specs=pl.BlockSpec((b, d), lambda i: (0, 0)),
        out_shape=jax.ShapeDtypeStruct((b, d), F32),
        compiler_params=_params("arbitrary"),
        name=name,
    )(state_t, u, w_group, scale)


def _mem_attn_kernel(q_ref, k_ref, v_ref, o_ref):
    rows = q_ref.shape[0]
    q = q_ref[...]
    if rows < MIN_ROWS:
        q = jnp.broadcast_to(q, (MIN_ROWS, q.shape[1]))
    q = q.astype(BF16)
    kb = k_ref[...].astype(BF16)
    vb = v_ref[...].astype(BF16)
    scale = MEM_HEAD_DIM ** -0.5
    outs = []
    for h in range(MEM_HEADS):
        sl = slice(h * MEM_HEAD_DIM, (h + 1) * MEM_HEAD_DIM)
        s = lax.dot_general(q[:, sl], kb[:, sl], NT_DIMS, preferred_element_type=F32) * scale
        p, _, _ = _softmax_rows(s)
        outs.append(jnp.dot(p.astype(BF16), vb[:, sl], preferred_element_type=F32))
    o = jnp.concatenate(outs, axis=1)
    o_ref[...] = o[:rows].astype(o_ref.dtype)


def mem_attention(q, mk, mv, layer, *, tm, out_dtype, name):
    n, length, dm = q.shape
    n_mem = mk.shape[2]
    tm = min(tm, length)
    kv_spec = _layer_spec(layer, (None, n_mem, dm), lambda b, i: (b, 0, 0))
    return pl.pallas_call(
        _mem_attn_kernel,
        grid=(n, length // tm),
        in_specs=[pl.BlockSpec((None, tm, dm), lambda b, i: (b, i, 0)), kv_spec, kv_spec],
        out_specs=pl.BlockSpec((None, tm, dm), lambda b, i: (b, i, 0)),
        out_shape=jax.ShapeDtypeStruct((n, length, dm), out_dtype),
        compiler_params=_params("parallel", "parallel"),
        name=name,
    )(q, mk, mv)


def _strided_rows(ref, start, dil):
    if dil == 1:
        return ref[pl.ds(start, DIL_BLOCK), :]
    return ref[pl.ds(start, DIL_BLOCK, stride=dil), :]


def _dil_prompt_kernel(qa_ref, qb_ref, k_ref, v_ref, o_ref, lse_ref, *, dil, seq):
    span = dil * DIL_BLOCK
    nb = seq // span
    qi = lax.broadcasted_iota(jnp.int32, (DIL_BLOCK, 2 * DIL_BLOCK), 0)
    kj = lax.broadcasted_iota(jnp.int32, (DIL_BLOCK, 2 * DIL_BLOCK), 1)
    dist = qi + DIL_BLOCK - kj
    in_window = (dist >= 0) & (dist <= DIL_BLOCK)
    causal = (lax.broadcasted_iota(jnp.int32, (DIL_BLOCK, DIL_BLOCK), 0)
              >= lax.broadcasted_iota(jnp.int32, (DIL_BLOCK, DIL_BLOCK), 1))
    lane = lax.broadcasted_iota(jnp.int32, (DIL_BLOCK, LANES), 1)
    scale = HEAD_DIM ** -0.5
    q_refs = (qa_ref, qb_ref)

    def store_rows(ref, lead, start, val):
        rows = pl.ds(start, DIL_BLOCK) if dil == 1 else pl.ds(start, DIL_BLOCK, stride=dil)
        ref[lead + (rows, slice(None))] = val

    def scores(jb):
        if nb > 1:
            r, b = jb // nb, jb % nb
            start = b * span + r
        else:
            b, start = 0, jb
        if dil == 1:
            start = pl.multiple_of(start, DIL_BLOCK)
        q2 = jnp.concatenate([_strided_rows(q, start, dil) for q in q_refs], axis=0).astype(BF16)
        k_cur = _strided_rows(k_ref, start, dil)
        v_cur = _strided_rows(v_ref, start, dil)
        if nb > 1:
            prev = jnp.maximum(b - 1, 0) * span + (start - b * span)
            if dil == 1:
                prev = pl.multiple_of(prev, DIL_BLOCK)
            kcat = jnp.concatenate([_strided_rows(k_ref, prev, dil), k_cur], axis=0).astype(BF16)
            vcat = jnp.concatenate([_strided_rows(v_ref, prev, dil), v_cur], axis=0).astype(BF16)
            first_key = jnp.where(b > 0, 0, DIL_BLOCK)
            valid = in_window & (kj >= first_key)
        else:
            kcat, vcat, valid = k_cur.astype(BF16), v_cur.astype(BF16), causal
        s = lax.dot_general(q2, kcat, NT_DIMS, preferred_element_type=F32) * scale
        return start, jnp.where(jnp.concatenate([valid] * KV_REP, axis=0), s, NEG), vcat

    def block_group(jg, carry):
        staged = [scores(jg * DIL_GROUP + i) for i in range(DIL_GROUP)]
        probs = [_softmax_rows(s) for _, s, _ in staged]
        outs = [jnp.dot(p.astype(BF16), vcat, preferred_element_type=F32)
                for (p, _, _), (_, _, vcat) in zip(probs, staged)]
        for (start, _, _), (_, m, den), o2 in zip(staged, probs, outs):
            lse2 = m + jnp.log(den)
            lse_tile = jnp.zeros((DIL_BLOCK, LANES), F32)
            for hh in range(KV_REP):
                rows = slice(hh * DIL_BLOCK, (hh + 1) * DIL_BLOCK)
                lse_tile = jnp.where(lane == hh, lse2[rows], lse_tile)
                store_rows(o_ref, (hh,), start, o2[rows])
            store_rows(lse_ref, (), start, lse_tile)
        return carry

    lax.fori_loop(0, seq // (DIL_BLOCK * DIL_GROUP), block_group, 0)


def dil_attention_prompt(q, k, v, group, dil, *, name):
    n, s, _ = q.shape
    qa = lambda b, j: (b, 0, group * HEADS_PER_GROUP + KV_REP * j)
    qb = lambda b, j: (b, 0, group * HEADS_PER_GROUP + KV_REP * j + 1)
    kv = lambda b, j: (b, 0, j)
    head_block = pl.BlockSpec((None, s, HEAD_DIM), kv)
    return pl.pallas_call(
        functools.partial(_dil_prompt_kernel, dil=dil, seq=s),
        grid=(n, N_KV_HEADS),
        in_specs=[pl.BlockSpec((None, s, HEAD_DIM), qa), pl.BlockSpec((None, s, HEAD_DIM), qb),
                  head_block, head_block],
        out_specs=[pl.BlockSpec((None, KV_REP, s, HEAD_DIM), lambda b, j: (b, j, 0, 0)),
                   pl.BlockSpec((None, None, s, LANES), lambda b, j: (b, j, 0, 0))],
        out_shape=[jax.ShapeDtypeStruct((n, HEADS_PER_GROUP, s, HEAD_DIM), F32),
                   jax.ShapeDtypeStruct((n, N_KV_HEADS, s, LANES), F32)],
        compiler_params=_params("parallel", "parallel"),
        name=name,
    )(q, q, k, v)


def _dil_merge_out_kernel(o0_ref, o1_ref, o2_ref, l0_ref, l1_ref, l2_ref, w_ref, x_ref, out_ref, mrg_ref):
    @pl.when(pl.program_id(1) == 0)
    def _():
        o_refs = (o0_ref, o1_ref, o2_ref)
        for kv in range(N_KV_HEADS):
            lses = [l0_ref[kv], l1_ref[kv], l2_ref[kv]]
            m = jnp.maximum(jnp.maximum(lses[0], lses[1]), lses[2])
            es = [jnp.exp(l - m) for l in lses]
            den = es[0] + es[1] + es[2]
            ws = [e / den for e in es]
            for hh in range(KV_REP):
                h = kv * KV_REP + hh
                acc = ws[0][:, hh:hh + 1] * o_refs[0][h]
                for gi in (1, 2):
                    acc = acc + ws[gi][:, hh:hh + 1] * o_refs[gi][h]
                mrg_ref[:, h * HEAD_DIM:(h + 1) * HEAD_DIM] = acc.astype(mrg_ref.dtype)

    out_ref[...] = x_ref[...] + jnp.dot(mrg_ref[...], w_ref[...].astype(BF16), preferred_element_type=F32)


def dil_merge_out(os_, lses, w, x, *, seq, tm, tn, name):
    m, d = x.shape
    w, lw = w
    od = HEADS_PER_GROUP * HEAD_DIM
    tps = seq // tm
    o_spec = pl.BlockSpec((None, HEADS_PER_GROUP, tm, HEAD_DIM), lambda i, j: (i // tps, 0, i % tps, 0))
    l_spec = pl.BlockSpec((None, N_KV_HEADS, tm, LANES), lambda i, j: (i // tps, 0, i % tps, 0))
    return pl.pallas_call(
        _dil_merge_out_kernel,
        grid=(m // tm, d // tn),
        in_specs=[o_spec] * 3 + [l_spec] * 3
                 + [_layer_spec(lw, (od, tn), lambda i, j: (0, j)), pl.BlockSpec((tm, tn), lambda i, j: (i, j))],
        out_specs=pl.BlockSpec((tm, tn), lambda i, j: (i, j)),
        out_shape=jax.ShapeDtypeStruct((m, d), F32),
        scratch_shapes=[pltpu.VMEM((tm, od), BF16)],
        compiler_params=_params("parallel", "arbitrary"),
        name=name,
    )(*os_, *lses, w, x)


def _dil_sample_kernel(q_ref, kn_ref, vn_ref, *refs, wbuf):
    k_refs, v_refs, o_ref = refs[:N_KV_HEADS], refs[N_KV_HEADS:2 * N_KV_HEADS], refs[2 * N_KV_HEADS]
    scale = HEAD_DIM ** -0.5
    head_row = lax.broadcasted_iota(jnp.int32, (HEADS_PER_GROUP, 1), 0)
    kn = kn_ref[...].astype(BF16).astype(F32)
    vn = vn_ref[...].astype(BF16).astype(F32)
    outs, lses = [], []
    for gi, (window, dil) in enumerate(DIL_PATTERNS):
        qg = q_ref[gi * HEADS_PER_GROUP:(gi + 1) * HEADS_PER_GROUP, :].astype(BF16)
        qf = qg.astype(F32)
        first = wbuf - window
        s = jnp.zeros((HEADS_PER_GROUP, DIL_BLOCK), F32)
        s_new = jnp.zeros((HEADS_PER_GROUP, 1), F32)
        for kv in range(N_KV_HEADS):
            mine = (head_row >= kv * KV_REP) & (head_row < (kv + 1) * KV_REP)
            kc = _strided_rows(k_refs[kv], first, dil).astype(BF16)
            s = jnp.where(mine, lax.dot_general(qg, kc, NT_DIMS, preferred_element_type=F32), s)
            s_new = jnp.where(mine, jnp.sum(qf * kn[kv:kv + 1, :], axis=-1, keepdims=True), s_new)
        s = s * scale
        s_new = s_new * scale
        m = jnp.maximum(jnp.max(s, axis=-1, keepdims=True), s_new)
        e = jnp.exp(s - m)
        e_new = jnp.exp(s_new - m)
        den = jnp.sum(e, axis=-1, keepdims=True) + e_new
        p = (e / den).astype(BF16)
        p_new = (e_new / den).astype(BF16).astype(F32)
        o = jnp.zeros((HEADS_PER_GROUP, HEAD_DIM), F32)
        for kv in range(N_KV_HEADS):
            mine = (head_row >= kv * KV_REP) & (head_row < (kv + 1) * KV_REP)
            vc = _strided_rows(v_refs[kv], first, dil).astype(BF16)
            o_kv = jnp.dot(p, vc, preferred_element_type=F32) + p_new * vn[kv:kv + 1, :]
            o = jnp.where(mine, o_kv, o)
        outs.append(o)
        lses.append(m + jnp.log(den))
    mm = jnp.maximum(jnp.maximum(lses[0], lses[1]), lses[2])
    es = [jnp.exp(l - mm) for l in lses]
    den = es[0] + es[1] + es[2]
    o_ref[...] = (es[0] / den) * outs[0] + (es[1] / den) * outs[1] + (es[2] / den) * outs[2]


def dil_attention_sample(q, k_new, v_new, cache_k, cache_v, *, name):
    b, qd = q.shape
    wbuf = cache_k.shape[1]
    n_heads = qd // HEAD_DIM
    for window, dil in DIL_PATTERNS:
        assert window // dil == DIL_BLOCK and window <= wbuf
    small = lambda rows: pl.BlockSpec((None, rows, HEAD_DIM), lambda i: (i, 0, 0))
    cache_specs = [pl.BlockSpec((None, wbuf, HEAD_DIM), functools.partial(lambda i, kv: (i, 0, kv), kv=kv))
                   for kv in range(N_KV_HEADS)]
    o = pl.pallas_call(
        functools.partial(_dil_sample_kernel, wbuf=wbuf),
        grid=(b,),
        in_specs=[small(n_heads), small(N_KV_HEADS), small(N_KV_HEADS)] + cache_specs + cache_specs,
        out_specs=small(HEADS_PER_GROUP),
        out_shape=jax.ShapeDtypeStruct((b, HEADS_PER_GROUP, HEAD_DIM), F32),
        compiler_params=_params("parallel"),
        name=name,
    )(q.reshape(b, n_heads, HEAD_DIM), k_new.reshape(b, N_KV_HEADS, HEAD_DIM),
      v_new.reshape(b, N_KV_HEADS, HEAD_DIM), *([cache_k] * N_KV_HEADS), *([cache_v] * N_KV_HEADS))
    return o.reshape(b, HEADS_PER_GROUP * HEAD_DIM)


def _trunk(x, mem_k, mem_v, w, *, n_seq, seq, pool_state, win_k, win_v, tag):
    depth = w["ffn_w_gate"].shape[0]
    n_a = w["pool_w_in"].shape[0]
    m, d = x.shape
    sample = pool_state is not None
    new_pool = []
    k_new = v_new = None
    for l in range(depth):
        if l == n_a:
            k_new, v_new = norm_matmul(x, (w["norm_kv"], 0), [(w["w_k_shared"], 0), (w["w_v_shared"], 0)],
                                       [F32, F32], tm=1024, tn=512, name=f"{tag}_kv")
        if l < n_a:
            (u,) = norm_matmul(x, (w["norm_mix"], l), [(w["pool_w_in"], l)], [F32], tm=1024, tn=512,
                               name=f"{tag}_pool_in{l}")
            if sample:
                state = pool_state[l]
                new_pool.append(jnp.concatenate([state[:, 1:], u[:, None, :]], axis=1))
                z = pool_sample(jnp.swapaxes(state, 0, 1), u, w["pool_w_group"], w["pool_scale"], l,
                                name=f"{tag}_pool{l}")
            else:
                new_pool.append(u.reshape(n_seq, seq, d)[:, seq - POOL_STATE:])
                z = pool_prompt(u, w["pool_w_group"], w["pool_scale"], l, seq=seq, tm=512,
                                name=f"{tag}_pool{l}")
            x = matmul_residual(z, (w["pool_w_out"], l), x, tm=1024, tn=1024, name=f"{tag}_pool_out{l}")
        else:
            j = l - n_a
            (q,) = norm_matmul(x, (w["norm_mix"], l), [(w["dil_w_q"], j)], [F32], tm=1024, tn=512,
                               name=f"{tag}_dil_q{j}")
            if sample:
                o = dil_attention_sample(q, k_new, v_new, win_k, win_v, name=f"{tag}_dil_attn{j}")
                x = matmul_residual(o, (w["dil_w_o"], j), x, tm=1024, tn=1024, name=f"{tag}_dil_out{j}")
            else:
                q3 = q.reshape(n_seq, seq, -1)
                k3 = k_new.reshape(n_seq, seq, -1)
                v3 = v_new.reshape(n_seq, seq, -1)
                os_, lses = [], []
                for gi, (_, dil) in enumerate(DIL_PATTERNS):
                    o_g, lse_g = dil_attention_prompt(q3, k3, v3, gi, dil, name=f"{tag}_dil_attn{j}_{gi}")
                    os_.append(o_g)
                    lses.append(lse_g)
                x = dil_merge_out(os_, lses, (w["dil_w_o"], j), x, seq=seq, tm=512, tn=1024,
                                  name=f"{tag}_dil_out{j}")
        (q,) = norm_matmul(x, (w["norm_mem_q"], l), [(w["mem_w_q"], l)], [F32 if sample else BF16],
                           tm=1024, tn=512, name=f"{tag}_mem_q{l}")
        o = mem_attention(q.reshape(n_seq, seq, -1), mem_k, mem_v, l, tm=512,
                          out_dtype=F32 if sample else BF16, name=f"{tag}_mem_attn{l}")
        x = matmul_residual(o.reshape(m, -1), (w["mem_w_o"], l), x, tm=1024, tn=1024, name=f"{tag}_mem_out{l}")
        x = ffn(x, (w["norm_ffn"], l), w["ffn_w_gate"], w["ffn_w_up"], w["ffn_w_down"], l,
                tm=512, tf=512, name=f"{tag}_ffn{l}")
    y = rmsnorm(x, w["norm_final"], tm=512, name=f"{tag}_final")
    return y, jnp.stack(new_pool), k_new, v_new


def kernel(x_prompt, x_sample, mem_prompt, state_pool, cache_win_k, cache_win_v, cache_mem_k, cache_mem_v,
           norm_mix, norm_mem_q, norm_mem_kv, norm_ffn, pool_w_in, pool_w_group, pool_scale, pool_w_out,
           norm_kv, w_k_shared, w_v_shared, dil_w_q, dil_w_o, mem_w_q, mem_w_k, mem_w_v, mem_w_o,
           ffn_w_gate, ffn_w_up, ffn_w_down, norm_final):
    batch, seq, d = x_prompt.shape
    dec_batch, dec_seq, _ = x_sample.shape
    depth = norm_mix.shape[0]
    n_mem = mem_prompt.shape[1]
    rows3 = lambda a: a.reshape(a.shape[0], 1, a.shape[1])
    w = dict(norm_mix=rows3(norm_mix), norm_mem_q=rows3(norm_mem_q), norm_ffn=rows3(norm_ffn),
             norm_kv=norm_kv.reshape(1, 1, d), norm_final=norm_final, pool_scale=rows3(pool_scale),
             pool_w_in=pool_w_in, pool_w_group=pool_w_group, pool_w_out=pool_w_out,
             w_k_shared=w_k_shared[None], w_v_shared=w_v_shared[None], dil_w_q=dil_w_q, dil_w_o=dil_w_o,
             mem_w_q=mem_w_q, mem_w_o=mem_w_o, ffn_w_gate=ffn_w_gate.astype(BF16), ffn_w_up=ffn_w_up.astype(BF16),
             ffn_w_down=ffn_w_down.astype(BF16))

    mem_rows = mem_prompt.reshape(batch * n_mem, d)
    norm_mem_kv3 = rows3(norm_mem_kv)
    mk_p, mv_p = [], []
    for l in range(depth):
        mk, mv = norm_matmul(mem_rows, (norm_mem_kv3, l), [(mem_w_k, l), (mem_w_v, l)], [F32, F32],
                             tm=1024, tn=512, name=f"mem_kv{l}")
        mk_p.append(mk.reshape(batch, n_mem, -1))
        mv_p.append(mv.reshape(batch, n_mem, -1))
    mk_p, mv_p = jnp.stack(mk_p), jnp.stack(mv_p)

    y_p, pool_p, k_p, v_p = _trunk(x_prompt.reshape(batch * seq, d), mk_p, mv_p, w, n_seq=batch, seq=seq,
                                   pool_state=None, win_k=None, win_v=None, tag="p")
    mk_s = cache_mem_k.reshape(depth, dec_batch, n_mem, -1)
    mv_s = cache_mem_v.reshape(depth, dec_batch, n_mem, -1)
    wbuf = cache_win_k.shape[1]
    y_s, pool_s, k_s, v_s = _trunk(x_sample.reshape(dec_batch * dec_seq, d), mk_s, mv_s, w, n_seq=dec_batch,
                                   seq=dec_seq, pool_state=state_pool,
                                   win_k=cache_win_k.reshape(dec_batch, wbuf, -1),
                                   win_v=cache_win_v.reshape(dec_batch, wbuf, -1), tag="s")

    max_window = max(wd for wd, _ in DIL_PATTERNS)
    keep_from = max(0, seq - max_window)
    kv_shape = (N_KV_HEADS, HEAD_DIM)
    mem_shape = (depth, batch, n_mem, MEM_HEADS, MEM_HEAD_DIM)
    return (y_p.reshape(batch, seq, d), y_s.reshape(dec_batch, dec_seq, d), pool_p, pool_s,
            k_p.reshape(batch, seq, *kv_shape)[:, keep_from:], v_p.reshape(batch, seq, *kv_shape)[:, keep_from:],
            k_s.reshape(dec_batch, dec_seq, *kv_shape), v_s.reshape(dec_batch, dec_seq, *kv_shape),
            mk_p.reshape(mem_shape), mv_p.reshape(mem_shape))
```

```python
import functools

import jax
import jax.numpy as jnp
from jax import lax
from jax.experimental import pallas as pl
from jax.experimental.pallas import tpu as pltpu

F32 = jnp.float32
BF16 = jnp.bfloat16

RMS_EPS = 1e-6
POOL_WINDOWS = (2, 4, 8, 16)
POOL_STATE = max(POOL_WINDOWS) - 1
POOL_HALO = 16
POOL_CHUNK = 64
DIL_PATTERNS = ((128, 1), (512, 4), (2048, 16))
DIL_BLOCK = 128
DIL_GROUP = 4
MERGE_ROWS = 256
HEAD_DIM = 128
N_KV_HEADS = 4
HEADS_PER_GROUP = 8
KV_REP = HEADS_PER_GROUP // N_KV_HEADS
MEM_HEADS = 4
MEM_HEAD_DIM = 128
LANES = 128
NEG = -1e30
VMEM_LIMIT = 56 * 1024 * 1024
MIN_ROWS = 8

NT_DIMS = (((1,), (1,)), ((), ()))


def _params(*sem):
    return pltpu.CompilerParams(dimension_semantics=sem, vmem_limit_bytes=VMEM_LIMIT)


def _layer_spec(layer, block, index_map):
    return pl.BlockSpec((None,) + tuple(block), lambda *g: (layer,) + tuple(index_map(*g)))


def _rms(x, g):
    ms = jnp.mean(x * x, axis=-1, keepdims=True)
    return x * lax.rsqrt(ms + RMS_EPS) * g


def _lhs_scratch_dtype(rows):
    return BF16 if rows % 16 == 0 else F32


def _softmax_rows(s):
    m = jnp.max(s, axis=-1, keepdims=True)
    e = jnp.exp(s - m)
    den = jnp.sum(e, axis=-1, keepdims=True)
    return e / den, m, den


def _norm_matmul_kernel(x_ref, g_ref, *refs, n_w):
    w_refs, o_refs, h_ref = refs[:n_w], refs[n_w:2 * n_w], refs[2 * n_w]

    @pl.when(pl.program_id(1) == 0)
    def _():
        h_ref[...] = _rms(x_ref[...], g_ref[...]).astype(h_ref.dtype)

    h = h_ref[...].astype(BF16)
    for w_ref, o_ref in zip(w_refs, o_refs):
        o_ref[...] = jnp.dot(h, w_ref[...].astype(BF16), preferred_element_type=F32).astype(o_ref.dtype)


def norm_matmul(x, g, ws, out_dtypes, *, tm, tn, name):
    m, d = x.shape
    gains, lg = g
    n = ws[0][0].shape[2]
    tm, tn = min(tm, m), min(tn, n)
    n_w = len(ws)
    return pl.pallas_call(
        functools.partial(_norm_matmul_kernel, n_w=n_w),
        grid=(m // tm, n // tn),
        in_specs=[pl.BlockSpec((tm, d), lambda i, j: (i, 0)),
                  _layer_spec(lg, (1, d), lambda i, j: (0, 0))]
                 + [_layer_spec(lw, (d, tn), lambda i, j: (0, j)) for _, lw in ws],
        out_specs=[pl.BlockSpec((tm, tn), lambda i, j: (i, j))] * n_w,
        out_shape=[jax.ShapeDtypeStruct((m, n), dt) for dt in out_dtypes],
        scratch_shapes=[pltpu.VMEM((tm, d), _lhs_scratch_dtype(tm))],
        compiler_params=_params("parallel", "arbitrary"),
        name=name,
    )(x, gains, *[w for w, _ in ws])


def _matmul_res_kernel(a_ref, w_ref, x_ref, o_ref):
    a = a_ref[...].astype(BF16)
    o_ref[...] = x_ref[...] + jnp.dot(a, w_ref[...].astype(BF16), preferred_element_type=F32)


def matmul_residual(a, w, x, *, tm, tn, name):
    m, k = a.shape
    w, lw = w
    n = w.shape[2]
    tm, tn = min(tm, m), min(tn, n)
    return pl.pallas_call(
        _matmul_res_kernel,
        grid=(m // tm, n // tn),
        in_specs=[pl.BlockSpec((tm, k), lambda i, j: (i, 0)),
                  _layer_spec(lw, (k, tn), lambda i, j: (0, j)),
                  pl.BlockSpec((tm, tn), lambda i, j: (i, j))],
        out_specs=pl.BlockSpec((tm, tn), lambda i, j: (i, j)),
        out_shape=jax.ShapeDtypeStruct((m, n), F32),
        compiler_params=_params("parallel", "parallel"),
        name=name,
    )(a, w, x)


def _ffn_kernel(x_ref, g_ref, wg_ref, wu_ref, wd_ref, *rest, final_norm):
    if final_norm:
        gf_ref, o_ref, h_ref = rest
    else:
        o_ref, h_ref = rest
    j = pl.program_id(1)

    @pl.when(j == 0)
    def _():
        x = x_ref[...]
        h_ref[...] = _rms(x, g_ref[...]).astype(h_ref.dtype)
        o_ref[...] = x

    h = h_ref[...].astype(BF16)
    gate = jnp.dot(h, wg_ref[...].astype(BF16), preferred_element_type=F32)
    up = jnp.dot(h, wu_ref[...].astype(BF16), preferred_element_type=F32)
    act = (gate * jax.nn.sigmoid(gate) * up).astype(BF16)
    o_ref[...] += jnp.dot(act, wd_ref[...].astype(BF16), preferred_element_type=F32)

    if final_norm:
        @pl.when(j == pl.num_programs(1) - 1)
        def _():
            o_ref[...] = _rms(o_ref[...], gf_ref[...])


def ffn(x, g, w_gate, w_up, w_down, layer, *, tm, tf, name, final_gain=None):
    m, d = x.shape
    gains, lg = g
    f = w_gate.shape[2]
    tm = min(tm, m)
    final_norm = final_gain is not None
    extra_specs = [pl.BlockSpec((1, d), lambda i, j: (0, 0))] if final_norm else []
    extra_args = [final_gain.reshape(1, d)] if final_norm else []
    return pl.pallas_call(
        functools.partial(_ffn_kernel, final_norm=final_norm),
        grid=(m // tm, f // tf),
        in_specs=[pl.BlockSpec((tm, d), lambda i, j: (i, 0)),
                  _layer_spec(lg, (1, d), lambda i, j: (0, 0)),
                  _layer_spec(layer, (d, tf), lambda i, j: (0, j)),
                  _layer_spec(layer, (d, tf), lambda i, j: (0, j)),
                  _layer_spec(layer, (tf, d), lambda i, j: (j, 0))] + extra_specs,
        out_specs=pl.BlockSpec((tm, d), lambda i, j: (i, 0)),
        out_shape=jax.ShapeDtypeStruct((m, d), F32),
        scratch_shapes=[pltpu.VMEM((tm, d), _lhs_scratch_dtype(tm))],
        compiler_params=_params("parallel", "arbitrary"),
        name=name,
    )(x, gains, w_gate, w_up, w_down, *extra_args)


def _pool_prompt_kernel(u_ref, uprev_ref, wg_ref, scale_ref, z_ref, ext_ref, p_ref, *, tm, tiles_per_seq):
    d = u_ref.shape[1]
    gdim = d // len(POOL_WINDOWS)
    t_in_seq = pl.program_id(0) % tiles_per_seq

    @pl.when(t_in_seq == 0)
    def _():
        ext_ref[0:POOL_HALO, :] = jnp.zeros((POOL_HALO, d), F32)

    @pl.when(t_in_seq != 0)
    def _():
        ext_ref[0:POOL_HALO, :] = uprev_ref[...]

    ext_ref[POOL_HALO:, :] = u_ref[...]
    for c in range(tm // POOL_CHUNK):
        r0 = POOL_HALO + c * POOL_CHUNK
        pos = t_in_seq * tm + c * POOL_CHUNK + lax.broadcasted_iota(jnp.int32, (POOL_CHUNK, 1), 0)
        for gi, w in enumerate(POOL_WINDOWS):
            cols = slice(gi * gdim, (gi + 1) * gdim)
            u_q = ext_ref[r0:r0 + POOL_CHUNK, cols]
            acc = u_q
            for k in range(1, w):
                acc = acc + ext_ref[r0 - k:r0 - k + POOL_CHUNK, cols]
            inv_cnt = 1.0 / jnp.minimum(pos + 1, w).astype(F32)
            p_ref[c * POOL_CHUNK:(c + 1) * POOL_CHUNK, cols] = (acc * inv_cnt - u_q).astype(BF16)
    for gi in range(len(POOL_WINDOWS)):
        cols = slice(gi * gdim, (gi + 1) * gdim)
        z = jnp.dot(p_ref[:, cols], wg_ref[gi].astype(BF16), preferred_element_type=F32)
        z_ref[:, cols] = (z * scale_ref[:, cols]).astype(z_ref.dtype)


def pool_prompt(u, w_group, scale, layer, *, seq, tm, name):
    m, d = u.shape
    tiles_per_seq = seq // tm
    halo_blocks = tm // POOL_HALO
    return pl.pallas_call(
        functools.partial(_pool_prompt_kernel, tm=tm, tiles_per_seq=tiles_per_seq),
        grid=(m // tm,),
        in_specs=[pl.BlockSpec((tm, d), lambda i: (i, 0)),
                  pl.BlockSpec((POOL_HALO, d), lambda i: (jnp.maximum(i * halo_blocks - 1, 0), 0)),
                  _layer_spec(layer, w_group.shape[1:], lambda i: (0, 0, 0)),
                  _layer_spec(layer, (1, d), lambda i: (0, 0))],
        out_specs=pl.BlockSpec((tm, d), lambda i: (i, 0)),
        out_shape=jax.ShapeDtypeStruct((m, d), BF16),
        scratch_shapes=[pltpu.VMEM((tm + POOL_HALO, d), F32), pltpu.VMEM((tm, d), BF16)],
        compiler_params=_params("parallel"),
        name=name,
    )(u, u, w_group, scale)


def _pool_sample_kernel(state_ref, u_ref, wg_ref, scale_ref, z_ref):
    n_state, _, d = state_ref.shape
    gdim = d // len(POOL_WINDOWS)
    u = u_ref[...]
    for gi, w in enumerate(POOL_WINDOWS):
        cols = slice(gi * gdim, (gi + 1) * gdim)
        u_q = u[:, cols]
        acc = u_q
        for k in range(1, w):
            acc = acc + state_ref[n_state - k, :, cols]
        cnt = float(min(n_state + 1, w))
        p = (acc / cnt - u_q).astype(BF16)
        z = jnp.dot(p, wg_ref[gi].astype(BF16), preferred_element_type=F32)
        z_ref[:, cols] = z * scale_ref[:, cols]


def pool_sample(state_t, u, w_group, scale, layer, *, name):
    b, d = u.shape
    return pl.pallas_call(
        _pool_sample_kernel,
        grid=(1,),
        in_specs=[pl.BlockSpec(state_t.shape, lambda i: (0, 0, 0)),
                  pl.BlockSpec((b, d), lambda i: (0, 0)),
                  _layer_spec(layer, w_group.shape[1:], lambda i: (0, 0, 0)),
                  _layer_spec(layer, (1, d), lambda i: (0, 0))],
        out_specs=pl.BlockSpec((b, d), lambda i: (0, 0)),
        out_shape=jax.ShapeDtypeStruct((b, d), F32),
        compiler_params=_params("arbitrary"),
        name=name,
    )(state_t, u, w_group, scale)


def _mem_block_kernel(x_ref, g_ref, wq_ref, k_ref, v_ref, wo_ref, o_ref):
    rows = x_ref.shape[0]
    x = x_ref[...]
    if rows < MIN_ROWS:
        x = jnp.broadcast_to(x, (MIN_ROWS, x.shape[1]))
    h = _rms(x, g_ref[...]).astype(BF16)
    q = jnp.dot(h, wq_ref[...].astype(BF16), preferred_element_type=F32).astype(BF16)
    kb = k_ref[...].astype(BF16)
    vb = v_ref[...].astype(BF16)
    scale = MEM_HEAD_DIM ** -0.5
    outs = []
    for hd in range(MEM_HEADS):
        sl = slice(hd * MEM_HEAD_DIM, (hd + 1) * MEM_HEAD_DIM)
        s = lax.dot_general(q[:, sl], kb[:, sl], NT_DIMS, preferred_element_type=F32) * scale
        p, _, _ = _softmax_rows(s)
        outs.append(jnp.dot(p.astype(BF16), vb[:, sl], preferred_element_type=F32).astype(BF16))
    o = jnp.concatenate(outs, axis=1)
    out = x + jnp.dot(o, wo_ref[...].astype(BF16), preferred_element_type=F32)
    o_ref[...] = out[:rows]


def mem_block(x, g, w_q, mk, mv, w_o, layer, *, tm, name):
    n, length, d = x.shape
    gains, lg = g
    n_mem, dm = mk.shape[2], mk.shape[3]
    tm = min(tm, length)
    kv_spec = _layer_spec(layer, (None, n_mem, dm), lambda b, i: (b, 0, 0))
    x_spec = pl.BlockSpec((None, tm, d), lambda b, i: (b, i, 0))
    return pl.pallas_call(
        _mem_block_kernel,
        grid=(n, length // tm),
        in_specs=[x_spec,
                  _layer_spec(lg, (1, d), lambda b, i: (0, 0)),
                  _layer_spec(layer, (d, dm), lambda b, i: (0, 0)),
                  kv_spec, kv_spec,
                  _layer_spec(layer, (dm, d), lambda b, i: (0, 0))],
        out_specs=x_spec,
        out_shape=jax.ShapeDtypeStruct((n, length, d), F32),
        compiler_params=_params("parallel", "parallel"),
        name=name,
    )(x, gains, w_q, mk, mv, w_o)


def _strided_rows(ref, start, dil):
    if dil == 1:
        return ref[pl.ds(start, DIL_BLOCK), :]
    return ref[pl.ds(start, DIL_BLOCK, stride=dil), :]


def _dil_prompt_kernel(qa_ref, qb_ref, k_ref, v_ref, o_ref, lse_ref, *, dil, seq):
    span = dil * DIL_BLOCK
    nb = seq // span
    qi = lax.broadcasted_iota(jnp.int32, (DIL_BLOCK, 2 * DIL_BLOCK), 0)
    kj = lax.broadcasted_iota(jnp.int32, (DIL_BLOCK, 2 * DIL_BLOCK), 1)
    dist = qi + DIL_BLOCK - kj
    in_window = (dist >= 0) & (dist <= DIL_BLOCK)
    causal = (lax.broadcasted_iota(jnp.int32, (DIL_BLOCK, DIL_BLOCK), 0)
              >= lax.broadcasted_iota(jnp.int32, (DIL_BLOCK, DIL_BLOCK), 1))
    lane = lax.broadcasted_iota(jnp.int32, (DIL_BLOCK, LANES), 1)
    scale = HEAD_DIM ** -0.5
    q_refs = (qa_ref, qb_ref)
    kv_head = pl.program_id(1)

    @pl.when(kv_head == 0)
    def _():
        lse_ref[...] = jnp.zeros(lse_ref.shape, F32)

    def store_rows(ref, lead, start, val):
        rows = pl.ds(start, DIL_BLOCK) if dil == 1 else pl.ds(start, DIL_BLOCK, stride=dil)
        ref[lead + (rows, slice(None))] = val

    def scores(jb):
        if nb > 1:
            r, b = jb // nb, jb % nb
            start = b * span + r
        else:
            b, start = 0, jb
        if dil == 1:
            start = pl.multiple_of(start, DIL_BLOCK)
        q2 = jnp.concatenate([_strided_rows(q, start, dil) for q in q_refs], axis=0).astype(BF16)
        k_cur = _strided_rows(k_ref, start, dil)
        v_cur = _strided_rows(v_ref, start, dil)
        if nb > 1:
            prev = jnp.maximum(b - 1, 0) * span + (start - b * span)
            if dil == 1:
                prev = pl.multiple_of(prev, DIL_BLOCK)
            kcat = jnp.concatenate([_strided_rows(k_ref, prev, dil), k_cur], axis=0).astype(BF16)
            vcat = jnp.concatenate([_strided_rows(v_ref, prev, dil), v_cur], axis=0).astype(BF16)
            first_key = jnp.where(b > 0, 0, DIL_BLOCK)
            valid = in_window & (kj >= first_key)
        else:
            kcat, vcat, valid = k_cur.astype(BF16), v_cur.astype(BF16), causal
        s = lax.dot_general(q2, kcat, NT_DIMS, preferred_element_type=F32) * scale
        return start, jnp.where(jnp.concatenate([valid] * KV_REP, axis=0), s, NEG), vcat

    def block_group(jg, carry):
        staged = [scores(jg * DIL_GROUP + i) for i in range(DIL_GROUP)]
        lse_old = [_strided_rows(lse_ref, start, dil) for start, _, _ in staged]
        probs = [_softmax_rows(s) for _, s, _ in staged]
        outs = [jnp.dot(p.astype(BF16), vcat, preferred_element_type=F32)
                for (p, _, _), (_, _, vcat) in zip(probs, staged)]
        for (start, _, _), (_, m, den), o2, lse_tile in zip(staged, probs, outs, lse_old):
            lse2 = m + jnp.log(den)
            for hh in range(KV_REP):
                rows = slice(hh * DIL_BLOCK, (hh + 1) * DIL_BLOCK)
                lse_tile = jnp.where(lane == kv_head * KV_REP + hh, lse2[rows], lse_tile)
                store_rows(o_ref, (hh,), start, o2[rows])
            store_rows(lse_ref, (), start, lse_tile)
        return carry

    lax.fori_loop(0, seq // (DIL_BLOCK * DIL_GROUP), block_group, 0)


def dil_attention_prompt(q, k, v, group, dil, *, name):
    n, s, _ = q.shape
    qa = lambda b, j: (b, 0, group * HEADS_PER_GROUP + KV_REP * j)
    qb = lambda b, j: (b, 0, group * HEADS_PER_GROUP + KV_REP * j + 1)
    kv = lambda b, j: (b, 0, j)
    head_block = pl.BlockSpec((None, s, HEAD_DIM), kv)
    return pl.pallas_call(
        functools.partial(_dil_prompt_kernel, dil=dil, seq=s),
        grid=(n, N_KV_HEADS),
        in_specs=[pl.BlockSpec((None, s, HEAD_DIM), qa), pl.BlockSpec((None, s, HEAD_DIM), qb),
                  head_block, head_block],
        out_specs=[pl.BlockSpec((None, KV_REP, s, HEAD_DIM), lambda b, j: (b, j, 0, 0)),
                   pl.BlockSpec((None, s, LANES), lambda b, j: (b, 0, 0))],
        out_shape=[jax.ShapeDtypeStruct((n, HEADS_PER_GROUP, s, HEAD_DIM), F32),
                   jax.ShapeDtypeStruct((n, s, LANES), F32)],
        compiler_params=_params("parallel", "arbitrary"),
        name=name,
    )(q, q, k, v)


def _dil_merge_out_kernel(o0_ref, o1_ref, o2_ref, l0_ref, l1_ref, l2_ref, w_ref, x_ref, out_ref):
    o_refs = (o0_ref, o1_ref, o2_ref)
    l_refs = (l0_ref, l1_ref, l2_ref)
    tm = x_ref.shape[0]
    chunks = [slice(c * MERGE_ROWS, (c + 1) * MERGE_ROWS) for c in range(tm // MERGE_ROWS)]
    merged = []
    for rows in chunks:
        lses = [l_ref[rows, :] for l_ref in l_refs]
        m = jnp.maximum(jnp.maximum(lses[0], lses[1]), lses[2])
        es = [jnp.exp(l - m) for l in lses]
        den = es[0] + es[1] + es[2]
        ws = [e / den for e in es]
        heads = []
        for h in range(HEADS_PER_GROUP):
            acc = ws[0][:, h:h + 1] * o_refs[0][h, rows, :]
            for gi in (1, 2):
                acc = acc + ws[gi][:, h:h + 1] * o_refs[gi][h, rows, :]
            heads.append(acc.astype(BF16))
        merged.append(jnp.concatenate(heads, axis=1))
    w = w_ref[...]
    for rows, mrg in zip(chunks, merged):
        out_ref[rows, :] = x_ref[rows, :] + jnp.dot(mrg, w, preferred_element_type=F32)


def dil_merge_out(os_, lses, w, x, *, seq, tm, name):
    m, d = x.shape
    w, lw = w
    od = HEADS_PER_GROUP * HEAD_DIM
    tps = seq // tm
    o_spec = pl.BlockSpec((None, HEADS_PER_GROUP, tm, HEAD_DIM), lambda i: (i // tps, 0, i % tps, 0))
    l_spec = pl.BlockSpec((None, tm, LANES), lambda i: (i // tps, i % tps, 0))
    return pl.pallas_call(
        _dil_merge_out_kernel,
        grid=(m // tm,),
        in_specs=[o_spec] * 3 + [l_spec] * 3
                 + [_layer_spec(lw, (od, d), lambda i: (0, 0)), pl.BlockSpec((tm, d), lambda i: (i, 0))],
        out_specs=pl.BlockSpec((tm, d), lambda i: (i, 0)),
        out_shape=jax.ShapeDtypeStruct((m, d), F32),
        compiler_params=_params("parallel"),
        name=name,
    )(*os_, *lses, w, x)


def _dil_sample_kernel(q_ref, kn_ref, vn_ref, *refs, wbuf):
    k_refs, v_refs, o_ref = refs[:N_KV_HEADS], refs[N_KV_HEADS:2 * N_KV_HEADS], refs[2 * N_KV_HEADS]
    scale = HEAD_DIM ** -0.5
    head_row = lax.broadcasted_iota(jnp.int32, (HEADS_PER_GROUP, 1), 0)
    kn = kn_ref[...].astype(BF16).astype(F32)
    vn = vn_ref[...].astype(BF16).astype(F32)
    outs, lses = [], []
    for gi, (window, dil) in enumerate(DIL_PATTERNS):
        qg = q_ref[gi * HEADS_PER_GROUP:(gi + 1) * HEADS_PER_GROUP, :].astype(BF16)
        qf = qg.astype(F32)
        first = wbuf - window
        s = jnp.zeros((HEADS_PER_GROUP, DIL_BLOCK), F32)
        s_new = jnp.zeros((HEADS_PER_GROUP, 1), F32)
        for kv in range(N_KV_HEADS):
            mine = (head_row >= kv * KV_REP) & (head_row < (kv + 1) * KV_REP)
            kc = _strided_rows(k_refs[kv], first, dil).astype(BF16)
            s = jnp.where(mine, lax.dot_general(qg, kc, NT_DIMS, preferred_element_type=F32), s)
            s_new = jnp.where(mine, jnp.sum(qf * kn[kv:kv + 1, :], axis=-1, keepdims=True), s_new)
        s = s * scale
        s_new = s_new * scale
        m = jnp.maximum(jnp.max(s, axis=-1, keepdims=True), s_new)
        e = jnp.exp(s - m)
        e_new = jnp.exp(s_new - m)
        den = jnp.sum(e, axis=-1, keepdims=True) + e_new
        p = (e / den).astype(BF16)
        p_new = (e_new / den).astype(BF16).astype(F32)
        o = jnp.zeros((HEADS_PER_GROUP, HEAD_DIM), F32)
        for kv in range(N_KV_HEADS):
            mine = (head_row >= kv * KV_REP) & (head_row < (kv + 1) * KV_REP)
            vc = _strided_rows(v_refs[kv], first, dil).astype(BF16)
            o_kv = jnp.dot(p, vc, preferred_element_type=F32) + p_new * vn[kv:kv + 1, :]
            o = jnp.where(mine, o_kv, o)
        outs.append(o)
        lses.append(m + jnp.log(den))
    mm = jnp.maximum(jnp.maximum(lses[0], lses[1]), lses[2])
    es = [jnp.exp(l - mm) for l in lses]
    den = es[0] + es[1] + es[2]
    o_ref[...] = (es[0] / den) * outs[0] + (es[1] / den) * outs[1] + (es[2] / den) * outs[2]


def dil_attention_sample(q, k_new, v_new, cache_k, cache_v, *, name):
    b, qd = q.shape
    wbuf = cache_k.shape[1]
    n_heads = qd // HEAD_DIM
    for window, dil in DIL_PATTERNS:
        assert window // dil == DIL_BLOCK and window <= wbuf
    small = lambda rows: pl.BlockSpec((None, rows, HEAD_DIM), lambda i: (i, 0, 0))
    cache_specs = [pl.BlockSpec((None, wbuf, HEAD_DIM), functools.partial(lambda i, kv: (i, 0, kv), kv=kv))
                   for kv in range(N_KV_HEADS)]
    o = pl.pallas_call(
        functools.partial(_dil_sample_kernel, wbuf=wbuf),
        grid=(b,),
        in_specs=[small(n_heads), small(N_KV_HEADS), small(N_KV_HEADS)] + cache_specs + cache_specs,
        out_specs=small(HEADS_PER_GROUP),
        out_shape=jax.ShapeDtypeStruct((b, HEADS_PER_GROUP, HEAD_DIM), F32),
        compiler_params=_params("parallel"),
        name=name,
    )(q.reshape(b, n_heads, HEAD_DIM), k_new.reshape(b, N_KV_HEADS, HEAD_DIM),
      v_new.reshape(b, N_KV_HEADS, HEAD_DIM), *([cache_k] * N_KV_HEADS), *([cache_v] * N_KV_HEADS))
    return o.reshape(b, HEADS_PER_GROUP * HEAD_DIM)


def _trunk(x, mem_k, mem_v, w, *, n_seq, seq, pool_state, win_k, win_v, tag):
    depth = w["ffn_w_gate"].shape[0]
    n_a = w["pool_w_in"].shape[0]
    m, d = x.shape
    sample = pool_state is not None
    new_pool = []
    k_new = v_new = None
    for l in range(depth):
        if l == n_a:
            k_new, v_new = norm_matmul(x, (w["norm_kv"], 0), [(w["w_k_shared"], 0), (w["w_v_shared"], 0)],
                                       [F32, F32], tm=1024, tn=512, name=f"{tag}_kv")
        if l < n_a:
            (u,) = norm_matmul(x, (w["norm_mix"], l), [(w["pool_w_in"], l)], [F32], tm=1024, tn=512,
                               name=f"{tag}_pool_in{l}")
            if sample:
                state = pool_state[l]
                new_pool.append(jnp.concatenate([state[:, 1:], u[:, None, :]], axis=1))
                z = pool_sample(jnp.swapaxes(state, 0, 1), u, w["pool_w_group"], w["pool_scale"], l,
                                name=f"{tag}_pool{l}")
            else:
                new_pool.append(u.reshape(n_seq, seq, d)[:, seq - POOL_STATE:])
                z = pool_prompt(u, w["pool_w_group"], w["pool_scale"], l, seq=seq, tm=512,
                                name=f"{tag}_pool{l}")
            x = matmul_residual(z, (w["pool_w_out"], l), x, tm=1024, tn=1024, name=f"{tag}_pool_out{l}")
        else:
            j = l - n_a
            (q,) = norm_matmul(x, (w["norm_mix"], l), [(w["dil_w_q"], j)], [F32], tm=1024, tn=512,
                               name=f"{tag}_dil_q{j}")
            if sample:
                o = dil_attention_sample(q, k_new, v_new, win_k, win_v, name=f"{tag}_dil_attn{j}")
                x = matmul_residual(o, (w["dil_w_o"], j), x, tm=1024, tn=1024, name=f"{tag}_dil_out{j}")
            else:
                q3 = q.reshape(n_seq, seq, -1)
                k3 = k_new.reshape(n_seq, seq, -1)
                v3 = v_new.reshape(n_seq, seq, -1)
                os_, lses = [], []
                for gi, (_, dil) in enumerate(DIL_PATTERNS):
                    o_g, lse_g = dil_attention_prompt(q3, k3, v3, gi, dil, name=f"{tag}_dil_attn{j}_{gi}")
                    os_.append(o_g)
                    lses.append(lse_g)
                x = dil_merge_out(os_, lses, (w["dil_w_o"], j), x, seq=seq, tm=512, name=f"{tag}_dil_out{j}")
        x = mem_block(x.reshape(n_seq, seq, d), (w["norm_mem_q"], l), w["mem_w_q"], mem_k, mem_v, w["mem_w_o"], l,
                      tm=512, name=f"{tag}_mem{l}").reshape(m, d)
        x = ffn(x, (w["norm_ffn"], l), w["ffn_w_gate"], w["ffn_w_up"], w["ffn_w_down"], l,
                tm=1024, tf=256, name=f"{tag}_ffn{l}", final_gain=w["norm_final"] if l == depth - 1 else None)
    return x, jnp.stack(new_pool), k_new, v_new


def kernel(x_prompt, x_sample, mem_prompt, state_pool, cache_win_k, cache_win_v, cache_mem_k, cache_mem_v,
           norm_mix, norm_mem_q, norm_mem_kv, norm_ffn, pool_w_in, pool_w_group, pool_scale, pool_w_out,
           norm_kv, w_k_shared, w_v_shared, dil_w_q, dil_w_o, mem_w_q, mem_w_k, mem_w_v, mem_w_o,
           ffn_w_gate, ffn_w_up, ffn_w_down, norm_final):
    batch, seq, d = x_prompt.shape
    dec_batch, dec_seq, _ = x_sample.shape
    depth = norm_mix.shape[0]
    n_mem = mem_prompt.shape[1]
    rows3 = lambda a: a.reshape(a.shape[0], 1, a.shape[1])
    w = dict(norm_mix=rows3(norm_mix), norm_mem_q=rows3(norm_mem_q), norm_ffn=rows3(norm_ffn),
             norm_kv=norm_kv.reshape(1, 1, d), norm_final=norm_final, pool_scale=rows3(pool_scale),
             pool_w_in=pool_w_in, pool_w_group=pool_w_group, pool_w_out=pool_w_out,
             w_k_shared=w_k_shared[None], w_v_shared=w_v_shared[None], dil_w_q=dil_w_q,
             dil_w_o=dil_w_o.astype(BF16),
             mem_w_q=mem_w_q, mem_w_o=mem_w_o, ffn_w_gate=ffn_w_gate, ffn_w_up=ffn_w_up, ffn_w_down=ffn_w_down)

    mem_rows = mem_prompt.reshape(batch * n_mem, d)
    norm_mem_kv3 = rows3(norm_mem_kv)
    mk_p, mv_p = [], []
    for l in range(depth):
        mk, mv = norm_matmul(mem_rows, (norm_mem_kv3, l), [(mem_w_k, l), (mem_w_v, l)], [F32, F32],
                             tm=1024, tn=512, name=f"mem_kv{l}")
        mk_p.append(mk.reshape(batch, n_mem, -1))
        mv_p.append(mv.reshape(batch, n_mem, -1))
    mk_p, mv_p = jnp.stack(mk_p), jnp.stack(mv_p)

    y_p, pool_p, k_p, v_p = _trunk(x_prompt.reshape(batch * seq, d), mk_p, mv_p, w, n_seq=batch, seq=seq,
                                   pool_state=None, win_k=None, win_v=None, tag="p")
    mk_s = cache_mem_k.reshape(depth, dec_batch, n_mem, -1)
    mv_s = cache_mem_v.reshape(depth, dec_batch, n_mem, -1)
    wbuf = cache_win_k.shape[1]
    y_s, pool_s, k_s, v_s = _trunk(x_sample.reshape(dec_batch * dec_seq, d), mk_s, mv_s, w, n_seq=dec_batch,
                                   seq=dec_seq, pool_state=state_pool,
                                   win_k=cache_win_k.reshape(dec_batch, wbuf, -1),
                                   win_v=cache_win_v.reshape(dec_batch, wbuf, -1), tag="s")

    max_window = max(wd for wd, _ in DIL_PATTERNS)
    keep_from = max(0, seq - max_window)
    kv_shape = (N_KV_HEADS, HEAD_DIM)
    mem_shape = (depth, batch, n_mem, MEM_HEADS, MEM_HEAD_DIM)
    return (y_p.reshape(batch, seq, d), y_s.reshape(dec_batch, dec_seq, d), pool_p, pool_s,
            k_p.reshape(batch, seq, *kv_shape)[:, keep_from:], v_p.reshape(batch, seq, *kv_shape)[:, keep_from:],
            k_s.reshape(dec_batch, dec_seq, *kv_shape), v_s.reshape(dec_batch, dec_seq, *kv_shape),
            mk_p.reshape(mem_shape), mv_p.reshape(mem_shape))
```

```python
import functools

import jax
import jax.numpy as jnp
from jax import lax
from jax.experimental import pallas as pl
from jax.experimental.pallas import tpu as pltpu

F32 = jnp.float32
BF16 = jnp.bfloat16

RMS_EPS = 1e-6
POOL_WINDOWS = (2, 4, 8, 16)
POOL_STATE = max(POOL_WINDOWS) - 1
POOL_HALO = 16
POOL_CHUNK = 64
POOL_TM = 512
EXTRA_ROWS = 16
DIL_PATTERNS = ((128, 1), (512, 4), (2048, 16))
DIL_BLOCK = 128
DIL_GROUP = 4
MERGE_ROWS = 256
HEAD_DIM = 128
N_KV_HEADS = 4
HEADS_PER_GROUP = 8
KV_REP = HEADS_PER_GROUP // N_KV_HEADS
MEM_HEADS = 4
MEM_HEAD_DIM = 128
LANES = 128
NEG = -1e30
VMEM_LIMIT = 56 * 1024 * 1024
MIN_ROWS = 8

NT_DIMS = (((1,), (1,)), ((), ()))


def _params(*sem):
    return pltpu.CompilerParams(dimension_semantics=sem, vmem_limit_bytes=VMEM_LIMIT)


def _layer_spec(layer, block, index_map):
    return pl.BlockSpec((None,) + tuple(block), lambda *g: (layer,) + tuple(index_map(*g)))


def _resident_spec(layer, block):
    zeros = (0,) * len(block)
    return pl.BlockSpec((None,) + tuple(block), lambda *g: (layer,) + zeros, pipeline_mode=pl.Buffered(1))


def _rms(x, g):
    ms = jnp.mean(x * x, axis=-1, keepdims=True)
    return x * lax.rsqrt(ms + RMS_EPS) * g


def _lhs_scratch_dtype(rows):
    return BF16 if rows % 16 == 0 else F32


def _softmax_rows(s):
    m = jnp.max(s, axis=-1, keepdims=True)
    e = jnp.exp(s - m)
    den = jnp.sum(e, axis=-1, keepdims=True)
    return e / den, m, den


def _norm_matmul_kernel(x_ref, g_ref, *refs, n_w):
    w_refs, o_refs, h_ref = refs[:n_w], refs[n_w:2 * n_w], refs[2 * n_w]

    @pl.when(pl.program_id(1) == 0)
    def _():
        h_ref[...] = _rms(x_ref[...], g_ref[...]).astype(h_ref.dtype)

    h = h_ref[...].astype(BF16)
    for w_ref, o_ref in zip(w_refs, o_refs):
        o_ref[...] = jnp.dot(h, w_ref[...].astype(BF16), preferred_element_type=F32).astype(o_ref.dtype)


def norm_matmul(x, g, ws, out_dtypes, *, tm, tn, name):
    m, d = x.shape
    gains, lg = g
    n = ws[0][0].shape[2]
    tm, tn = min(tm, m), min(tn, n)
    n_w = len(ws)
    return pl.pallas_call(
        functools.partial(_norm_matmul_kernel, n_w=n_w),
        grid=(m // tm, n // tn),
        in_specs=[pl.BlockSpec((tm, d), lambda i, j: (i, 0)),
                  _layer_spec(lg, (1, d), lambda i, j: (0, 0))]
                 + [_layer_spec(lw, (d, tn), lambda i, j: (0, j)) for _, lw in ws],
        out_specs=[pl.BlockSpec((tm, tn), lambda i, j: (i, j))] * n_w,
        out_shape=[jax.ShapeDtypeStruct((m, n), dt) for dt in out_dtypes],
        scratch_shapes=[pltpu.VMEM((tm, d), _lhs_scratch_dtype(tm))],
        compiler_params=_params("parallel", "arbitrary"),
        name=name,
    )(x, gains, *[w for w, _ in ws])


def _norm_matmul_resident_kernel(x_ref, g_ref, w_ref, o_ref):
    h = _rms(x_ref[...], g_ref[...]).astype(BF16)
    o_ref[...] = jnp.dot(h, w_ref[...], preferred_element_type=F32).astype(o_ref.dtype)


def norm_matmul_resident(x, g, w, out_dtype, *, tm, name):
    m, d = x.shape
    gains, lg = g
    w, lw = w
    n = w.shape[2]
    return pl.pallas_call(
        _norm_matmul_resident_kernel,
        grid=(m // tm,),
        in_specs=[pl.BlockSpec((tm, d), lambda i: (i, 0)),
                  _layer_spec(lg, (1, d), lambda i: (0, 0)),
                  _resident_spec(lw, (d, n))],
        out_specs=pl.BlockSpec((tm, n), lambda i: (i, 0)),
        out_shape=jax.ShapeDtypeStruct((m, n), out_dtype),
        compiler_params=_params("parallel"),
        name=name,
    )(x, gains, w)


def _matmul_res_kernel(a_ref, w_ref, x_ref, o_ref):
    a = a_ref[...].astype(BF16)
    o_ref[...] = x_ref[...] + jnp.dot(a, w_ref[...].astype(BF16), preferred_element_type=F32)


def matmul_residual(a, w, x, *, tm, tn, name):
    m, k = a.shape
    w, lw = w
    n = w.shape[2]
    tm, tn = min(tm, m), min(tn, n)
    return pl.pallas_call(
        _matmul_res_kernel,
        grid=(m // tm, n // tn),
        in_specs=[pl.BlockSpec((tm, k), lambda i, j: (i, 0)),
                  _layer_spec(lw, (k, tn), lambda i, j: (0, j)),
                  pl.BlockSpec((tm, tn), lambda i, j: (i, j))],
        out_specs=pl.BlockSpec((tm, tn), lambda i, j: (i, j)),
        out_shape=jax.ShapeDtypeStruct((m, n), F32),
        compiler_params=_params("parallel", "parallel"),
        name=name,
    )(a, w, x)


def _ffn_kernel(x_ref, xs_ref, g_ref, wg_ref, wu_ref, wd_ref, *rest, final_norm):
    if final_norm:
        gf_ref, o_ref, os_ref, h_ref = rest
    else:
        o_ref, os_ref, h_ref = rest
    i, j = pl.program_id(0), pl.program_id(1)
    tm = x_ref.shape[0]
    first_tile = i == 0

    @pl.when(j == 0)
    def _():
        x = x_ref[...]
        h_ref[0:tm, :] = _rms(x, g_ref[...]).astype(BF16)
        o_ref[...] = x

    @pl.when((j == 0) & first_tile)
    def _():
        xs = xs_ref[...]
        h_ref[tm:, :] = _rms(xs, g_ref[...]).astype(BF16)
        os_ref[...] = xs

    h = h_ref[...]
    gate = jnp.dot(h, wg_ref[...].astype(BF16), preferred_element_type=F32)
    up = jnp.dot(h, wu_ref[...].astype(BF16), preferred_element_type=F32)
    act = (gate * jax.nn.sigmoid(gate) * up).astype(BF16)
    acc = jnp.dot(act, wd_ref[...].astype(BF16), preferred_element_type=F32)
    o_ref[...] += acc[:tm]

    @pl.when(first_tile)
    def _():
        os_ref[...] += acc[tm:]

    if final_norm:
        last = j == pl.num_programs(1) - 1

        @pl.when(last)
        def _():
            o_ref[...] = _rms(o_ref[...], gf_ref[...])

        @pl.when(last & first_tile)
        def _():
            os_ref[...] = _rms(os_ref[...], gf_ref[...])


def ffn(x, xs, g, w_gate, w_up, w_down, layer, *, tm, tf, name, final_gain=None):
    m, d = x.shape
    rows_s = xs.shape[0]
    assert rows_s == EXTRA_ROWS and m % tm == 0
    gains, lg = g
    f = w_gate.shape[2]
    final_norm = final_gain is not None
    extra_specs = [pl.BlockSpec((1, d), lambda i, j: (0, 0))] if final_norm else []
    extra_args = [final_gain.reshape(1, d)] if final_norm else []
    return pl.pallas_call(
        functools.partial(_ffn_kernel, final_norm=final_norm),
        grid=(m // tm, f // tf),
        in_specs=[pl.BlockSpec((tm, d), lambda i, j: (i, 0)),
                  pl.BlockSpec((rows_s, d), lambda i, j: (0, 0)),
                  _layer_spec(lg, (1, d), lambda i, j: (0, 0)),
                  _layer_spec(layer, (d, tf), lambda i, j: (0, j)),
                  _layer_spec(layer, (d, tf), lambda i, j: (0, j)),
                  _layer_spec(layer, (tf, d), lambda i, j: (j, 0))] + extra_specs,
        out_specs=[pl.BlockSpec((tm, d), lambda i, j: (i, 0)), pl.BlockSpec((rows_s, d), lambda i, j: (0, 0))],
        out_shape=[jax.ShapeDtypeStruct((m, d), F32), jax.ShapeDtypeStruct((rows_s, d), F32)],
        scratch_shapes=[pltpu.VMEM((tm + rows_s, d), BF16)],
        compiler_params=_params("arbitrary", "arbitrary"),
        name=name,
    )(x, xs, gains, w_gate, w_up, w_down, *extra_args)


def _pool_block_kernel(x_ref, g_ref, win_ref, wg_ref, scale_ref, wout_ref, o_ref, tail_ref, ext_ref, p_ref,
                       *, tm, tiles_per_seq):
    d = x_ref.shape[1]
    gdim = d // len(POOL_WINDOWS)
    t_in_seq = pl.program_id(0) % tiles_per_seq
    x = x_ref[...]
    u = jnp.dot(_rms(x, g_ref[...]).astype(BF16), win_ref[...], preferred_element_type=F32)

    @pl.when(t_in_seq == 0)
    def _():
        ext_ref[0:POOL_HALO, :] = jnp.zeros((POOL_HALO, d), F32)

    @pl.when(t_in_seq != 0)
    def _():
        ext_ref[0:POOL_HALO, :] = ext_ref[tm:tm + POOL_HALO, :]

    ext_ref[POOL_HALO:, :] = u
    tail_ref[...] = u[tm - POOL_HALO:, :]
    for c in range(tm // POOL_CHUNK):
        r0 = POOL_HALO + c * POOL_CHUNK
        pos = t_in_seq * tm + c * POOL_CHUNK + lax.broadcasted_iota(jnp.int32, (POOL_CHUNK, 1), 0)
        for gi, w in enumerate(POOL_WINDOWS):
            cols = slice(gi * gdim, (gi + 1) * gdim)
            u_q = ext_ref[r0:r0 + POOL_CHUNK, cols]
            acc = u_q
            for k in range(1, w):
                acc = acc + ext_ref[r0 - k:r0 - k + POOL_CHUNK, cols]
            inv_cnt = 1.0 / jnp.minimum(pos + 1, w).astype(F32)
            p_ref[c * POOL_CHUNK:(c + 1) * POOL_CHUNK, cols] = (acc * inv_cnt - u_q).astype(BF16)
    zs = []
    for gi in range(len(POOL_WINDOWS)):
        cols = slice(gi * gdim, (gi + 1) * gdim)
        z = jnp.dot(p_ref[:, cols], wg_ref[gi], preferred_element_type=F32)
        zs.append((z * scale_ref[:, cols]).astype(BF16))
    o_ref[...] = x_ref[...] + jnp.dot(jnp.concatenate(zs, axis=1), wout_ref[...], preferred_element_type=F32)


def pool_block(x, g, w_in, w_group, scale, w_out, layer, *, seq, tm, name):
    m, d = x.shape
    gains, lg = g
    tiles_per_seq = seq // tm
    return pl.pallas_call(
        functools.partial(_pool_block_kernel, tm=tm, tiles_per_seq=tiles_per_seq),
        grid=(m // tm,),
        in_specs=[pl.BlockSpec((tm, d), lambda i: (i, 0)),
                  _layer_spec(lg, (1, d), lambda i: (0, 0)),
                  _resident_spec(layer, (d, d)),
                  _resident_spec(layer, w_group.shape[1:]),
                  _layer_spec(layer, (1, d), lambda i: (0, 0)),
                  _resident_spec(layer, (d, d))],
        out_specs=[pl.BlockSpec((tm, d), lambda i: (i, 0)),
                   pl.BlockSpec((None, POOL_HALO, d), lambda i: (i // tiles_per_seq, 0, 0))],
        out_shape=[jax.ShapeDtypeStruct((m, d), F32), jax.ShapeDtypeStruct((m // seq, POOL_HALO, d), F32)],
        scratch_shapes=[pltpu.VMEM((tm + POOL_HALO, d), F32), pltpu.VMEM((tm, d), BF16)],
        compiler_params=_params("arbitrary"),
        name=name,
    )(x, gains, w_in, w_group, scale, w_out)


def _pool_sample_kernel(state_ref, u_ref, wg_ref, scale_ref, z_ref):
    n_state, _, d = state_ref.shape
    gdim = d // len(POOL_WINDOWS)
    u = u_ref[...]
    for gi, w in enumerate(POOL_WINDOWS):
        cols = slice(gi * gdim, (gi + 1) * gdim)
        u_q = u[:, cols]
        acc = u_q
        for k in range(1, w):
            acc = acc + state_ref[n_state - k, :, cols]
        cnt = float(min(n_state + 1, w))
        p = (acc / cnt - u_q).astype(BF16)
        z = jnp.dot(p, wg_ref[gi].astype(BF16), preferred_element_type=F32)
        z_ref[:, cols] = z * scale_ref[:, cols]


def pool_sample(state_t, u, w_group, scale, layer, *, name):
    b, d = u.shape
    return pl.pallas_call(
        _pool_sample_kernel,
        grid=(1,),
        in_specs=[pl.BlockSpec(state_t.shape, lambda i: (0, 0, 0)),
                  pl.BlockSpec((b, d), lambda i: (0, 0)),
                  _layer_spec(layer, w_group.shape[1:], lambda i: (0, 0, 0)),
                  _layer_spec(layer, (1, d), lambda i: (0, 0))],
        out_specs=pl.BlockSpec((b, d), lambda i: (0, 0)),
        out_shape=jax.ShapeDtypeStruct((b, d), F32),
        compiler_params=_params("arbitrary"),
        name=name,
    )(state_t, u, w_group, scale)


def _mem_block_kernel(x_ref, g_ref, wq_ref, k_ref, v_ref, wo_ref, o_ref):
    rows = x_ref.shape[0]
    x = x_ref[...]
    if rows < MIN_ROWS:
        x = jnp.broadcast_to(x, (MIN_ROWS, x.shape[1]))
    h = _rms(x, g_ref[...]).astype(BF16)
    q = jnp.dot(h, wq_ref[...].astype(BF16), preferred_element_type=F32).astype(BF16)
    kb = k_ref[...].astype(BF16)
    vb = v_ref[...].astype(BF16)
    scale = MEM_HEAD_DIM ** -0.5
    outs = []
    for hd in range(MEM_HEADS):
        sl = slice(hd * MEM_HEAD_DIM, (hd + 1) * MEM_HEAD_DIM)
        s = lax.dot_general(q[:, sl], kb[:, sl], NT_DIMS, preferred_element_type=F32) * scale
        p, _, _ = _softmax_rows(s)
        outs.append(jnp.dot(p.astype(BF16), vb[:, sl], preferred_element_type=F32).astype(BF16))
    o = jnp.concatenate(outs, axis=1)
    out = x + jnp.dot(o, wo_ref[...].astype(BF16), preferred_element_type=F32)
    o_ref[...] = out[:rows]


def mem_block(x, g, w_q, mk, mv, w_o, layer, *, tm, name):
    n, length, d = x.shape
    gains, lg = g
    n_mem, dm = mk.shape[2], mk.shape[3]
    tm = min(tm, length)
    kv_spec = _layer_spec(layer, (None, n_mem, dm), lambda b, i: (b, 0, 0))
    x_spec = pl.BlockSpec((None, tm, d), lambda b, i: (b, i, 0))
    return pl.pallas_call(
        _mem_block_kernel,
        grid=(n, length // tm),
        in_specs=[x_spec,
                  _layer_spec(lg, (1, d), lambda b, i: (0, 0)),
                  _layer_spec(layer, (d, dm), lambda b, i: (0, 0)),
                  kv_spec, kv_spec,
                  _layer_spec(layer, (dm, d), lambda b, i: (0, 0))],
        out_specs=x_spec,
        out_shape=jax.ShapeDtypeStruct((n, length, d), F32),
        compiler_params=_params("parallel", "parallel"),
        name=name,
    )(x, gains, w_q, mk, mv, w_o)


def _strided_rows(ref, start, dil):
    if dil == 1:
        return ref[pl.ds(start, DIL_BLOCK), :]
    return ref[pl.ds(start, DIL_BLOCK, stride=dil), :]


def _dil_prompt_kernel(qa_ref, qb_ref, k_ref, v_ref, o_ref, lse_ref, *, dil, seq):
    span = dil * DIL_BLOCK
    nb = seq // span
    qi = lax.broadcasted_iota(jnp.int32, (DIL_BLOCK, 2 * DIL_BLOCK), 0)
    kj = lax.broadcasted_iota(jnp.int32, (DIL_BLOCK, 2 * DIL_BLOCK), 1)
    dist = qi + DIL_BLOCK - kj
    in_window = (dist >= 0) & (dist <= DIL_BLOCK)
    causal = (lax.broadcasted_iota(jnp.int32, (DIL_BLOCK, DIL_BLOCK), 0)
              >= lax.broadcasted_iota(jnp.int32, (DIL_BLOCK, DIL_BLOCK), 1))
    lane = lax.broadcasted_iota(jnp.int32, (DIL_BLOCK, LANES), 1)
    scale = HEAD_DIM ** -0.5
    q_refs = (qa_ref, qb_ref)
    kv_head = pl.program_id(1)

    @pl.when(kv_head == 0)
    def _():
        lse_ref[...] = jnp.zeros(lse_ref.shape, F32)

    def store_rows(ref, lead, start, val):
        rows = pl.ds(start, DIL_BLOCK) if dil == 1 else pl.ds(start, DIL_BLOCK, stride=dil)
        ref[lead + (rows, slice(None))] = val

    def scores(jb):
        if nb > 1:
            r, b = jb // nb, jb % nb
            start = b * span + r
        else:
            b, start = 0, jb
        if dil == 1:
            start = pl.multiple_of(start, DIL_BLOCK)
        q2 = jnp.concatenate([_strided_rows(q, start, dil) for q in q_refs], axis=0).astype(BF16)
        k_cur = _strided_rows(k_ref, start, dil)
        v_cur = _strided_rows(v_ref, start, dil)
        if nb > 1:
            prev = jnp.maximum(b - 1, 0) * span + (start - b * span)
            if dil == 1:
                prev = pl.multiple_of(prev, DIL_BLOCK)
            kcat = jnp.concatenate([_strided_rows(k_ref, prev, dil), k_cur], axis=0).astype(BF16)
            vcat = jnp.concatenate([_strided_rows(v_ref, prev, dil), v_cur], axis=0).astype(BF16)
            first_key = jnp.where(b > 0, 0, DIL_BLOCK)
            valid = in_window & (kj >= first_key)
        else:
            kcat, vcat, valid = k_cur.astype(BF16), v_cur.astype(BF16), causal
        s = lax.dot_general(q2, kcat, NT_DIMS, preferred_element_type=F32) * scale
        return start, jnp.where(jnp.concatenate([valid] * KV_REP, axis=0), s, NEG), vcat

    def block_group(jg, carry):
        staged = [scores(jg * DIL_GROUP + i) for i in range(DIL_GROUP)]
        lse_old = [_strided_rows(lse_ref, start, dil) for start, _, _ in staged]
        probs = [_softmax_rows(s) for _, s, _ in staged]
        outs = [jnp.dot(p.astype(BF16), vcat, preferred_element_type=F32)
                for (p, _, _), (_, _, vcat) in zip(probs, staged)]
        for (start, _, _), (_, m, den), o2, lse_tile in zip(staged, probs, outs, lse_old):
            lse2 = m + jnp.log(den)
            for hh in range(KV_REP):
                rows = slice(hh * DIL_BLOCK, (hh + 1) * DIL_BLOCK)
                lse_tile = jnp.where(lane == kv_head * KV_REP + hh, lse2[rows], lse_tile)
                store_rows(o_ref, (hh,), start, o2[rows])
            store_rows(lse_ref, (), start, lse_tile)
        return carry

    lax.fori_loop(0, seq // (DIL_BLOCK * DIL_GROUP), block_group, 0)


def dil_attention_prompt(q, k, v, group, dil, *, name):
    n, s, _ = q.shape
    qa = lambda b, j: (b, 0, group * HEADS_PER_GROUP + KV_REP * j)
    qb = lambda b, j: (b, 0, group * HEADS_PER_GROUP + KV_REP * j + 1)
    kv = lambda b, j: (b, 0, j)
    head_block = pl.BlockSpec((None, s, HEAD_DIM), kv)
    return pl.pallas_call(
        functools.partial(_dil_prompt_kernel, dil=dil, seq=s),
        grid=(n, N_KV_HEADS),
        in_specs=[pl.BlockSpec((None, s, HEAD_DIM), qa), pl.BlockSpec((None, s, HEAD_DIM), qb),
                  head_block, head_block],
        out_specs=[pl.BlockSpec((None, KV_REP, s, HEAD_DIM), lambda b, j: (b, j, 0, 0)),
                   pl.BlockSpec((None, s, LANES), lambda b, j: (b, 0, 0))],
        out_shape=[jax.ShapeDtypeStruct((n, HEADS_PER_GROUP, s, HEAD_DIM), F32),
                   jax.ShapeDtypeStruct((n, s, LANES), F32)],
        compiler_params=_params("parallel", "arbitrary"),
        name=name,
    )(q, q, k, v)


def _dil_merge_out_kernel(o0_ref, o1_ref, o2_ref, l0_ref, l1_ref, l2_ref, w_ref, x_ref, out_ref):
    o_refs = (o0_ref, o1_ref, o2_ref)
    l_refs = (l0_ref, l1_ref, l2_ref)
    tm = x_ref.shape[0]
    chunks = [slice(c * MERGE_ROWS, (c + 1) * MERGE_ROWS) for c in range(tm // MERGE_ROWS)]
    merged = []
    for rows in chunks:
        lses = [l_ref[rows, :] for l_ref in l_refs]
        m = jnp.maximum(jnp.maximum(lses[0], lses[1]), lses[2])
        es = [jnp.exp(l - m) for l in lses]
        den = es[0] + es[1] + es[2]
        ws = [e / den for e in es]
        heads = []
        for h in range(HEADS_PER_GROUP):
            acc = ws[0][:, h:h + 1] * o_refs[0][h, rows, :]
            for gi in (1, 2):
                acc = acc + ws[gi][:, h:h + 1] * o_refs[gi][h, rows, :]
            heads.append(acc.astype(BF16))
        merged.append(jnp.concatenate(heads, axis=1))
    w = w_ref[...]
    for rows, mrg in zip(chunks, merged):
        out_ref[rows, :] = x_ref[rows, :] + jnp.dot(mrg, w, preferred_element_type=F32)


def dil_merge_out(os_, lses, w, x, *, seq, tm, name):
    m, d = x.shape
    w, lw = w
    od = HEADS_PER_GROUP * HEAD_DIM
    tps = seq // tm
    o_spec = pl.BlockSpec((None, HEADS_PER_GROUP, tm, HEAD_DIM), lambda i: (i // tps, 0, i % tps, 0))
    l_spec = pl.BlockSpec((None, tm, LANES), lambda i: (i // tps, i % tps, 0))
    return pl.pallas_call(
        _dil_merge_out_kernel,
        grid=(m // tm,),
        in_specs=[o_spec] * 3 + [l_spec] * 3
                 + [_layer_spec(lw, (od, d), lambda i: (0, 0)), pl.BlockSpec((tm, d), lambda i: (i, 0))],
        out_specs=pl.BlockSpec((tm, d), lambda i: (i, 0)),
        out_shape=jax.ShapeDtypeStruct((m, d), F32),
        compiler_params=_params("parallel"),
        name=name,
    )(*os_, *lses, w, x)


def _dil_sample_kernel(q_ref, kn_ref, vn_ref, *refs, wbuf):
    k_refs, v_refs, o_ref = refs[:N_KV_HEADS], refs[N_KV_HEADS:2 * N_KV_HEADS], refs[2 * N_KV_HEADS]
    scale = HEAD_DIM ** -0.5
    head_row = lax.broadcasted_iota(jnp.int32, (HEADS_PER_GROUP, 1), 0)
    kn = kn_ref[...].astype(BF16).astype(F32)
    vn = vn_ref[...].astype(BF16).astype(F32)
    outs, lses = [], []
    for gi, (window, dil) in enumerate(DIL_PATTERNS):
        qg = q_ref[gi * HEADS_PER_GROUP:(gi + 1) * HEADS_PER_GROUP, :].astype(BF16)
        qf = qg.astype(F32)
        first = wbuf - window
        s = jnp.zeros((HEADS_PER_GROUP, DIL_BLOCK), F32)
        s_new = jnp.zeros((HEADS_PER_GROUP, 1), F32)
        for kv in range(N_KV_HEADS):
            mine = (head_row >= kv * KV_REP) & (head_row < (kv + 1) * KV_REP)
            kc = _strided_rows(k_refs[kv], first, dil).astype(BF16)
            s = jnp.where(mine, lax.dot_general(qg, kc, NT_DIMS, preferred_element_type=F32), s)
            s_new = jnp.where(mine, jnp.sum(qf * kn[kv:kv + 1, :], axis=-1, keepdims=True), s_new)
        s = s * scale
        s_new = s_new * scale
        m = jnp.maximum(jnp.max(s, axis=-1, keepdims=True), s_new)
        e = jnp.exp(s - m)
        e_new = jnp.exp(s_new - m)
        den = jnp.sum(e, axis=-1, keepdims=True) + e_new
        p = (e / den).astype(BF16)
        p_new = (e_new / den).astype(BF16).astype(F32)
        o = jnp.zeros((HEADS_PER_GROUP, HEAD_DIM), F32)
        for kv in range(N_KV_HEADS):
            mine = (head_row >= kv * KV_REP) & (head_row < (kv + 1) * KV_REP)
            vc = _strided_rows(v_refs[kv], first, dil).astype(BF16)
            o_kv = jnp.dot(p, vc, preferred_element_type=F32) + p_new * vn[kv:kv + 1, :]
            o = jnp.where(mine, o_kv, o)
        outs.append(o)
        lses.append(m + jnp.log(den))
    mm = jnp.maximum(jnp.maximum(lses[0], lses[1]), lses[2])
    es = [jnp.exp(l - mm) for l in lses]
    den = es[0] + es[1] + es[2]
    o_ref[...] = (es[0] / den) * outs[0] + (es[1] / den) * outs[1] + (es[2] / den) * outs[2]


def dil_attention_sample(q, k_new, v_new, cache_k, cache_v, *, name):
    b, qd = q.shape
    wbuf = cache_k.shape[1]
    n_heads = qd // HEAD_DIM
    for window, dil in DIL_PATTERNS:
        assert window // dil == DIL_BLOCK and window <= wbuf
    small = lambda rows: pl.BlockSpec((None, rows, HEAD_DIM), lambda i: (i, 0, 0))
    cache_specs = [pl.BlockSpec((None, wbuf, HEAD_DIM), functools.partial(lambda i, kv: (i, 0, kv), kv=kv))
                   for kv in range(N_KV_HEADS)]
    o = pl.pallas_call(
        functools.partial(_dil_sample_kernel, wbuf=wbuf),
        grid=(b,),
        in_specs=[small(n_heads), small(N_KV_HEADS), small(N_KV_HEADS)] + cache_specs + cache_specs,
        out_specs=small(HEADS_PER_GROUP),
        out_shape=jax.ShapeDtypeStruct((b, HEADS_PER_GROUP, HEAD_DIM), F32),
        compiler_params=_params("parallel"),
        name=name,
    )(q.reshape(b, n_heads, HEAD_DIM), k_new.reshape(b, N_KV_HEADS, HEAD_DIM),
      v_new.reshape(b, N_KV_HEADS, HEAD_DIM), *([cache_k] * N_KV_HEADS), *([cache_v] * N_KV_HEADS))
    return o.reshape(b, HEADS_PER_GROUP * HEAD_DIM)


def _trunks(xp, xs, mem_p, mem_s, w, *, batch, seq, pool_state, win_k, win_v):
    depth = w["ffn_w_gate"].shape[0]
    n_a = w["pool_w_in"].shape[0]
    d = xp.shape[1]
    dec_batch = xs.shape[0]
    pool_p, pool_s = [], []
    kp = vp = ks = vs = None
    for l in range(depth):
        if l == n_a:
            kv_w = [(w["w_k_shared"], 0), (w["w_v_shared"], 0)]
            kp, vp = norm_matmul(xp, (w["norm_kv"], 0), kv_w, [F32, F32], tm=1024, tn=512, name="p_kv")
            ks, vs = norm_matmul(xs, (w["norm_kv"], 0), kv_w, [F32, F32], tm=1024, tn=512, name="s_kv")
        if l < n_a:
            xp, tail = pool_block(xp, (w["norm_mix"], l), w["pool_w_in"], w["pool_w_group"], w["pool_scale"],
                                  w["pool_w_out"], l, seq=seq, tm=POOL_TM, name=f"p_pool{l}")
            pool_p.append(tail[:, POOL_HALO - POOL_STATE:])
            (u,) = norm_matmul(xs, (w["norm_mix"], l), [(w["pool_w_in"], l)], [F32], tm=1024, tn=512,
                               name=f"s_pool_in{l}")
            state = pool_state[l]
            pool_s.append(jnp.concatenate([state[:, 1:], u[:, None, :]], axis=1))
            z = pool_sample(jnp.swapaxes(state, 0, 1), u, w["pool_w_group"], w["pool_scale"], l, name=f"s_pool{l}")
            xs = matmul_residual(z, (w["pool_w_out"], l), xs, tm=1024, tn=1024, name=f"s_pool_out{l}")
        else:
            j = l - n_a
            q = norm_matmul_resident(xp, (w["norm_mix"], l), (w["dil_w_q"], j), F32, tm=512, name=f"p_dil_q{j}")
            q3, k3, v3 = (a.reshape(batch, seq, -1) for a in (q, kp, vp))
            os_, lses = [], []
            for gi, (_, dil) in enumerate(DIL_PATTERNS):
                o_g, lse_g = dil_attention_prompt(q3, k3, v3, gi, dil, name=f"p_dil_attn{j}_{gi}")
                os_.append(o_g)
                lses.append(lse_g)
            xp = dil_merge_out(os_, lses, (w["dil_w_o"], j), xp, seq=seq, tm=512, name=f"p_dil_out{j}")
            (q,) = norm_matmul(xs, (w["norm_mix"], l), [(w["dil_w_q"], j)], [F32], tm=1024, tn=512,
                               name=f"s_dil_q{j}")
            o = dil_attention_sample(q, ks, vs, win_k, win_v, name=f"s_dil_attn{j}")
            xs = matmul_residual(o, (w["dil_w_o"], j), xs, tm=1024, tn=1024, name=f"s_dil_out{j}")
        xp = mem_block(xp.reshape(batch, seq, d), (w["norm_mem_q"], l), w["mem_w_q"], *mem_p, w["mem_w_o"], l,
                       tm=512, name=f"p_mem{l}").reshape(batch * seq, d)
        xs = mem_block(xs.reshape(dec_batch, 1, d), (w["norm_mem_q"], l), w["mem_w_q"], *mem_s, w["mem_w_o"], l,
                       tm=1, name=f"s_mem{l}").reshape(dec_batch, d)
        xs_pad = jnp.pad(xs, ((0, EXTRA_ROWS - dec_batch), (0, 0)))
        xp, xs_pad = ffn(xp, xs_pad, (w["norm_ffn"], l), w["ffn_w_gate"], w["ffn_w_up"], w["ffn_w_down"], l,
                         tm=1024, tf=256, name=f"ffn{l}", final_gain=w["norm_final"] if l == depth - 1 else None)
        xs = xs_pad[:dec_batch]
    return (xp, jnp.stack(pool_p), kp, vp), (xs, jnp.stack(pool_s), ks, vs)


def kernel(x_prompt, x_sample, mem_prompt, state_pool, cache_win_k, cache_win_v, cache_mem_k, cache_mem_v,
           norm_mix, norm_mem_q, norm_mem_kv, norm_ffn, pool_w_in, pool_w_group, pool_scale, pool_w_out,
           norm_kv, w_k_shared, w_v_shared, dil_w_q, dil_w_o, mem_w_q, mem_w_k, mem_w_v, mem_w_o,
           ffn_w_gate, ffn_w_up, ffn_w_down, norm_final):
    batch, seq, d = x_prompt.shape
    dec_batch, dec_seq, _ = x_sample.shape
    depth = norm_mix.shape[0]
    n_mem = mem_prompt.shape[1]
    rows3 = lambda a: a.reshape(a.shape[0], 1, a.shape[1])
    w = dict(norm_mix=rows3(norm_mix), norm_mem_q=rows3(norm_mem_q), norm_ffn=rows3(norm_ffn),
             norm_kv=norm_kv.reshape(1, 1, d), norm_final=norm_final, pool_scale=rows3(pool_scale),
             pool_w_in=pool_w_in.astype(BF16), pool_w_group=pool_w_group.astype(BF16),
             pool_w_out=pool_w_out.astype(BF16), dil_w_q=dil_w_q.astype(BF16), dil_w_o=dil_w_o.astype(BF16),
             w_k_shared=w_k_shared[None], w_v_shared=w_v_shared[None],
             mem_w_q=mem_w_q, mem_w_o=mem_w_o, ffn_w_gate=ffn_w_gate, ffn_w_up=ffn_w_up, ffn_w_down=ffn_w_down)

    mem_rows = mem_prompt.reshape(batch * n_mem, d)
    norm_mem_kv3 = rows3(norm_mem_kv)
    mk_p, mv_p = [], []
    for l in range(depth):
        mk, mv = norm_matmul(mem_rows, (norm_mem_kv3, l), [(mem_w_k, l), (mem_w_v, l)], [F32, F32],
                             tm=1024, tn=512, name=f"mem_kv{l}")
        mk_p.append(mk.reshape(batch, n_mem, -1))
        mv_p.append(mv.reshape(batch, n_mem, -1))
    mk_p, mv_p = jnp.stack(mk_p), jnp.stack(mv_p)

    assert dec_seq == 1
    mk_s = cache_mem_k.reshape(depth, dec_batch, n_mem, -1)
    mv_s = cache_mem_v.reshape(depth, dec_batch, n_mem, -1)
    wbuf = cache_win_k.shape[1]
    (y_p, pool_p, k_p, v_p), (y_s, pool_s, k_s, v_s) = _trunks(
        x_prompt.reshape(batch * seq, d), x_sample.reshape(dec_batch, d), (mk_p, mv_p), (mk_s, mv_s), w,
        batch=batch, seq=seq, pool_state=state_pool,
        win_k=cache_win_k.reshape(dec_batch, wbuf, -1), win_v=cache_win_v.reshape(dec_batch, wbuf, -1))

    max_window = max(wd for wd, _ in DIL_PATTERNS)
    keep_from = max(0, seq - max_window)
    kv_shape = (N_KV_HEADS, HEAD_DIM)
    mem_shape = (depth, batch, n_mem, MEM_HEADS, MEM_HEAD_DIM)
    return (y_p.reshape(batch, seq, d), y_s.reshape(dec_batch, dec_seq, d), pool_p, pool_s,
            k_p.reshape(batch, seq, *kv_shape)[:, keep_from:], v_p.reshape(batch, seq, *kv_shape)[:, keep_from:],
            k_s.reshape(dec_batch, dec_seq, *kv_shape), v_s.reshape(dec_batch, dec_seq, *kv_shape),
            mk_p.reshape(mem_shape), mv_p.reshape(mem_shape))
```

```python
import functools

import jax
import jax.numpy as jnp
from jax import lax
from jax.experimental import pallas as pl
from jax.experimental.pallas import tpu as pltpu

F32 = jnp.float32
BF16 = jnp.bfloat16

RMS_EPS = 1e-6
POOL_WINDOWS = (2, 4, 8, 16)
POOL_STATE = max(POOL_WINDOWS) - 1
POOL_HALO = 16
POOL_CHUNK = 64
POOL_TM = 512
EXTRA_ROWS = 16
DIL_PATTERNS = ((128, 1), (512, 4), (2048, 16))
DIL_BLOCK = 128
DIL_GROUP = 4
MERGE_ROWS = 256
MEM_ROWS = 256
HEAD_DIM = 128
N_KV_HEADS = 4
HEADS_PER_GROUP = 8
KV_REP = HEADS_PER_GROUP // N_KV_HEADS
MEM_HEADS = 4
MEM_HEAD_DIM = 128
LANES = 128
NEG = -1e30
VMEM_LIMIT = 56 * 1024 * 1024
MIN_ROWS = 8

NT_DIMS = (((1,), (1,)), ((), ()))


def _params(*sem):
    return pltpu.CompilerParams(dimension_semantics=sem, vmem_limit_bytes=VMEM_LIMIT)


def _layer_spec(layer, block, index_map):
    return pl.BlockSpec((None,) + tuple(block), lambda *g: (layer,) + tuple(index_map(*g)))


def _resident_spec(layer, block):
    zeros = (0,) * len(block)
    return pl.BlockSpec((None,) + tuple(block), lambda *g: (layer,) + zeros, pipeline_mode=pl.Buffered(1))


def _rms(x, g):
    ms = jnp.mean(x * x, axis=-1, keepdims=True)
    return x * lax.rsqrt(ms + RMS_EPS) * g


def _lhs_scratch_dtype(rows):
    return BF16 if rows % 16 == 0 else F32


def _softmax_rows(s):
    m = jnp.max(s, axis=-1, keepdims=True)
    e = jnp.exp(s - m)
    den = jnp.sum(e, axis=-1, keepdims=True)
    return e / den, m, den


def _norm_matmul_kernel(x_ref, g_ref, *refs, n_w):
    w_refs, o_refs, h_ref = refs[:n_w], refs[n_w:2 * n_w], refs[2 * n_w]

    @pl.when(pl.program_id(1) == 0)
    def _():
        h_ref[...] = _rms(x_ref[...], g_ref[...]).astype(h_ref.dtype)

    h = h_ref[...].astype(BF16)
    for w_ref, o_ref in zip(w_refs, o_refs):
        o_ref[...] = jnp.dot(h, w_ref[...].astype(BF16), preferred_element_type=F32).astype(o_ref.dtype)


def norm_matmul(x, g, ws, out_dtypes, *, tm, tn, name):
    m, d = x.shape
    gains, lg = g
    n = ws[0][0].shape[2]
    tm, tn = min(tm, m), min(tn, n)
    n_w = len(ws)
    return pl.pallas_call(
        functools.partial(_norm_matmul_kernel, n_w=n_w),
        grid=(m // tm, n // tn),
        in_specs=[pl.BlockSpec((tm, d), lambda i, j: (i, 0)),
                  _layer_spec(lg, (1, d), lambda i, j: (0, 0))]
                 + [_layer_spec(lw, (d, tn), lambda i, j: (0, j)) for _, lw in ws],
        out_specs=[pl.BlockSpec((tm, tn), lambda i, j: (i, j))] * n_w,
        out_shape=[jax.ShapeDtypeStruct((m, n), dt) for dt in out_dtypes],
        scratch_shapes=[pltpu.VMEM((tm, d), _lhs_scratch_dtype(tm))],
        compiler_params=_params("parallel", "arbitrary"),
        name=name,
    )(x, gains, *[w for w, _ in ws])


def _norm_matmul_resident_kernel(x_ref, g_ref, w_ref, o_ref):
    h = _rms(x_ref[...], g_ref[...]).astype(BF16)
    o_ref[...] = jnp.dot(h, w_ref[...], preferred_element_type=F32).astype(o_ref.dtype)


def norm_matmul_resident(x, g, w, out_dtype, *, tm, name):
    m, d = x.shape
    gains, lg = g
    w, lw = w
    n = w.shape[2]
    return pl.pallas_call(
        _norm_matmul_resident_kernel,
        grid=(m // tm,),
        in_specs=[pl.BlockSpec((tm, d), lambda i: (i, 0)),
                  _layer_spec(lg, (1, d), lambda i: (0, 0)),
                  _resident_spec(lw, (d, n))],
        out_specs=pl.BlockSpec((tm, n), lambda i: (i, 0)),
        out_shape=jax.ShapeDtypeStruct((m, n), out_dtype),
        compiler_params=_params("parallel"),
        name=name,
    )(x, gains, w)


def _matmul_res_kernel(a_ref, w_ref, x_ref, o_ref):
    a = a_ref[...].astype(BF16)
    o_ref[...] = x_ref[...] + jnp.dot(a, w_ref[...].astype(BF16), preferred_element_type=F32)


def matmul_residual(a, w, x, *, tm, tn, name):
    m, k = a.shape
    w, lw = w
    n = w.shape[2]
    tm, tn = min(tm, m), min(tn, n)
    return pl.pallas_call(
        _matmul_res_kernel,
        grid=(m // tm, n // tn),
        in_specs=[pl.BlockSpec((tm, k), lambda i, j: (i, 0)),
                  _layer_spec(lw, (k, tn), lambda i, j: (0, j)),
                  pl.BlockSpec((tm, tn), lambda i, j: (i, j))],
        out_specs=pl.BlockSpec((tm, tn), lambda i, j: (i, j)),
        out_shape=jax.ShapeDtypeStruct((m, n), F32),
        compiler_params=_params("parallel", "parallel"),
        name=name,
    )(a, w, x)


def _ffn_kernel(x_ref, xs_ref, g_ref, wg_ref, wu_ref, wd_ref, *rest, final_norm):
    if final_norm:
        gf_ref, o_ref, os_ref, h_ref = rest
    else:
        o_ref, os_ref, h_ref = rest
    i, j = pl.program_id(0), pl.program_id(1)
    tm = x_ref.shape[0]
    first_tile = i == 0

    @pl.when(j == 0)
    def _():
        x = x_ref[...]
        h_ref[0:tm, :] = _rms(x, g_ref[...]).astype(BF16)
        o_ref[...] = x

    @pl.when((j == 0) & first_tile)
    def _():
        xs = xs_ref[...]
        h_ref[tm:, :] = _rms(xs, g_ref[...]).astype(BF16)
        os_ref[...] = xs

    h = h_ref[...]
    gate = jnp.dot(h, wg_ref[...].astype(BF16), preferred_element_type=F32)
    up = jnp.dot(h, wu_ref[...].astype(BF16), preferred_element_type=F32)
    act = (gate * jax.nn.sigmoid(gate) * up).astype(BF16)
    wd = wd_ref[...].astype(BF16)
    o_ref[...] += jnp.dot(act[:tm], wd, preferred_element_type=F32)

    @pl.when(first_tile)
    def _():
        os_ref[...] += jnp.dot(act[tm:], wd, preferred_element_type=F32)

    if final_norm:
        last = j == pl.num_programs(1) - 1

        @pl.when(last)
        def _():
            o_ref[...] = _rms(o_ref[...], gf_ref[...])

        @pl.when(last & first_tile)
        def _():
            os_ref[...] = _rms(os_ref[...], gf_ref[...])


def ffn(x, xs, g, w_gate, w_up, w_down, layer, *, tm, tf, name, final_gain=None):
    m, d = x.shape
    rows_s = xs.shape[0]
    assert rows_s == EXTRA_ROWS and m % tm == 0
    gains, lg = g
    f = w_gate.shape[2]
    final_norm = final_gain is not None
    extra_specs = [pl.BlockSpec((1, d), lambda i, j: (0, 0))] if final_norm else []
    extra_args = [final_gain.reshape(1, d)] if final_norm else []
    return pl.pallas_call(
        functools.partial(_ffn_kernel, final_norm=final_norm),
        grid=(m // tm, f // tf),
        in_specs=[pl.BlockSpec((tm, d), lambda i, j: (i, 0)),
                  pl.BlockSpec((rows_s, d), lambda i, j: (0, 0)),
                  _layer_spec(lg, (1, d), lambda i, j: (0, 0)),
                  _layer_spec(layer, (d, tf), lambda i, j: (0, j)),
                  _layer_spec(layer, (d, tf), lambda i, j: (0, j)),
                  _layer_spec(layer, (tf, d), lambda i, j: (j, 0))] + extra_specs,
        out_specs=[pl.BlockSpec((tm, d), lambda i, j: (i, 0)), pl.BlockSpec((rows_s, d), lambda i, j: (0, 0))],
        out_shape=[jax.ShapeDtypeStruct((m, d), F32), jax.ShapeDtypeStruct((rows_s, d), F32)],
        scratch_shapes=[pltpu.VMEM((tm + rows_s, d), BF16)],
        compiler_params=_params("arbitrary", "arbitrary"),
        name=name,
    )(x, xs, gains, w_gate, w_up, w_down, *extra_args)


def _pool_block_kernel(x_ref, g_ref, win_ref, wg_ref, scale_ref, wout_ref, o_ref, tail_ref, ext_ref, p_ref,
                       *, tm, tiles_per_seq):
    d = x_ref.shape[1]
    gdim = d // len(POOL_WINDOWS)
    t_in_seq = pl.program_id(0) % tiles_per_seq
    x = x_ref[...]
    u = jnp.dot(_rms(x, g_ref[...]).astype(BF16), win_ref[...], preferred_element_type=F32)

    @pl.when(t_in_seq == 0)
    def _():
        ext_ref[0:POOL_HALO, :] = jnp.zeros((POOL_HALO, d), F32)

    @pl.when(t_in_seq != 0)
    def _():
        ext_ref[0:POOL_HALO, :] = ext_ref[tm:tm + POOL_HALO, :]

    ext_ref[POOL_HALO:, :] = u
    tail_ref[...] = u[tm - POOL_HALO:, :]
    for c in range(tm // POOL_CHUNK):
        r0 = POOL_HALO + c * POOL_CHUNK
        pos = t_in_seq * tm + c * POOL_CHUNK + lax.broadcasted_iota(jnp.int32, (POOL_CHUNK, 1), 0)
        for gi, w in enumerate(POOL_WINDOWS):
            cols = slice(gi * gdim, (gi + 1) * gdim)
            u_q = ext_ref[r0:r0 + POOL_CHUNK, cols]
            acc = u_q
            for k in range(1, w):
                acc = acc + ext_ref[r0 - k:r0 - k + POOL_CHUNK, cols]
            inv_cnt = 1.0 / jnp.minimum(pos + 1, w).astype(F32)
            p_ref[c * POOL_CHUNK:(c + 1) * POOL_CHUNK, cols] = (acc * inv_cnt - u_q).astype(BF16)
    zs = []
    for gi in range(len(POOL_WINDOWS)):
        cols = slice(gi * gdim, (gi + 1) * gdim)
        z = jnp.dot(p_ref[:, cols], wg_ref[gi], preferred_element_type=F32)
        zs.append((z * scale_ref[:, cols]).astype(BF16))
    o_ref[...] = x_ref[...] + jnp.dot(jnp.concatenate(zs, axis=1), wout_ref[...], preferred_element_type=F32)


def pool_block(x, g, w_in, w_group, scale, w_out, layer, *, seq, tm, name):
    m, d = x.shape
    gains, lg = g
    tiles_per_seq = seq // tm
    return pl.pallas_call(
        functools.partial(_pool_block_kernel, tm=tm, tiles_per_seq=tiles_per_seq),
        grid=(m // tm,),
        in_specs=[pl.BlockSpec((tm, d), lambda i: (i, 0)),
                  _layer_spec(lg, (1, d), lambda i: (0, 0)),
                  _resident_spec(layer, (d, d)),
                  _resident_spec(layer, w_group.shape[1:]),
                  _layer_spec(layer, (1, d), lambda i: (0, 0)),
                  _resident_spec(layer, (d, d))],
        out_specs=[pl.BlockSpec((tm, d), lambda i: (i, 0)),
                   pl.BlockSpec((None, POOL_HALO, d), lambda i: (i // tiles_per_seq, 0, 0))],
        out_shape=[jax.ShapeDtypeStruct((m, d), F32), jax.ShapeDtypeStruct((m // seq, POOL_HALO, d), F32)],
        scratch_shapes=[pltpu.VMEM((tm + POOL_HALO, d), F32), pltpu.VMEM((tm, d), BF16)],
        compiler_params=_params("arbitrary"),
        name=name,
    )(x, gains, w_in, w_group, scale, w_out)


def _pool_sample_kernel(state_ref, u_ref, wg_ref, scale_ref, z_ref):
    n_state, _, d = state_ref.shape
    gdim = d // len(POOL_WINDOWS)
    u = u_ref[...]
    for gi, w in enumerate(POOL_WINDOWS):
        cols = slice(gi * gdim, (gi + 1) * gdim)
        u_q = u[:, cols]
        acc = u_q
        for k in range(1, w):
            acc = acc + state_ref[n_state - k, :, cols]
        cnt = float(min(n_state + 1, w))
        p = (acc / cnt - u_q).astype(BF16)
        z = jnp.dot(p, wg_ref[gi].astype(BF16), preferred_element_type=F32)
        z_ref[:, cols] = z * scale_ref[:, cols]


def pool_sample(state_t, u, w_group, scale, layer, *, name):
    b, d = u.shape
    return pl.pallas_call(
        _pool_sample_kernel,
        grid=(1,),
        in_specs=[pl.BlockSpec(state_t.shape, lambda i: (0, 0, 0)),
                  pl.BlockSpec((b, d), lambda i: (0, 0)),
                  _layer_spec(layer, w_group.shape[1:], lambda i: (0, 0, 0)),
                  _layer_spec(layer, (1, d), lambda i: (0, 0))],
        out_specs=pl.BlockSpec((b, d), lambda i: (0, 0)),
        out_shape=jax.ShapeDtypeStruct((b, d), F32),
        compiler_params=_params("arbitrary"),
        name=name,
    )(state_t, u, w_group, scale)


def _head_cols(hd):
    return slice(hd * MEM_HEAD_DIM, (hd + 1) * MEM_HEAD_DIM)


def _mem_block_kernel(x_ref, g_ref, wq_ref, k_ref, v_ref, wo_ref, o_ref):
    tm = x_ref.shape[0]
    chunks = [slice(c * MEM_ROWS, (c + 1) * MEM_ROWS) for c in range(tm // MEM_ROWS)]
    wq = wq_ref[...].astype(BF16)
    kb = k_ref[...].astype(BF16)
    vb = v_ref[...].astype(BF16)
    scale = MEM_HEAD_DIM ** -0.5
    hs = [_rms(x_ref[rows, :], g_ref[...]).astype(BF16) for rows in chunks]
    qs = [jnp.dot(h, wq, preferred_element_type=F32).astype(BF16) for h in hs]
    ss = [[lax.dot_general(q[:, _head_cols(hd)], kb[:, _head_cols(hd)], NT_DIMS, preferred_element_type=F32) * scale
           for hd in range(MEM_HEADS)] for q in qs]
    ps = [[_softmax_rows(s)[0].astype(BF16) for s in row] for row in ss]
    os_ = [jnp.concatenate([jnp.dot(p, vb[:, _head_cols(hd)], preferred_element_type=F32).astype(BF16)
                            for hd, p in enumerate(row)], axis=1) for row in ps]
    wo = wo_ref[...].astype(BF16)
    for rows, o in zip(chunks, os_):
        o_ref[rows, :] = x_ref[rows, :] + jnp.dot(o, wo, preferred_element_type=F32)


def _mem_sample_kernel(x_ref, g_ref, wq_ref, k_ref, v_ref, wo_ref, o_ref):
    n_b = x_ref.shape[0]
    x = x_ref[...]
    h = _rms(x, g_ref[...]).astype(BF16)
    q = jnp.dot(h, wq_ref[...].astype(BF16), preferred_element_type=F32).astype(BF16)
    scale = MEM_HEAD_DIM ** -0.5
    row = lax.broadcasted_iota(jnp.int32, (n_b, 1), 0)
    pairs = [(b, hd) for b in range(n_b) for hd in range(MEM_HEADS)]
    ss = [lax.dot_general(q[:, _head_cols(hd)], k_ref[b, :, _head_cols(hd)].astype(BF16), NT_DIMS,
                          preferred_element_type=F32) * scale for b, hd in pairs]
    ps = [_softmax_rows(s)[0].astype(BF16) for s in ss]
    outs = [jnp.zeros((n_b, MEM_HEAD_DIM), F32) for _ in range(MEM_HEADS)]
    for (b, hd), p in zip(pairs, ps):
        o_b = jnp.dot(p, v_ref[b, :, _head_cols(hd)].astype(BF16), preferred_element_type=F32)
        outs[hd] = jnp.where(row == b, o_b, outs[hd])
    o = jnp.concatenate(outs, axis=1).astype(BF16)
    o_ref[...] = x + jnp.dot(o, wo_ref[...].astype(BF16), preferred_element_type=F32)


def mem_block_sample(x, g, w_q, mk, mv, w_o, layer, *, name):
    n_b, d = x.shape
    gains, lg = g
    n_mem, dm = mk.shape[2], mk.shape[3]
    kv_spec = _layer_spec(layer, (n_b, n_mem, dm), lambda i: (0, 0, 0))
    x_spec = pl.BlockSpec((n_b, d), lambda i: (0, 0))
    return pl.pallas_call(
        _mem_sample_kernel,
        grid=(1,),
        in_specs=[x_spec,
                  _layer_spec(lg, (1, d), lambda i: (0, 0)),
                  _layer_spec(layer, (d, dm), lambda i: (0, 0)),
                  kv_spec, kv_spec,
                  _layer_spec(layer, (dm, d), lambda i: (0, 0))],
        out_specs=x_spec,
        out_shape=jax.ShapeDtypeStruct((n_b, d), F32),
        compiler_params=_params("arbitrary"),
        name=name,
    )(x, gains, w_q, mk, mv, w_o)


def mem_block(x, g, w_q, mk, mv, w_o, layer, *, tm, name):
    n, length, d = x.shape
    gains, lg = g
    n_mem, dm = mk.shape[2], mk.shape[3]
    tm = min(tm, length)
    kv_spec = _layer_spec(layer, (None, n_mem, dm), lambda b, i: (b, 0, 0))
    x_spec = pl.BlockSpec((None, tm, d), lambda b, i: (b, i, 0))
    return pl.pallas_call(
        _mem_block_kernel,
        grid=(n, length // tm),
        in_specs=[x_spec,
                  _layer_spec(lg, (1, d), lambda b, i: (0, 0)),
                  _layer_spec(layer, (d, dm), lambda b, i: (0, 0)),
                  kv_spec, kv_spec,
                  _layer_spec(layer, (dm, d), lambda b, i: (0, 0))],
        out_specs=x_spec,
        out_shape=jax.ShapeDtypeStruct((n, length, d), F32),
        compiler_params=_params("parallel", "parallel"),
        name=name,
    )(x, gains, w_q, mk, mv, w_o)


def _strided_rows(ref, start, dil):
    if dil == 1:
        return ref[pl.ds(start, DIL_BLOCK), :]
    return ref[pl.ds(start, DIL_BLOCK, stride=dil), :]


def _dil_group_attention(q_refs, k_ref, v_ref, o_ref, lse_ref, *, dil, seq):
    span = dil * DIL_BLOCK
    nb = seq // span
    qi = lax.broadcasted_iota(jnp.int32, (DIL_BLOCK, 2 * DIL_BLOCK), 0)
    kj = lax.broadcasted_iota(jnp.int32, (DIL_BLOCK, 2 * DIL_BLOCK), 1)
    dist = qi + DIL_BLOCK - kj
    in_window = (dist >= 0) & (dist <= DIL_BLOCK)
    causal = (lax.broadcasted_iota(jnp.int32, (DIL_BLOCK, DIL_BLOCK), 0)
              >= lax.broadcasted_iota(jnp.int32, (DIL_BLOCK, DIL_BLOCK), 1))
    scale = HEAD_DIM ** -0.5

    def store_rows(ref, lead, start, val):
        rows = pl.ds(start, DIL_BLOCK) if dil == 1 else pl.ds(start, DIL_BLOCK, stride=dil)
        ref[lead + (rows, slice(None))] = val

    def scores(jb):
        if nb > 1:
            r, b = jb // nb, jb % nb
            start = b * span + r
        else:
            b, start = 0, jb
        if dil == 1:
            start = pl.multiple_of(start, DIL_BLOCK)
        q2 = jnp.concatenate([_strided_rows(q, start, dil) for q in q_refs], axis=0).astype(BF16)
        k_cur = _strided_rows(k_ref, start, dil)
        v_cur = _strided_rows(v_ref, start, dil)
        if nb > 1:
            prev = jnp.maximum(b - 1, 0) * span + (start - b * span)
            if dil == 1:
                prev = pl.multiple_of(prev, DIL_BLOCK)
            kcat = jnp.concatenate([_strided_rows(k_ref, prev, dil), k_cur], axis=0).astype(BF16)
            vcat = jnp.concatenate([_strided_rows(v_ref, prev, dil), v_cur], axis=0).astype(BF16)
            first_key = jnp.where(b > 0, 0, DIL_BLOCK)
            valid = in_window & (kj >= first_key)
        else:
            kcat, vcat, valid = k_cur.astype(BF16), v_cur.astype(BF16), causal
        s = lax.dot_general(q2, kcat, NT_DIMS, preferred_element_type=F32) * scale
        return start, jnp.where(jnp.concatenate([valid] * KV_REP, axis=0), s, NEG), vcat

    def block_group(jg, carry):
        staged = [scores(jg * DIL_GROUP + i) for i in range(DIL_GROUP)]
        probs = [_softmax_rows(s) for _, s, _ in staged]
        outs = [jnp.dot(p.astype(BF16), vcat, preferred_element_type=F32)
                for (p, _, _), (_, _, vcat) in zip(probs, staged)]
        for (start, _, _), (_, m, den), o2 in zip(staged, probs, outs):
            lse2 = m + jnp.log(den)
            for hh in range(KV_REP):
                rows = slice(hh * DIL_BLOCK, (hh + 1) * DIL_BLOCK)
                store_rows(o_ref, (hh,), start, o2[rows])
                store_rows(lse_ref, (hh,), start, jnp.broadcast_to(lse2[rows], (DIL_BLOCK, HEAD_DIM)))
        return carry

    lax.fori_loop(0, seq // (DIL_BLOCK * DIL_GROUP), block_group, 0)


def _dil_prompt_kernel(*refs, seq):
    n_g = len(DIL_PATTERNS)
    q_refs = refs[:n_g * KV_REP]
    k_ref, v_ref, out_ref, o_scr, lse_scr = refs[n_g * KV_REP:]
    for gi, (_, dil) in enumerate(DIL_PATTERNS):
        _dil_group_attention(q_refs[gi * KV_REP:(gi + 1) * KV_REP], k_ref, v_ref, o_scr.at[gi], lse_scr.at[gi],
                             dil=dil, seq=seq)

    def merge(c, carry):
        rows = pl.ds(pl.multiple_of(c * MERGE_ROWS, MERGE_ROWS), MERGE_ROWS)
        for hh in range(KV_REP):
            lses = [lse_scr[gi, hh, rows, :] for gi in range(n_g)]
            m = functools.reduce(jnp.maximum, lses)
            es = [jnp.exp(l - m) for l in lses]
            den = functools.reduce(jnp.add, es)
            acc = sum((e / den) * o_scr[gi, hh, rows, :] for gi, e in enumerate(es))
            out_ref[hh, rows, :] = acc.astype(out_ref.dtype)
        return carry

    lax.fori_loop(0, seq // MERGE_ROWS, merge, 0)


def dil_attention_prompt(q, k, v, *, name):
    n, s, _ = q.shape
    n_g = len(DIL_PATTERNS)
    q_map = lambda head: (lambda b, j: (b, 0, head + KV_REP * j))
    q_specs = [pl.BlockSpec((None, s, HEAD_DIM), q_map(gi * HEADS_PER_GROUP + hh))
               for gi in range(n_g) for hh in range(KV_REP)]
    head_block = pl.BlockSpec((None, s, HEAD_DIM), lambda b, j: (b, 0, j))
    return pl.pallas_call(
        functools.partial(_dil_prompt_kernel, seq=s),
        grid=(n, N_KV_HEADS),
        in_specs=q_specs + [head_block, head_block],
        out_specs=pl.BlockSpec((None, KV_REP, s, HEAD_DIM), lambda b, j: (b, j, 0, 0)),
        out_shape=jax.ShapeDtypeStruct((n, HEADS_PER_GROUP, s, HEAD_DIM), BF16),
        scratch_shapes=[pltpu.VMEM((n_g, KV_REP, s, HEAD_DIM), F32), pltpu.VMEM((n_g, KV_REP, s, HEAD_DIM), F32)],
        compiler_params=_params("parallel", "parallel"),
        name=name,
    )(*([q] * (n_g * KV_REP)), k, v)


def _heads_out_kernel(o_ref, w_ref, x_ref, out_ref):
    a = jnp.concatenate([o_ref[h] for h in range(o_ref.shape[0])], axis=1)
    out_ref[...] = x_ref[...] + jnp.dot(a, w_ref[...], preferred_element_type=F32)


def heads_out_residual(o, w, x, *, seq, tm, name):
    m, d = x.shape
    w, lw = w
    tps = seq // tm
    return pl.pallas_call(
        _heads_out_kernel,
        grid=(m // tm,),
        in_specs=[pl.BlockSpec((None,) + o.shape[1:2] + (tm, o.shape[3]), lambda i: (i // tps, 0, i % tps, 0)),
                  _resident_spec(lw, w.shape[1:]),
                  pl.BlockSpec((tm, d), lambda i: (i, 0))],
        out_specs=pl.BlockSpec((tm, d), lambda i: (i, 0)),
        out_shape=jax.ShapeDtypeStruct((m, d), F32),
        compiler_params=_params("parallel"),
        name=name,
    )(o, w, x)


def _dil_sample_kernel(q_ref, kn_ref, vn_ref, *refs, wbuf):
    k_refs, v_refs, o_ref = refs[:N_KV_HEADS], refs[N_KV_HEADS:2 * N_KV_HEADS], refs[2 * N_KV_HEADS]
    scale = HEAD_DIM ** -0.5
    head_row = lax.broadcasted_iota(jnp.int32, (HEADS_PER_GROUP, 1), 0)
    kn = kn_ref[...].astype(BF16).astype(F32)
    vn = vn_ref[...].astype(BF16).astype(F32)
    outs, lses = [], []
    for gi, (window, dil) in enumerate(DIL_PATTERNS):
        qg = q_ref[gi * HEADS_PER_GROUP:(gi + 1) * HEADS_PER_GROUP, :].astype(BF16)
        qf = qg.astype(F32)
        first = wbuf - window
        s = jnp.zeros((HEADS_PER_GROUP, DIL_BLOCK), F32)
        s_new = jnp.zeros((HEADS_PER_GROUP, 1), F32)
        for kv in range(N_KV_HEADS):
            mine = (head_row >= kv * KV_REP) & (head_row < (kv + 1) * KV_REP)
            kc = _strided_rows(k_refs[kv], first, dil).astype(BF16)
            s = jnp.where(mine, lax.dot_general(qg, kc, NT_DIMS, preferred_element_type=F32), s)
            s_new = jnp.where(mine, jnp.sum(qf * kn[kv:kv + 1, :], axis=-1, keepdims=True), s_new)
        s = s * scale
        s_new = s_new * scale
        m = jnp.maximum(jnp.max(s, axis=-1, keepdims=True), s_new)
        e = jnp.exp(s - m)
        e_new = jnp.exp(s_new - m)
        den = jnp.sum(e, axis=-1, keepdims=True) + e_new
        p = (e / den).astype(BF16)
        p_new = (e_new / den).astype(BF16).astype(F32)
        o = jnp.zeros((HEADS_PER_GROUP, HEAD_DIM), F32)
        for kv in range(N_KV_HEADS):
            mine = (head_row >= kv * KV_REP) & (head_row < (kv + 1) * KV_REP)
            vc = _strided_rows(v_refs[kv], first, dil).astype(BF16)
            o_kv = jnp.dot(p, vc, preferred_element_type=F32) + p_new * vn[kv:kv + 1, :]
            o = jnp.where(mine, o_kv, o)
        outs.append(o)
        lses.append(m + jnp.log(den))
    mm = jnp.maximum(jnp.maximum(lses[0], lses[1]), lses[2])
    es = [jnp.exp(l - mm) for l in lses]
    den = es[0] + es[1] + es[2]
    o_ref[...] = (es[0] / den) * outs[0] + (es[1] / den) * outs[1] + (es[2] / den) * outs[2]


def dil_attention_sample(q, k_new, v_new, cache_k, cache_v, *, name):
    b, qd = q.shape
    wbuf = cache_k.shape[1]
    n_heads = qd // HEAD_DIM
    for window, dil in DIL_PATTERNS:
        assert window // dil == DIL_BLOCK and window <= wbuf
    small = lambda rows: pl.BlockSpec((None, rows, HEAD_DIM), lambda i: (i, 0, 0))
    cache_specs = [pl.BlockSpec((None, wbuf, HEAD_DIM), functools.partial(lambda i, kv: (i, 0, kv), kv=kv))
                   for kv in range(N_KV_HEADS)]
    o = pl.pallas_call(
        functools.partial(_dil_sample_kernel, wbuf=wbuf),
        grid=(b,),
        in_specs=[small(n_heads), small(N_KV_HEADS), small(N_KV_HEADS)] + cache_specs + cache_specs,
        out_specs=small(HEADS_PER_GROUP),
        out_shape=jax.ShapeDtypeStruct((b, HEADS_PER_GROUP, HEAD_DIM), F32),
        compiler_params=_params("parallel"),
        name=name,
    )(q.reshape(b, n_heads, HEAD_DIM), k_new.reshape(b, N_KV_HEADS, HEAD_DIM),
      v_new.reshape(b, N_KV_HEADS, HEAD_DIM), *([cache_k] * N_KV_HEADS), *([cache_v] * N_KV_HEADS))
    return o.reshape(b, HEADS_PER_GROUP * HEAD_DIM)


def _trunks(xp, xs, mem_p, mem_s, w, *, batch, seq, pool_state, win_k, win_v):
    depth = w["ffn_w_gate"].shape[0]
    n_a = w["pool_w_in"].shape[0]
    d = xp.shape[1]
    dec_batch = xs.shape[0]
    pool_p, pool_s = [], []
    kp = vp = ks = vs = None
    for l in range(depth):
        if l == n_a:
            kv_w = [(w["w_k_shared"], 0), (w["w_v_shared"], 0)]
            kp, vp = norm_matmul(xp, (w["norm_kv"], 0), kv_w, [F32, F32], tm=1024, tn=512, name="p_kv")
            ks, vs = norm_matmul(xs, (w["norm_kv"], 0), kv_w, [F32, F32], tm=1024, tn=512, name="s_kv")
        if l < n_a:
            xp, tail = pool_block(xp, (w["norm_mix"], l), w["pool_w_in"], w["pool_w_group"], w["pool_scale"],
                                  w["pool_w_out"], l, seq=seq, tm=POOL_TM, name=f"p_pool{l}")
            pool_p.append(tail[:, POOL_HALO - POOL_STATE:])
            (u,) = norm_matmul(xs, (w["norm_mix"], l), [(w["pool_w_in"], l)], [F32], tm=1024, tn=512,
                               name=f"s_pool_in{l}")
            state = pool_state[l]
            pool_s.append(jnp.concatenate([state[:, 1:], u[:, None, :]], axis=1))
            z = pool_sample(jnp.swapaxes(state, 0, 1), u, w["pool_w_group"], w["pool_scale"], l, name=f"s_pool{l}")
            xs = matmul_residual(z, (w["pool_w_out"], l), xs, tm=1024, tn=1024, name=f"s_pool_out{l}")
        else:
            j = l - n_a
            q = norm_matmul_resident(xp, (w["norm_mix"], l), (w["dil_w_q"], j), F32, tm=512, name=f"p_dil_q{j}")
            q3, k3, v3 = (a.reshape(batch, seq, -1) for a in (q, kp, vp))
            o = dil_attention_prompt(q3, k3, v3, name=f"p_dil_attn{j}")
            xp = heads_out_residual(o, (w["dil_w_o"], j), xp, seq=seq, tm=1024, name=f"p_dil_out{j}")
            (q,) = norm_matmul(xs, (w["norm_mix"], l), [(w["dil_w_q"], j)], [F32], tm=1024, tn=512,
                               name=f"s_dil_q{j}")
            o = dil_attention_sample(q, ks, vs, win_k, win_v, name=f"s_dil_attn{j}")
            xs = matmul_residual(o, (w["dil_w_o"], j), xs, tm=1024, tn=1024, name=f"s_dil_out{j}")
        xp = mem_block(xp.reshape(batch, seq, d), (w["norm_mem_q"], l), w["mem_w_q"], *mem_p, w["mem_w_o"], l,
                       tm=512, name=f"p_mem{l}").reshape(batch * seq, d)
        xs = mem_block_sample(xs, (w["norm_mem_q"], l), w["mem_w_q"], *mem_s, w["mem_w_o"], l, name=f"s_mem{l}")
        xs_pad = jnp.pad(xs, ((0, EXTRA_ROWS - dec_batch), (0, 0)))
        xp, xs_pad = ffn(xp, xs_pad, (w["norm_ffn"], l), w["ffn_w_gate"], w["ffn_w_up"], w["ffn_w_down"], l,
                         tm=1024, tf=256, name=f"ffn{l}", final_gain=w["norm_final"] if l == depth - 1 else None)
        xs = xs_pad[:dec_batch]
    return (xp, jnp.stack(pool_p), kp, vp), (xs, jnp.stack(pool_s), ks, vs)


def kernel(x_prompt, x_sample, mem_prompt, state_pool, cache_win_k, cache_win_v, cache_mem_k, cache_mem_v,
           norm_mix, norm_mem_q, norm_mem_kv, norm_ffn, pool_w_in, pool_w_group, pool_scale, pool_w_out,
           norm_kv, w_k_shared, w_v_shared, dil_w_q, dil_w_o, mem_w_q, mem_w_k, mem_w_v, mem_w_o,
           ffn_w_gate, ffn_w_up, ffn_w_down, norm_final):
    batch, seq, d = x_prompt.shape
    dec_batch, dec_seq, _ = x_sample.shape
    depth = norm_mix.shape[0]
    n_mem = mem_prompt.shape[1]
    rows3 = lambda a: a.reshape(a.shape[0], 1, a.shape[1])
    w = dict(norm_mix=rows3(norm_mix), norm_mem_q=rows3(norm_mem_q), norm_ffn=rows3(norm_ffn),
             norm_kv=norm_kv.reshape(1, 1, d), norm_final=norm_final, pool_scale=rows3(pool_scale),
             pool_w_in=pool_w_in.astype(BF16), pool_w_group=pool_w_group.astype(BF16),
             pool_w_out=pool_w_out.astype(BF16), dil_w_q=dil_w_q.astype(BF16), dil_w_o=dil_w_o.astype(BF16),
             w_k_shared=w_k_shared[None], w_v_shared=w_v_shared[None],
             mem_w_q=mem_w_q, mem_w_o=mem_w_o, ffn_w_gate=ffn_w_gate, ffn_w_up=ffn_w_up, ffn_w_down=ffn_w_down)

    mem_rows = mem_prompt.reshape(batch * n_mem, d)
    norm_mem_kv3 = rows3(norm_mem_kv)
    mk_p, mv_p = [], []
    for l in range(depth):
        mk, mv = norm_matmul(mem_rows, (norm_mem_kv3, l), [(mem_w_k, l), (mem_w_v, l)], [F32, F32],
                             tm=1024, tn=512, name=f"mem_kv{l}")
        mk_p.append(mk.reshape(batch, n_mem, -1))
        mv_p.append(mv.reshape(batch, n_mem, -1))
    mk_p, mv_p = jnp.stack(mk_p), jnp.stack(mv_p)

    assert dec_seq == 1
    mk_s = cache_mem_k.reshape(depth, dec_batch, n_mem, -1)
    mv_s = cache_mem_v.reshape(depth, dec_batch, n_mem, -1)
    wbuf = cache_win_k.shape[1]
    (y_p, pool_p, k_p, v_p), (y_s, pool_s, k_s, v_s) = _trunks(
        x_prompt.reshape(batch * seq, d), x_sample.reshape(dec_batch, d), (mk_p, mv_p), (mk_s, mv_s), w,
        batch=batch, seq=seq, pool_state=state_pool,
        win_k=cache_win_k.reshape(dec_batch, wbuf, -1), win_v=cache_win_v.reshape(dec_batch, wbuf, -1))

    max_window = max(wd for wd, _ in DIL_PATTERNS)
    keep_from = max(0, seq - max_window)
    kv_shape = (N_KV_HEADS, HEAD_DIM)
    mem_shape = (depth, batch, n_mem, MEM_HEADS, MEM_HEAD_DIM)
    return (y_p.reshape(batch, seq, d), y_s.reshape(dec_batch, dec_seq, d), pool_p, pool_s,
            k_p.reshape(batch, seq, *kv_shape)[:, keep_from:], v_p.reshape(batch, seq, *kv_shape)[:, keep_from:],
            k_s.reshape(dec_batch, dec_seq, *kv_shape), v_s.reshape(dec_batch, dec_seq, *kv_shape),
            mk_p.reshape(mem_shape), mv_p.reshape(mem_shape))
```

```python
import functools

import jax
import jax.numpy as jnp
from jax import lax
from jax.experimental import pallas as pl
from jax.experimental.pallas import tpu as pltpu

F32 = jnp.float32
BF16 = jnp.bfloat16

RMS_EPS = 1e-6
POOL_WINDOWS = (2, 4, 8, 16)
POOL_STATE = max(POOL_WINDOWS) - 1
POOL_HALO = 16
POOL_CHUNK = 64
POOL_TM = 512
POOL_PART = 256
EXTRA_ROWS = 16
DIL_PATTERNS = ((128, 1), (512, 4), (2048, 16))
DIL_BLOCK = 128
DIL_GROUP = 8
MERGE_ROWS = 256
MEM_ROWS = 256
HEAD_DIM = 128
N_KV_HEADS = 4
HEADS_PER_GROUP = 8
KV_REP = HEADS_PER_GROUP // N_KV_HEADS
MEM_HEADS = 4
MEM_HEAD_DIM = 128
LANES = 128
NEG = -1e30
VMEM_LIMIT = 56 * 1024 * 1024
MIN_ROWS = 8

NT_DIMS = (((1,), (1,)), ((), ()))


def _params(*sem):
    return pltpu.CompilerParams(dimension_semantics=sem, vmem_limit_bytes=VMEM_LIMIT)


def _layer_spec(layer, block, index_map):
    return pl.BlockSpec((None,) + tuple(block), lambda *g: (layer,) + tuple(index_map(*g)))


def _resident_spec(layer, block):
    zeros = (0,) * len(block)
    return pl.BlockSpec((None,) + tuple(block), lambda *g: (layer,) + zeros, pipeline_mode=pl.Buffered(1))


def _rms(x, g):
    ms = jnp.mean(x * x, axis=-1, keepdims=True)
    return x * lax.rsqrt(ms + RMS_EPS) * g


def _lhs_scratch_dtype(rows):
    return BF16 if rows % 16 == 0 else F32


def _softmax_rows(s):
    m = jnp.max(s, axis=-1, keepdims=True)
    e = jnp.exp(s - m)
    den = jnp.sum(e, axis=-1, keepdims=True)
    return e / den, m, den


def _norm_matmul_kernel(x_ref, g_ref, *refs, n_w):
    w_refs, o_refs, h_ref = refs[:n_w], refs[n_w:2 * n_w], refs[2 * n_w]

    @pl.when(pl.program_id(1) == 0)
    def _():
        h_ref[...] = _rms(x_ref[...], g_ref[...]).astype(h_ref.dtype)

    h = h_ref[...].astype(BF16)
    for w_ref, o_ref in zip(w_refs, o_refs):
        o_ref[...] = jnp.dot(h, w_ref[...].astype(BF16), preferred_element_type=F32).astype(o_ref.dtype)


def norm_matmul(x, g, ws, out_dtypes, *, tm, tn, name):
    m, d = x.shape
    gains, lg = g
    n = ws[0][0].shape[2]
    tm, tn = min(tm, m), min(tn, n)
    n_w = len(ws)
    return pl.pallas_call(
        functools.partial(_norm_matmul_kernel, n_w=n_w),
        grid=(m // tm, n // tn),
        in_specs=[pl.BlockSpec((tm, d), lambda i, j: (i, 0)),
                  _layer_spec(lg, (1, d), lambda i, j: (0, 0))]
                 + [_layer_spec(lw, (d, tn), lambda i, j: (0, j)) for _, lw in ws],
        out_specs=[pl.BlockSpec((tm, tn), lambda i, j: (i, j))] * n_w,
        out_shape=[jax.ShapeDtypeStruct((m, n), dt) for dt in out_dtypes],
        scratch_shapes=[pltpu.VMEM((tm, d), _lhs_scratch_dtype(tm))],
        compiler_params=_params("parallel", "arbitrary"),
        name=name,
    )(x, gains, *[w for w, _ in ws])


def _shared_kv_kernel(x_ref, g_ref, wk_ref, wv_ref, k_ref, v_ref, kh_ref, vh_ref):
    tm = x_ref.shape[0]
    h = _rms(x_ref[...], g_ref[...]).astype(BF16)
    for w_ref, o_ref, oh_ref in ((wk_ref, k_ref, kh_ref), (wv_ref, v_ref, vh_ref)):
        r = jnp.dot(h, w_ref[...].astype(BF16), preferred_element_type=F32)
        o_ref[...] = r
        for hd in range(N_KV_HEADS):
            oh_ref[pl.ds(hd, tm, stride=N_KV_HEADS), :] = r[:, hd * HEAD_DIM:(hd + 1) * HEAD_DIM]


def shared_kv(x, g, w_k, w_v, *, tm, name):
    m, d = x.shape
    gains, lg = g
    (w_k, lk), (w_v, lv) = w_k, w_v
    n = w_k.shape[2]
    row = lambda i: (i, 0)
    const = lambda i: (0, 0)
    return pl.pallas_call(
        _shared_kv_kernel,
        grid=(m // tm,),
        in_specs=[pl.BlockSpec((tm, d), row), _layer_spec(lg, (1, d), const),
                  _layer_spec(lk, (d, n), const), _layer_spec(lv, (d, n), const)],
        out_specs=[pl.BlockSpec((tm, n), row)] * 2 + [pl.BlockSpec((tm * N_KV_HEADS, HEAD_DIM), row)] * 2,
        out_shape=[jax.ShapeDtypeStruct((m, n), F32)] * 2
                  + [jax.ShapeDtypeStruct((m * N_KV_HEADS, HEAD_DIM), F32)] * 2,
        compiler_params=_params("parallel"),
        name=name,
    )(x, gains, w_k, w_v)


def _norm_matmul_resident_kernel(x_ref, g_ref, w_ref, o_ref):
    h = _rms(x_ref[...], g_ref[...]).astype(BF16)
    o_ref[...] = jnp.dot(h, w_ref[...], preferred_element_type=F32).astype(o_ref.dtype)


def norm_matmul_resident(x, g, w, out_dtype, *, tm, name):
    m, d = x.shape
    gains, lg = g
    w, lw = w
    n = w.shape[2]
    return pl.pallas_call(
        _norm_matmul_resident_kernel,
        grid=(m // tm,),
        in_specs=[pl.BlockSpec((tm, d), lambda i: (i, 0)),
                  _layer_spec(lg, (1, d), lambda i: (0, 0)),
                  _resident_spec(lw, (d, n))],
        out_specs=pl.BlockSpec((tm, n), lambda i: (i, 0)),
        out_shape=jax.ShapeDtypeStruct((m, n), out_dtype),
        compiler_params=_params("parallel"),
        name=name,
    )(x, gains, w)


def _matmul_res_kernel(a_ref, w_ref, x_ref, o_ref):
    a = a_ref[...].astype(BF16)
    o_ref[...] = x_ref[...] + jnp.dot(a, w_ref[...].astype(BF16), preferred_element_type=F32)


def matmul_residual(a, w, x, *, tm, tn, name):
    m, k = a.shape
    w, lw = w
    n = w.shape[2]
    tm, tn = min(tm, m), min(tn, n)
    return pl.pallas_call(
        _matmul_res_kernel,
        grid=(m // tm, n // tn),
        in_specs=[pl.BlockSpec((tm, k), lambda i, j: (i, 0)),
                  _layer_spec(lw, (k, tn), lambda i, j: (0, j)),
                  pl.BlockSpec((tm, tn), lambda i, j: (i, j))],
        out_specs=pl.BlockSpec((tm, tn), lambda i, j: (i, j)),
        out_shape=jax.ShapeDtypeStruct((m, n), F32),
        compiler_params=_params("parallel", "parallel"),
        name=name,
    )(a, w, x)


def _ffn_kernel(x_ref, xs_ref, g_ref, wg_ref, wu_ref, wd_ref, *rest, final_norm):
    if final_norm:
        gf_ref, o_ref, os_ref, h_ref = rest
    else:
        o_ref, os_ref, h_ref = rest
    i, j = pl.program_id(0), pl.program_id(1)
    tm = x_ref.shape[0]
    first_tile = i == 0

    @pl.when(j == 0)
    def _():
        x = x_ref[...]
        h_ref[0:tm, :] = _rms(x, g_ref[...]).astype(BF16)
        o_ref[...] = x

    @pl.when((j == 0) & first_tile)
    def _():
        xs = xs_ref[...]
        h_ref[tm:, :] = _rms(xs, g_ref[...]).astype(BF16)
        os_ref[...] = xs

    h = h_ref[...]
    gate = jnp.dot(h, wg_ref[...].astype(BF16), preferred_element_type=F32)
    up = jnp.dot(h, wu_ref[...].astype(BF16), preferred_element_type=F32)
    act = (gate * jax.nn.sigmoid(gate) * up).astype(BF16)
    wd = wd_ref[...].astype(BF16)
    o_ref[...] += jnp.dot(act[:tm], wd, preferred_element_type=F32)

    @pl.when(first_tile)
    def _():
        os_ref[...] += jnp.dot(act[tm:], wd, preferred_element_type=F32)

    if final_norm:
        last = j == pl.num_programs(1) - 1

        @pl.when(last)
        def _():
            o_ref[...] = _rms(o_ref[...], gf_ref[...])

        @pl.when(last & first_tile)
        def _():
            os_ref[...] = _rms(os_ref[...], gf_ref[...])


def ffn(x, xs, g, w_gate, w_up, w_down, layer, *, tm, tf, name, final_gain=None):
    m, d = x.shape
    rows_s = xs.shape[0]
    assert rows_s == EXTRA_ROWS and m % tm == 0
    gains, lg = g
    f = w_gate.shape[2]
    final_norm = final_gain is not None
    extra_specs = [pl.BlockSpec((1, d), lambda i, j: (0, 0))] if final_norm else []
    extra_args = [final_gain.reshape(1, d)] if final_norm else []
    return pl.pallas_call(
        functools.partial(_ffn_kernel, final_norm=final_norm),
        grid=(m // tm, f // tf),
        in_specs=[pl.BlockSpec((tm, d), lambda i, j: (i, 0)),
                  pl.BlockSpec((rows_s, d), lambda i, j: (0, 0)),
                  _layer_spec(lg, (1, d), lambda i, j: (0, 0)),
                  _layer_spec(layer, (d, tf), lambda i, j: (0, j)),
                  _layer_spec(layer, (d, tf), lambda i, j: (0, j)),
                  _layer_spec(layer, (tf, d), lambda i, j: (j, 0))] + extra_specs,
        out_specs=[pl.BlockSpec((tm, d), lambda i, j: (i, 0)), pl.BlockSpec((rows_s, d), lambda i, j: (0, 0))],
        out_shape=[jax.ShapeDtypeStruct((m, d), F32), jax.ShapeDtypeStruct((rows_s, d), F32)],
        scratch_shapes=[pltpu.VMEM((tm + rows_s, d), BF16)],
        compiler_params=_params("arbitrary", "arbitrary"),
        name=name,
    )(x, xs, gains, w_gate, w_up, w_down, *extra_args)


def _pool_block_kernel(x_ref, g_ref, win_ref, wg_ref, scale_ref, wout_ref, o_ref, tail_ref, ext_ref, p_ref,
                       *, tm, tiles_per_seq):
    d = x_ref.shape[1]
    gdim = d // len(POOL_WINDOWS)
    t_in_seq = pl.program_id(0) % tiles_per_seq

    @pl.when(t_in_seq == 0)
    def _():
        ext_ref[0:POOL_HALO, :] = jnp.zeros((POOL_HALO, d), F32)

    @pl.when(t_in_seq != 0)
    def _():
        ext_ref[0:POOL_HALO, :] = ext_ref[tm:tm + POOL_HALO, :]

    parts = [slice(c * POOL_PART, (c + 1) * POOL_PART) for c in range(tm // POOL_PART)]
    us = [jnp.dot(_rms(x_ref[rows, :], g_ref[...]).astype(BF16), win_ref[...], preferred_element_type=F32)
          for rows in parts]
    for rows, u in zip(parts, us):
        ext_ref[POOL_HALO + rows.start:POOL_HALO + rows.stop, :] = u
    tail_ref[...] = us[-1][POOL_PART - POOL_HALO:, :]
    for rows in parts:
        for c in range(rows.start // POOL_CHUNK, rows.stop // POOL_CHUNK):
            r0 = POOL_HALO + c * POOL_CHUNK
            pos = t_in_seq * tm + c * POOL_CHUNK + lax.broadcasted_iota(jnp.int32, (POOL_CHUNK, 1), 0)
            for gi, w in enumerate(POOL_WINDOWS):
                cols = slice(gi * gdim, (gi + 1) * gdim)
                u_q = ext_ref[r0:r0 + POOL_CHUNK, cols]
                acc = u_q
                for k in range(1, w):
                    acc = acc + ext_ref[r0 - k:r0 - k + POOL_CHUNK, cols]
                inv_cnt = 1.0 / jnp.minimum(pos + 1, w).astype(F32)
                p_ref[c * POOL_CHUNK:(c + 1) * POOL_CHUNK, cols] = (acc * inv_cnt - u_q).astype(BF16)
        zs = []
        for gi in range(len(POOL_WINDOWS)):
            cols = slice(gi * gdim, (gi + 1) * gdim)
            z = jnp.dot(p_ref[rows, cols], wg_ref[gi], preferred_element_type=F32)
            zs.append((z * scale_ref[:, cols]).astype(BF16))
        o_ref[rows, :] = x_ref[rows, :] + jnp.dot(jnp.concatenate(zs, axis=1), wout_ref[...],
                                                   preferred_element_type=F32)


def pool_block(x, g, w_in, w_group, scale, w_out, layer, *, seq, tm, name):
    m, d = x.shape
    gains, lg = g
    tiles_per_seq = seq // tm
    return pl.pallas_call(
        functools.partial(_pool_block_kernel, tm=tm, tiles_per_seq=tiles_per_seq),
        grid=(m // tm,),
        in_specs=[pl.BlockSpec((tm, d), lambda i: (i, 0)),
                  _layer_spec(lg, (1, d), lambda i: (0, 0)),
                  _resident_spec(layer, (d, d)),
                  _resident_spec(layer, w_group.shape[1:]),
                  _layer_spec(layer, (1, d), lambda i: (0, 0)),
                  _resident_spec(layer, (d, d))],
        out_specs=[pl.BlockSpec((tm, d), lambda i: (i, 0)),
                   pl.BlockSpec((None, POOL_HALO, d), lambda i: (i // tiles_per_seq, 0, 0))],
        out_shape=[jax.ShapeDtypeStruct((m, d), F32), jax.ShapeDtypeStruct((m // seq, POOL_HALO, d), F32)],
        scratch_shapes=[pltpu.VMEM((tm + POOL_HALO, d), F32), pltpu.VMEM((tm, d), BF16)],
        compiler_params=_params("arbitrary"),
        name=name,
    )(x, gains, w_in, w_group, scale, w_out)


def _pool_sample_kernel(state_ref, u_ref, wg_ref, scale_ref, z_ref):
    n_state, _, d = state_ref.shape
    gdim = d // len(POOL_WINDOWS)
    u = u_ref[...]
    for gi, w in enumerate(POOL_WINDOWS):
        cols = slice(gi * gdim, (gi + 1) * gdim)
        u_q = u[:, cols]
        acc = u_q
        for k in range(1, w):
            acc = acc + state_ref[n_state - k, :, cols]
        cnt = float(min(n_state + 1, w))
        p = (acc / cnt - u_q).astype(BF16)
        z = jnp.dot(p, wg_ref[gi].astype(BF16), preferred_element_type=F32)
        z_ref[:, cols] = z * scale_ref[:, cols]


def pool_sample(state_t, u, w_group, scale, layer, *, name):
    b, d = u.shape
    return pl.pallas_call(
        _pool_sample_kernel,
        grid=(1,),
        in_specs=[pl.BlockSpec(state_t.shape, lambda i: (0, 0, 0)),
                  pl.BlockSpec((b, d), lambda i: (0, 0)),
                  _layer_spec(layer, w_group.shape[1:], lambda i: (0, 0, 0)),
                  _layer_spec(layer, (1, d), lambda i: (0, 0))],
        out_specs=pl.BlockSpec((b, d), lambda i: (0, 0)),
        out_shape=jax.ShapeDtypeStruct((b, d), F32),
        compiler_params=_params("arbitrary"),
        name=name,
    )(state_t, u, w_group, scale)


def _head_cols(hd):
    return slice(hd * MEM_HEAD_DIM, (hd + 1) * MEM_HEAD_DIM)


def _mem_block_kernel(x_ref, g_ref, wq_ref, k_ref, v_ref, wo_ref, o_ref):
    tm = x_ref.shape[0]
    chunks = [slice(c * MEM_ROWS, (c + 1) * MEM_ROWS) for c in range(tm // MEM_ROWS)]
    wq = wq_ref[...].astype(BF16)
    kb = k_ref[...].astype(BF16)
    vb = v_ref[...].astype(BF16)
    scale = MEM_HEAD_DIM ** -0.5
    hs = [_rms(x_ref[rows, :], g_ref[...]).astype(BF16) for rows in chunks]
    qs = [jnp.dot(h, wq, preferred_element_type=F32).astype(BF16) for h in hs]
    ss = [[lax.dot_general(q[:, _head_cols(hd)], kb[:, _head_cols(hd)], NT_DIMS, preferred_element_type=F32) * scale
           for hd in range(MEM_HEADS)] for q in qs]
    ps = [[_softmax_rows(s)[0].astype(BF16) for s in row] for row in ss]
    os_ = [jnp.concatenate([jnp.dot(p, vb[:, _head_cols(hd)], preferred_element_type=F32).astype(BF16)
                            for hd, p in enumerate(row)], axis=1) for row in ps]
    wo = wo_ref[...].astype(BF16)
    for rows, o in zip(chunks, os_):
        o_ref[rows, :] = x_ref[rows, :] + jnp.dot(o, wo, preferred_element_type=F32)


def _mem_sample_kernel(x_ref, g_ref, wq_ref, k_ref, v_ref, wo_ref, o_ref):
    n_b = x_ref.shape[0]
    n_mem = k_ref.shape[1] // MEM_HEADS
    head_rows = lambda ref, b, hd: ref[b, pl.ds(hd, n_mem, stride=MEM_HEADS), :].astype(BF16)
    x = x_ref[...]
    h = _rms(x, g_ref[...]).astype(BF16)
    q = jnp.dot(h, wq_ref[...].astype(BF16), preferred_element_type=F32).astype(BF16)
    scale = MEM_HEAD_DIM ** -0.5
    row = lax.broadcasted_iota(jnp.int32, (n_b, 1), 0)
    pairs = [(b, hd) for b in range(n_b) for hd in range(MEM_HEADS)]
    ss = [lax.dot_general(q[:, _head_cols(hd)], head_rows(k_ref, b, hd), NT_DIMS,
                          preferred_element_type=F32) * scale for b, hd in pairs]
    ps = [_softmax_rows(s)[0].astype(BF16) for s in ss]
    outs = [jnp.zeros((n_b, MEM_HEAD_DIM), F32) for _ in range(MEM_HEADS)]
    for (b, hd), p in zip(pairs, ps):
        o_b = jnp.dot(p, head_rows(v_ref, b, hd), preferred_element_type=F32)
        outs[hd] = jnp.where(row == b, o_b, outs[hd])
    o = jnp.concatenate(outs, axis=1).astype(BF16)
    o_ref[...] = x + jnp.dot(o, wo_ref[...].astype(BF16), preferred_element_type=F32)


def mem_block_sample(x, g, w_q, mk, mv, w_o, layer, *, name):
    n_b, d = x.shape
    gains, lg = g
    dm = w_q.shape[2]
    kv_spec = _layer_spec(layer, mk.shape[1:], lambda i: (0, 0, 0))
    x_spec = pl.BlockSpec((n_b, d), lambda i: (0, 0))
    return pl.pallas_call(
        _mem_sample_kernel,
        grid=(1,),
        in_specs=[x_spec,
                  _layer_spec(lg, (1, d), lambda i: (0, 0)),
                  _layer_spec(layer, (d, dm), lambda i: (0, 0)),
                  kv_spec, kv_spec,
                  _layer_spec(layer, (dm, d), lambda i: (0, 0))],
        out_specs=x_spec,
        out_shape=jax.ShapeDtypeStruct((n_b, d), F32),
        compiler_params=_params("arbitrary"),
        name=name,
    )(x, gains, w_q, mk, mv, w_o)


def mem_block(x, g, w_q, mk, mv, w_o, layer, *, tm, name):
    n, length, d = x.shape
    gains, lg = g
    n_mem, dm = mk.shape[2], mk.shape[3]
    tm = min(tm, length)
    kv_spec = _layer_spec(layer, (None, n_mem, dm), lambda b, i: (b, 0, 0))
    x_spec = pl.BlockSpec((None, tm, d), lambda b, i: (b, i, 0))
    return pl.pallas_call(
        _mem_block_kernel,
        grid=(n, length // tm),
        in_specs=[x_spec,
                  _layer_spec(lg, (1, d), lambda b, i: (0, 0)),
                  _layer_spec(layer, (d, dm), lambda b, i: (0, 0)),
                  kv_spec, kv_spec,
                  _layer_spec(layer, (dm, d), lambda b, i: (0, 0))],
        out_specs=x_spec,
        out_shape=jax.ShapeDtypeStruct((n, length, d), F32),
        compiler_params=_params("parallel", "parallel"),
        name=name,
    )(x, gains, w_q, mk, mv, w_o)


def _strided_rows(ref, start, dil):
    if dil == 1:
        return ref[pl.ds(start, DIL_BLOCK), :]
    return ref[pl.ds(start, DIL_BLOCK, stride=dil), :]


def _dil_group_attention(q_refs, k_ref, v_ref, o_ref, lse_ref, *, dil, seq):
    span = dil * DIL_BLOCK
    nb = seq // span
    qi = lax.broadcasted_iota(jnp.int32, (DIL_BLOCK, 2 * DIL_BLOCK), 0)
    kj = lax.broadcasted_iota(jnp.int32, (DIL_BLOCK, 2 * DIL_BLOCK), 1)
    dist = qi + DIL_BLOCK - kj
    in_window = (dist >= 0) & (dist <= DIL_BLOCK)
    causal = (lax.broadcasted_iota(jnp.int32, (DIL_BLOCK, DIL_BLOCK), 0)
              >= lax.broadcasted_iota(jnp.int32, (DIL_BLOCK, DIL_BLOCK), 1))
    scale = HEAD_DIM ** -0.5

    def store_rows(ref, lead, start, val):
        rows = pl.ds(start, DIL_BLOCK) if dil == 1 else pl.ds(start, DIL_BLOCK, stride=dil)
        ref[lead + (rows, slice(None))] = val

    def scores(jb):
        if nb > 1:
            r, b = jb // nb, jb % nb
            start = b * span + r
        else:
            b, start = 0, jb
        if dil == 1:
            start = pl.multiple_of(start, DIL_BLOCK)
        q2 = jnp.concatenate([_strided_rows(q, start, dil) for q in q_refs], axis=0).astype(BF16)
        k_cur = _strided_rows(k_ref, start, dil)
        v_cur = _strided_rows(v_ref, start, dil)
        if nb > 1:
            prev = jnp.maximum(b - 1, 0) * span + (start - b * span)
            if dil == 1:
                prev = pl.multiple_of(prev, DIL_BLOCK)
            kcat = jnp.concatenate([_strided_rows(k_ref, prev, dil), k_cur], axis=0).astype(BF16)
            vcat = jnp.concatenate([_strided_rows(v_ref, prev, dil), v_cur], axis=0).astype(BF16)
            first_key = jnp.where(b > 0, 0, DIL_BLOCK)
            valid = in_window & (kj >= first_key)
        else:
            kcat, vcat, valid = k_cur.astype(BF16), v_cur.astype(BF16), causal
        s = lax.dot_general(q2, kcat, NT_DIMS, preferred_element_type=F32) * scale
        return start, jnp.where(jnp.concatenate([valid] * KV_REP, axis=0), s, NEG), vcat

    def block_group(jg, carry):
        staged = [scores(jg * DIL_GROUP + i) for i in range(DIL_GROUP)]
        probs = [_softmax_rows(s) for _, s, _ in staged]
        outs = [jnp.dot(p.astype(BF16), vcat, preferred_element_type=F32)
                for (p, _, _), (_, _, vcat) in zip(probs, staged)]
        for (start, _, _), (_, m, den), o2 in zip(staged, probs, outs):
            lse2 = m + jnp.log(den)
            for hh in range(KV_REP):
                rows = slice(hh * DIL_BLOCK, (hh + 1) * DIL_BLOCK)
                store_rows(o_ref, (hh,), start, o2[rows])
                store_rows(lse_ref, (hh,), start, jnp.broadcast_to(lse2[rows], (DIL_BLOCK, HEAD_DIM)))
        return carry

    lax.fori_loop(0, seq // (DIL_BLOCK * DIL_GROUP), block_group, 0)


def _dil_prompt_kernel(*refs, seq):
    n_g = len(DIL_PATTERNS)
    q_refs = refs[:n_g * KV_REP]
    k_ref, v_ref, out_ref, o_scr, lse_scr = refs[n_g * KV_REP:]
    for gi, (_, dil) in enumerate(DIL_PATTERNS):
        _dil_group_attention(q_refs[gi * KV_REP:(gi + 1) * KV_REP], k_ref, v_ref, o_scr.at[gi], lse_scr.at[gi],
                             dil=dil, seq=seq)

    def merge(c, carry):
        rows = pl.ds(pl.multiple_of(c * MERGE_ROWS, MERGE_ROWS), MERGE_ROWS)
        for hh in range(KV_REP):
            lses = [lse_scr[gi, hh, rows, :] for gi in range(n_g)]
            m = functools.reduce(jnp.maximum, lses)
            es = [jnp.exp(l - m) for l in lses]
            den = functools.reduce(jnp.add, es)
            acc = sum((e / den) * o_scr[gi, hh, rows, :] for gi, e in enumerate(es))
            out_ref[hh, rows, :] = acc.astype(out_ref.dtype)
        return carry

    lax.fori_loop(0, seq // MERGE_ROWS, merge, 0)


def dil_attention_prompt(q, k, v, *, name):
    n, s, _ = q.shape
    n_g = len(DIL_PATTERNS)
    q_map = lambda head: (lambda b, j: (b, 0, head + KV_REP * j))
    q_specs = [pl.BlockSpec((None, s, HEAD_DIM), q_map(gi * HEADS_PER_GROUP + hh))
               for gi in range(n_g) for hh in range(KV_REP)]
    head_block = pl.BlockSpec((None, s, HEAD_DIM), lambda b, j: (b, 0, j))
    return pl.pallas_call(
        functools.partial(_dil_prompt_kernel, seq=s),
        grid=(n, N_KV_HEADS),
        in_specs=q_specs + [head_block, head_block],
        out_specs=pl.BlockSpec((None, KV_REP, s, HEAD_DIM), lambda b, j: (b, j, 0, 0)),
        out_shape=jax.ShapeDtypeStruct((n, HEADS_PER_GROUP, s, HEAD_DIM), BF16),
        scratch_shapes=[pltpu.VMEM((n_g, KV_REP, s, HEAD_DIM), F32), pltpu.VMEM((n_g, KV_REP, s, HEAD_DIM), F32)],
        compiler_params=_params("parallel", "parallel"),
        name=name,
    )(*([q] * (n_g * KV_REP)), k, v)


def _heads_out_kernel(o_ref, w_ref, x_ref, out_ref):
    a = jnp.concatenate([o_ref[h] for h in range(o_ref.shape[0])], axis=1)
    out_ref[...] = x_ref[...] + jnp.dot(a, w_ref[...], preferred_element_type=F32)


def heads_out_residual(o, w, x, *, seq, tm, name):
    m, d = x.shape
    w, lw = w
    tps = seq // tm
    return pl.pallas_call(
        _heads_out_kernel,
        grid=(m // tm,),
        in_specs=[pl.BlockSpec((None,) + o.shape[1:2] + (tm, o.shape[3]), lambda i: (i // tps, 0, i % tps, 0)),
                  _resident_spec(lw, w.shape[1:]),
                  pl.BlockSpec((tm, d), lambda i: (i, 0))],
        out_specs=pl.BlockSpec((tm, d), lambda i: (i, 0)),
        out_shape=jax.ShapeDtypeStruct((m, d), F32),
        compiler_params=_params("parallel"),
        name=name,
    )(o, w, x)


def _dil_sample_kernel(q_ref, kn_ref, vn_ref, k_ref, v_ref, o_ref, *, wbuf):
    scale = HEAD_DIM ** -0.5
    head_row = lax.broadcasted_iota(jnp.int32, (HEADS_PER_GROUP, 1), 0)
    kn = kn_ref[...].astype(BF16).astype(F32)
    vn = vn_ref[...].astype(BF16).astype(F32)
    outs, lses = [], []
    for gi, (window, dil) in enumerate(DIL_PATTERNS):
        qg = q_ref[gi * HEADS_PER_GROUP:(gi + 1) * HEADS_PER_GROUP, :].astype(BF16)
        qf = qg.astype(F32)
        first = wbuf - window
        s = jnp.zeros((HEADS_PER_GROUP, DIL_BLOCK), F32)
        s_new = jnp.zeros((HEADS_PER_GROUP, 1), F32)
        for kv in range(N_KV_HEADS):
            mine = (head_row >= kv * KV_REP) & (head_row < (kv + 1) * KV_REP)
            kc = _strided_rows(k_ref, first * N_KV_HEADS + kv, dil * N_KV_HEADS).astype(BF16)
            s = jnp.where(mine, lax.dot_general(qg, kc, NT_DIMS, preferred_element_type=F32), s)
            s_new = jnp.where(mine, jnp.sum(qf * kn[kv:kv + 1, :], axis=-1, keepdims=True), s_new)
        s = s * scale
        s_new = s_new * scale
        m = jnp.maximum(jnp.max(s, axis=-1, keepdims=True), s_new)
        e = jnp.exp(s - m)
        e_new = jnp.exp(s_new - m)
        den = jnp.sum(e, axis=-1, keepdims=True) + e_new
        p = (e / den).astype(BF16)
        p_new = (e_new / den).astype(BF16).astype(F32)
        o = jnp.zeros((HEADS_PER_GROUP, HEAD_DIM), F32)
        for kv in range(N_KV_HEADS):
            mine = (head_row >= kv * KV_REP) & (head_row < (kv + 1) * KV_REP)
            vc = _strided_rows(v_ref, first * N_KV_HEADS + kv, dil * N_KV_HEADS).astype(BF16)
            o_kv = jnp.dot(p, vc, preferred_element_type=F32) + p_new * vn[kv:kv + 1, :]
            o = jnp.where(mine, o_kv, o)
        outs.append(o)
        lses.append(m + jnp.log(den))
    mm = jnp.maximum(jnp.maximum(lses[0], lses[1]), lses[2])
    es = [jnp.exp(l - mm) for l in lses]
    den = es[0] + es[1] + es[2]
    o_ref[...] = (es[0] / den) * outs[0] + (es[1] / den) * outs[1] + (es[2] / den) * outs[2]


def dil_attention_sample(q, k_new, v_new, cache_k, cache_v, *, name):
    b, qd = q.shape
    wbuf = cache_k.shape[1]
    n_heads = qd // HEAD_DIM
    for window, dil in DIL_PATTERNS:
        assert window // dil == DIL_BLOCK and window <= wbuf
    small = lambda rows: pl.BlockSpec((None, rows, HEAD_DIM), lambda i: (i, 0, 0))
    rows_view = lambda c: c.reshape(b, wbuf * N_KV_HEADS, HEAD_DIM)
    o = pl.pallas_call(
        functools.partial(_dil_sample_kernel, wbuf=wbuf),
        grid=(b,),
        in_specs=[small(n_heads), small(N_KV_HEADS), small(N_KV_HEADS),
                  small(wbuf * N_KV_HEADS), small(wbuf * N_KV_HEADS)],
        out_specs=small(HEADS_PER_GROUP),
        out_shape=jax.ShapeDtypeStruct((b, HEADS_PER_GROUP, HEAD_DIM), F32),
        compiler_params=_params("parallel"),
        name=name,
    )(q.reshape(b, n_heads, HEAD_DIM), k_new.reshape(b, N_KV_HEADS, HEAD_DIM),
      v_new.reshape(b, N_KV_HEADS, HEAD_DIM), rows_view(cache_k), rows_view(cache_v))
    return o.reshape(b, HEADS_PER_GROUP * HEAD_DIM)


def _trunks(xp, xs, mem_p, mem_s, w, *, batch, seq, pool_state, win_k, win_v):
    depth = w["ffn_w_gate"].shape[0]
    n_a = w["pool_w_in"].shape[0]
    d = xp.shape[1]
    dec_batch = xs.shape[0]
    pool_p, pool_s = [], []
    kp = vp = kp_rows = vp_rows = ks = vs = None
    for l in range(depth):
        if l == n_a:
            kv_w = [(w["w_k_shared"], 0), (w["w_v_shared"], 0)]
            kp, vp, kp_rows, vp_rows = shared_kv(xp, (w["norm_kv"], 0), *kv_w, tm=512, name="p_kv")
            ks, vs = norm_matmul(xs, (w["norm_kv"], 0), kv_w, [F32, F32], tm=1024, tn=512, name="s_kv")
        if l < n_a:
            xp, tail = pool_block(xp, (w["norm_mix"], l), w["pool_w_in"], w["pool_w_group"], w["pool_scale"],
                                  w["pool_w_out"], l, seq=seq, tm=POOL_TM, name=f"p_pool{l}")
            pool_p.append(tail[:, POOL_HALO - POOL_STATE:])
            (u,) = norm_matmul(xs, (w["norm_mix"], l), [(w["pool_w_in"], l)], [F32], tm=1024, tn=512,
                               name=f"s_pool_in{l}")
            state = pool_state[l]
            pool_s.append(jnp.concatenate([state[:, 1:], u[:, None, :]], axis=1))
            z = pool_sample(jnp.swapaxes(state, 0, 1), u, w["pool_w_group"], w["pool_scale"], l, name=f"s_pool{l}")
            xs = matmul_residual(z, (w["pool_w_out"], l), xs, tm=1024, tn=1024, name=f"s_pool_out{l}")
        else:
            j = l - n_a
            q = norm_matmul_resident(xp, (w["norm_mix"], l), (w["dil_w_q"], j), F32, tm=512, name=f"p_dil_q{j}")
            q3, k3, v3 = (a.reshape(batch, seq, -1) for a in (q, kp, vp))
            o = dil_attention_prompt(q3, k3, v3, name=f"p_dil_attn{j}")
            xp = heads_out_residual(o, (w["dil_w_o"], j), xp, seq=seq, tm=1024, name=f"p_dil_out{j}")
            (q,) = norm_matmul(xs, (w["norm_mix"], l), [(w["dil_w_q"], j)], [F32], tm=1024, tn=512,
                               name=f"s_dil_q{j}")
            o = dil_attention_sample(q, ks, vs, win_k, win_v, name=f"s_dil_attn{j}")
            xs = matmul_residual(o, (w["dil_w_o"], j), xs, tm=1024, tn=1024, name=f"s_dil_out{j}")
        xp = mem_block(xp.reshape(batch, seq, d), (w["norm_mem_q"], l), w["mem_w_q"], *mem_p, w["mem_w_o"], l,
                       tm=512, name=f"p_mem{l}").reshape(batch * seq, d)
        xs = mem_block_sample(xs, (w["norm_mem_q"], l), w["mem_w_q"], *mem_s, w["mem_w_o"], l, name=f"s_mem{l}")
        xs_pad = jnp.pad(xs, ((0, EXTRA_ROWS - dec_batch), (0, 0)))
        xp, xs_pad = ffn(xp, xs_pad, (w["norm_ffn"], l), w["ffn_w_gate"], w["ffn_w_up"], w["ffn_w_down"], l,
                         tm=1024, tf=256, name=f"ffn{l}", final_gain=w["norm_final"] if l == depth - 1 else None)
        xs = xs_pad[:dec_batch]
    return (xp, jnp.stack(pool_p), kp_rows, vp_rows), (xs, jnp.stack(pool_s), ks, vs)


def kernel(x_prompt, x_sample, mem_prompt, state_pool, cache_win_k, cache_win_v, cache_mem_k, cache_mem_v,
           norm_mix, norm_mem_q, norm_mem_kv, norm_ffn, pool_w_in, pool_w_group, pool_scale, pool_w_out,
           norm_kv, w_k_shared, w_v_shared, dil_w_q, dil_w_o, mem_w_q, mem_w_k, mem_w_v, mem_w_o,
           ffn_w_gate, ffn_w_up, ffn_w_down, norm_final):
    batch, seq, d = x_prompt.shape
    dec_batch, dec_seq, _ = x_sample.shape
    depth = norm_mix.shape[0]
    n_mem = mem_prompt.shape[1]
    rows3 = lambda a: a.reshape(a.shape[0], 1, a.shape[1])
    w = dict(norm_mix=rows3(norm_mix), norm_mem_q=rows3(norm_mem_q), norm_ffn=rows3(norm_ffn),
             norm_kv=norm_kv.reshape(1, 1, d), norm_final=norm_final, pool_scale=rows3(pool_scale),
             pool_w_in=pool_w_in.astype(BF16), pool_w_group=pool_w_group.astype(BF16),
             pool_w_out=pool_w_out.astype(BF16), dil_w_q=dil_w_q.astype(BF16), dil_w_o=dil_w_o.astype(BF16),
             w_k_shared=w_k_shared[None], w_v_shared=w_v_shared[None],
             mem_w_q=mem_w_q, mem_w_o=mem_w_o, ffn_w_gate=ffn_w_gate, ffn_w_up=ffn_w_up, ffn_w_down=ffn_w_down)

    mem_rows = mem_prompt.reshape(batch * n_mem, d)
    norm_mem_kv3 = rows3(norm_mem_kv)
    mk_p, mv_p, mk_rows, mv_rows = [], [], [], []
    for l in range(depth):
        mk, mv, mkr, mvr = shared_kv(mem_rows, (norm_mem_kv3, l), (mem_w_k, l), (mem_w_v, l), tm=512,
                                     name=f"mem_kv{l}")
        mk_p.append(mk.reshape(batch, n_mem, -1))
        mv_p.append(mv.reshape(batch, n_mem, -1))
        mk_rows.append(mkr)
        mv_rows.append(mvr)
    mk_p, mv_p = jnp.stack(mk_p), jnp.stack(mv_p)

    assert dec_seq == 1
    mk_s = cache_mem_k.reshape(depth, dec_batch, n_mem * MEM_HEADS, MEM_HEAD_DIM)
    mv_s = cache_mem_v.reshape(depth, dec_batch, n_mem * MEM_HEADS, MEM_HEAD_DIM)
    (y_p, pool_p, k_p, v_p), (y_s, pool_s, k_s, v_s) = _trunks(
        x_prompt.reshape(batch * seq, d), x_sample.reshape(dec_batch, d), (mk_p, mv_p), (mk_s, mv_s), w,
        batch=batch, seq=seq, pool_state=state_pool, win_k=cache_win_k, win_v=cache_win_v)

    max_window = max(wd for wd, _ in DIL_PATTERNS)
    keep_from = max(0, seq - max_window)
    kv_shape = (N_KV_HEADS, HEAD_DIM)
    mem_shape = (depth, batch, n_mem, MEM_HEADS, MEM_HEAD_DIM)
    return (y_p.reshape(batch, seq, d), y_s.reshape(dec_batch, dec_seq, d), pool_p, pool_s,
            k_p.reshape(batch, seq, *kv_shape)[:, keep_from:], v_p.reshape(batch, seq, *kv_shape)[:, keep_from:],
            k_s.reshape(dec_batch, dec_seq, *kv_shape), v_s.reshape(dec_batch, dec_seq, *kv_shape),
            jnp.stack(mk_rows).reshape(mem_shape), jnp.stack(mv_rows).reshape(mem_shape))
```

```python
import functools

import jax
import jax.numpy as jnp
from jax import lax
from jax.experimental import pallas as pl
from jax.experimental.pallas import tpu as pltpu

F32 = jnp.float32
BF16 = jnp.bfloat16

RMS_EPS = 1e-6
POOL_WINDOWS = (2, 4, 8, 16)
POOL_STATE = max(POOL_WINDOWS) - 1
POOL_HALO = 16
POOL_CHUNK = 64
POOL_TM = 512
POOL_PART = 256
EXTRA_ROWS = 16
DIL_PATTERNS = ((128, 1), (512, 4), (2048, 16))
DIL_BLOCK = 128
DIL_GROUP = 8
MERGE_ROWS = 256
MEM_ROWS = 256
HEAD_DIM = 128
N_KV_HEADS = 4
HEADS_PER_GROUP = 8
KV_REP = HEADS_PER_GROUP // N_KV_HEADS
MEM_HEADS = 4
MEM_HEAD_DIM = 128
LANES = 128
NEG = -1e30
VMEM_LIMIT = 56 * 1024 * 1024
MIN_ROWS = 8

NT_DIMS = (((1,), (1,)), ((), ()))


def _params(*sem):
    return pltpu.CompilerParams(dimension_semantics=sem, vmem_limit_bytes=VMEM_LIMIT)


def _layer_spec(layer, block, index_map):
    return pl.BlockSpec((None,) + tuple(block), lambda *g: (layer,) + tuple(index_map(*g)))


def _resident_spec(layer, block):
    zeros = (0,) * len(block)
    return pl.BlockSpec((None,) + tuple(block), lambda *g: (layer,) + zeros, pipeline_mode=pl.Buffered(1))


def _rms(x, g):
    ms = jnp.mean(x * x, axis=-1, keepdims=True)
    return x * lax.rsqrt(ms + RMS_EPS) * g


def _lhs_scratch_dtype(rows):
    return BF16 if rows % 16 == 0 else F32


def _softmax_rows(s):
    m = jnp.max(s, axis=-1, keepdims=True)
    e = jnp.exp(s - m)
    den = jnp.sum(e, axis=-1, keepdims=True)
    return e / den, m, den


def _norm_matmul_kernel(x_ref, g_ref, *refs, n_w):
    w_refs, o_refs, h_ref = refs[:n_w], refs[n_w:2 * n_w], refs[2 * n_w]

    @pl.when(pl.program_id(1) == 0)
    def _():
        h_ref[...] = _rms(x_ref[...], g_ref[...]).astype(h_ref.dtype)

    h = h_ref[...].astype(BF16)
    for w_ref, o_ref in zip(w_refs, o_refs):
        o_ref[...] = jnp.dot(h, w_ref[...].astype(BF16), preferred_element_type=F32).astype(o_ref.dtype)


def norm_matmul(x, g, ws, out_dtypes, *, tm, tn, name):
    m, d = x.shape
    gains, lg = g
    n = ws[0][0].shape[2]
    tm, tn = min(tm, m), min(tn, n)
    n_w = len(ws)
    return pl.pallas_call(
        functools.partial(_norm_matmul_kernel, n_w=n_w),
        grid=(m // tm, n // tn),
        in_specs=[pl.BlockSpec((tm, d), lambda i, j: (i, 0)),
                  _layer_spec(lg, (1, d), lambda i, j: (0, 0))]
                 + [_layer_spec(lw, (d, tn), lambda i, j: (0, j)) for _, lw in ws],
        out_specs=[pl.BlockSpec((tm, tn), lambda i, j: (i, j))] * n_w,
        out_shape=[jax.ShapeDtypeStruct((m, n), dt) for dt in out_dtypes],
        scratch_shapes=[pltpu.VMEM((tm, d), _lhs_scratch_dtype(tm))],
        compiler_params=_params("parallel", "arbitrary"),
        name=name,
    )(x, gains, *[w for w, _ in ws])


def _shared_kv_kernel(x_ref, g_ref, wk_ref, wv_ref, k_ref, v_ref, kh_ref, vh_ref):
    tm = x_ref.shape[0]
    h = _rms(x_ref[...], g_ref[...]).astype(BF16)
    for w_ref, o_ref, oh_ref in ((wk_ref, k_ref, kh_ref), (wv_ref, v_ref, vh_ref)):
        r = jnp.dot(h, w_ref[...].astype(BF16), preferred_element_type=F32)
        o_ref[...] = r
        for hd in range(N_KV_HEADS):
            oh_ref[pl.ds(hd, tm, stride=N_KV_HEADS), :] = r[:, hd * HEAD_DIM:(hd + 1) * HEAD_DIM]


def shared_kv(x, gains, w_k, w_v, *, tm, name):
    m, d = x.shape
    n_layers, _, n = w_k.shape
    layer_rows = lambda l, i: (l, i, 0)
    per_layer = lambda l, i: (l, 0, 0)
    return pl.pallas_call(
        _shared_kv_kernel,
        grid=(n_layers, m // tm),
        in_specs=[pl.BlockSpec((tm, d), lambda l, i: (i, 0)), pl.BlockSpec((None, 1, d), per_layer),
                  pl.BlockSpec((None, d, n), per_layer), pl.BlockSpec((None, d, n), per_layer)],
        out_specs=[pl.BlockSpec((None, tm, n), layer_rows)] * 2
                  + [pl.BlockSpec((None, tm * N_KV_HEADS, HEAD_DIM), layer_rows)] * 2,
        out_shape=[jax.ShapeDtypeStruct((n_layers, m, n), F32)] * 2
                  + [jax.ShapeDtypeStruct((n_layers, m * N_KV_HEADS, HEAD_DIM), F32)] * 2,
        compiler_params=_params("parallel", "parallel"),
        name=name,
    )(x, gains, w_k, w_v)


def _norm_matmul_resident_kernel(x_ref, g_ref, w_ref, o_ref):
    h = _rms(x_ref[...], g_ref[...]).astype(BF16)
    o_ref[...] = jnp.dot(h, w_ref[...], preferred_element_type=F32).astype(o_ref.dtype)


def norm_matmul_resident(x, g, w, out_dtype, *, tm, name):
    m, d = x.shape
    gains, lg = g
    w, lw = w
    n = w.shape[2]
    return pl.pallas_call(
        _norm_matmul_resident_kernel,
        grid=(m // tm,),
        in_specs=[pl.BlockSpec((tm, d), lambda i: (i, 0)),
                  _layer_spec(lg, (1, d), lambda i: (0, 0)),
                  _resident_spec(lw, (d, n))],
        out_specs=pl.BlockSpec((tm, n), lambda i: (i, 0)),
        out_shape=jax.ShapeDtypeStruct((m, n), out_dtype),
        compiler_params=_params("parallel"),
        name=name,
    )(x, gains, w)


def _matmul_res_kernel(a_ref, w_ref, x_ref, o_ref):
    a = a_ref[...].astype(BF16)
    o_ref[...] = x_ref[...] + jnp.dot(a, w_ref[...].astype(BF16), preferred_element_type=F32)


def matmul_residual(a, w, x, *, tm, tn, name):
    m, k = a.shape
    w, lw = w
    n = w.shape[2]
    tm, tn = min(tm, m), min(tn, n)
    return pl.pallas_call(
        _matmul_res_kernel,
        grid=(m // tm, n // tn),
        in_specs=[pl.BlockSpec((tm, k), lambda i, j: (i, 0)),
                  _layer_spec(lw, (k, tn), lambda i, j: (0, j)),
                  pl.BlockSpec((tm, tn), lambda i, j: (i, j))],
        out_specs=pl.BlockSpec((tm, tn), lambda i, j: (i, j)),
        out_shape=jax.ShapeDtypeStruct((m, n), F32),
        compiler_params=_params("parallel", "parallel"),
        name=name,
    )(a, w, x)


def _ffn_kernel(x_ref, xs_ref, g_ref, wg_ref, wu_ref, wd_ref, *rest, final_norm):
    if final_norm:
        gf_ref, o_ref, os_ref, h_ref = rest
    else:
        o_ref, os_ref, h_ref = rest
    i, j = pl.program_id(0), pl.program_id(1)
    tm = x_ref.shape[0]
    first_tile = i == 0

    @pl.when(j == 0)
    def _():
        x = x_ref[...]
        h_ref[0:tm, :] = _rms(x, g_ref[...]).astype(BF16)
        o_ref[...] = x

    @pl.when((j == 0) & first_tile)
    def _():
        xs = xs_ref[...]
        h_ref[tm:, :] = _rms(xs, g_ref[...]).astype(BF16)
        os_ref[...] = xs

    h = h_ref[...]
    gate = jnp.dot(h, wg_ref[...].astype(BF16), preferred_element_type=F32)
    up = jnp.dot(h, wu_ref[...].astype(BF16), preferred_element_type=F32)
    act = (gate * jax.nn.sigmoid(gate) * up).astype(BF16)
    wd = wd_ref[...].astype(BF16)
    o_ref[...] += jnp.dot(act[:tm], wd, preferred_element_type=F32)

    @pl.when(first_tile)
    def _():
        os_ref[...] += jnp.dot(act[tm:], wd, preferred_element_type=F32)

    if final_norm:
        last = j == pl.num_programs(1) - 1

        @pl.when(last)
        def _():
            o_ref[...] = _rms(o_ref[...], gf_ref[...])

        @pl.when(last & first_tile)
        def _():
            os_ref[...] = _rms(os_ref[...], gf_ref[...])


def ffn(x, xs, g, w_gate, w_up, w_down, layer, *, tm, tf, name, final_gain=None):
    m, d = x.shape
    rows_s = xs.shape[0]
    assert rows_s == EXTRA_ROWS and m % tm == 0
    gains, lg = g
    f = w_gate.shape[2]
    final_norm = final_gain is not None
    extra_specs = [pl.BlockSpec((1, d), lambda i, j: (0, 0))] if final_norm else []
    extra_args = [final_gain.reshape(1, d)] if final_norm else []
    return pl.pallas_call(
        functools.partial(_ffn_kernel, final_norm=final_norm),
        grid=(m // tm, f // tf),
        in_specs=[pl.BlockSpec((tm, d), lambda i, j: (i, 0)),
                  pl.BlockSpec((rows_s, d), lambda i, j: (0, 0)),
                  _layer_spec(lg, (1, d), lambda i, j: (0, 0)),
                  _layer_spec(layer, (d, tf), lambda i, j: (0, j)),
                  _layer_spec(layer, (d, tf), lambda i, j: (0, j)),
                  _layer_spec(layer, (tf, d), lambda i, j: (j, 0))] + extra_specs,
        out_specs=[pl.BlockSpec((tm, d), lambda i, j: (i, 0)), pl.BlockSpec((rows_s, d), lambda i, j: (0, 0))],
        out_shape=[jax.ShapeDtypeStruct((m, d), F32), jax.ShapeDtypeStruct((rows_s, d), F32)],
        scratch_shapes=[pltpu.VMEM((tm + rows_s, d), BF16)],
        compiler_params=_params("arbitrary", "arbitrary"),
        name=name,
    )(x, xs, gains, w_gate, w_up, w_down, *extra_args)


def _pool_block_kernel(x_ref, g_ref, win_ref, wg_ref, scale_ref, wout_ref, gm_ref, wq_ref, k_ref, v_ref, wo_ref,
                       o_ref, tail_ref, ext_ref, p_ref, *, tm, tiles_per_seq):
    d = x_ref.shape[1]
    gdim = d // len(POOL_WINDOWS)
    t_in_seq = pl.program_id(0) % tiles_per_seq

    @pl.when(t_in_seq == 0)
    def _():
        ext_ref[0:POOL_HALO, :] = jnp.zeros((POOL_HALO, d), F32)

    @pl.when(t_in_seq != 0)
    def _():
        ext_ref[0:POOL_HALO, :] = ext_ref[tm:tm + POOL_HALO, :]

    parts = [slice(c * POOL_PART, (c + 1) * POOL_PART) for c in range(tm // POOL_PART)]
    us = [jnp.dot(_rms(x_ref[rows, :], g_ref[...]).astype(BF16), win_ref[...], preferred_element_type=F32)
          for rows in parts]
    for rows, u in zip(parts, us):
        ext_ref[POOL_HALO + rows.start:POOL_HALO + rows.stop, :] = u
    tail_ref[...] = us[-1][POOL_PART - POOL_HALO:, :]
    ys = []
    for rows in parts:
        for c in range(rows.start // POOL_CHUNK, rows.stop // POOL_CHUNK):
            r0 = POOL_HALO + c * POOL_CHUNK
            pos = t_in_seq * tm + c * POOL_CHUNK + lax.broadcasted_iota(jnp.int32, (POOL_CHUNK, 1), 0)
            for gi, w in enumerate(POOL_WINDOWS):
                cols = slice(gi * gdim, (gi + 1) * gdim)
                u_q = ext_ref[r0:r0 + POOL_CHUNK, cols]
                acc = u_q
                for k in range(1, w):
                    acc = acc + ext_ref[r0 - k:r0 - k + POOL_CHUNK, cols]
                inv_cnt = 1.0 / jnp.minimum(pos + 1, w).astype(F32)
                p_ref[c * POOL_CHUNK:(c + 1) * POOL_CHUNK, cols] = (acc * inv_cnt - u_q).astype(BF16)
        zs = []
        for gi in range(len(POOL_WINDOWS)):
            cols = slice(gi * gdim, (gi + 1) * gdim)
            z = jnp.dot(p_ref[rows, cols], wg_ref[gi], preferred_element_type=F32)
            zs.append((z * scale_ref[:, cols]).astype(BF16))
        ys.append(x_ref[rows, :] + jnp.dot(jnp.concatenate(zs, axis=1), wout_ref[...],
                                            preferred_element_type=F32))
    for rows, y in zip(parts, _mem_rows(ys, gm_ref, wq_ref, k_ref, v_ref, wo_ref)):
        o_ref[rows, :] = y


def pool_mem_block(x, g, w_in, w_group, scale, w_out, g_mem, w_q, mk, mv, w_o, layer, *, seq, tm, name):
    m, d = x.shape
    gains, lg = g
    gains_mem, lgm = g_mem
    n_mem, dm = mk.shape[2], mk.shape[3]
    tiles_per_seq = seq // tm
    return pl.pallas_call(
        functools.partial(_pool_block_kernel, tm=tm, tiles_per_seq=tiles_per_seq),
        grid=(m // tm,),
        in_specs=[pl.BlockSpec((tm, d), lambda i: (i, 0)),
                  _layer_spec(lg, (1, d), lambda i: (0, 0)),
                  _resident_spec(layer, (d, d)),
                  _resident_spec(layer, w_group.shape[1:]),
                  _layer_spec(layer, (1, d), lambda i: (0, 0)),
                  _resident_spec(layer, (d, d))]
                 + _mem_specs(layer, lgm, d, dm, n_mem, lambda i: i // tiles_per_seq),
        out_specs=[pl.BlockSpec((tm, d), lambda i: (i, 0)),
                   pl.BlockSpec((None, POOL_HALO, d), lambda i: (i // tiles_per_seq, 0, 0))],
        out_shape=[jax.ShapeDtypeStruct((m, d), F32), jax.ShapeDtypeStruct((m // seq, POOL_HALO, d), F32)],
        scratch_shapes=[pltpu.VMEM((tm + POOL_HALO, d), F32), pltpu.VMEM((tm, d), BF16)],
        compiler_params=_params("arbitrary"),
        name=name,
    )(x, gains, w_in, w_group, scale, w_out, gains_mem, w_q, mk, mv, w_o)


def _pool_sample_kernel(state_ref, u_ref, wg_ref, scale_ref, z_ref):
    n_state, _, d = state_ref.shape
    gdim = d // len(POOL_WINDOWS)
    u = u_ref[...]
    for gi, w in enumerate(POOL_WINDOWS):
        cols = slice(gi * gdim, (gi + 1) * gdim)
        u_q = u[:, cols]
        acc = u_q
        for k in range(1, w):
            acc = acc + state_ref[n_state - k, :, cols]
        cnt = float(min(n_state + 1, w))
        p = (acc / cnt - u_q).astype(BF16)
        z = jnp.dot(p, wg_ref[gi].astype(BF16), preferred_element_type=F32)
        z_ref[:, cols] = z * scale_ref[:, cols]


def pool_sample(state_t, u, w_group, scale, layer, *, name):
    b, d = u.shape
    return pl.pallas_call(
        _pool_sample_kernel,
        grid=(1,),
        in_specs=[pl.BlockSpec(state_t.shape, lambda i: (0, 0, 0)),
                  pl.BlockSpec((b, d), lambda i: (0, 0)),
                  _layer_spec(layer, w_group.shape[1:], lambda i: (0, 0, 0)),
                  _layer_spec(layer, (1, d), lambda i: (0, 0))],
        out_specs=pl.BlockSpec((b, d), lambda i: (0, 0)),
        out_shape=jax.ShapeDtypeStruct((b, d), F32),
        compiler_params=_params("arbitrary"),
        name=name,
    )(state_t, u, w_group, scale)


def _head_cols(hd):
    return slice(hd * MEM_HEAD_DIM, (hd + 1) * MEM_HEAD_DIM)


def _mem_rows(xs, g_ref, wq_ref, k_ref, v_ref, wo_ref):
    wq = wq_ref[...].astype(BF16)
    kb = k_ref[...].astype(BF16)
    vb = v_ref[...].astype(BF16)
    scale = MEM_HEAD_DIM ** -0.5
    hs = [_rms(x, g_ref[...]).astype(BF16) for x in xs]
    qs = [jnp.dot(h, wq, preferred_element_type=F32).astype(BF16) for h in hs]
    ss = [[lax.dot_general(q[:, _head_cols(hd)], kb[:, _head_cols(hd)], NT_DIMS, preferred_element_type=F32) * scale
           for hd in range(MEM_HEADS)] for q in qs]
    ps = [[_softmax_rows(s)[0].astype(BF16) for s in row] for row in ss]
    os_ = [jnp.concatenate([jnp.dot(p, vb[:, _head_cols(hd)], preferred_element_type=F32).astype(BF16)
                            for hd, p in enumerate(row)], axis=1) for row in ps]
    wo = wo_ref[...].astype(BF16)
    return [x + jnp.dot(o, wo, preferred_element_type=F32) for x, o in zip(xs, os_)]


def _mem_specs(layer, gains_layer, d, dm, n_mem, seq_of_step):
    kv_spec = pl.BlockSpec((None, None, n_mem, dm), lambda i: (layer, seq_of_step(i), 0, 0))
    return [_layer_spec(gains_layer, (1, d), lambda i: (0, 0)), _resident_spec(layer, (d, dm)),
            kv_spec, kv_spec, _resident_spec(layer, (dm, d))]


def _mem_sample_kernel(x_ref, g_ref, wq_ref, k_ref, v_ref, wo_ref, o_ref):
    n_b = x_ref.shape[0]
    n_mem = k_ref.shape[1] // MEM_HEADS
    head_rows = lambda ref, b, hd: ref[b, pl.ds(hd, n_mem, stride=MEM_HEADS), :].astype(BF16)
    x = x_ref[...]
    h = _rms(x, g_ref[...]).astype(BF16)
    q = jnp.dot(h, wq_ref[...].astype(BF16), preferred_element_type=F32).astype(BF16)
    scale = MEM_HEAD_DIM ** -0.5
    row = lax.broadcasted_iota(jnp.int32, (n_b, 1), 0)
    pairs = [(b, hd) for b in range(n_b) for hd in range(MEM_HEADS)]
    ss = [lax.dot_general(q[:, _head_cols(hd)], head_rows(k_ref, b, hd), NT_DIMS,
                          preferred_element_type=F32) * scale for b, hd in pairs]
    ps = [_softmax_rows(s)[0].astype(BF16) for s in ss]
    outs = [jnp.zeros((n_b, MEM_HEAD_DIM), F32) for _ in range(MEM_HEADS)]
    for (b, hd), p in zip(pairs, ps):
        o_b = jnp.dot(p, head_rows(v_ref, b, hd), preferred_element_type=F32)
        outs[hd] = jnp.where(row == b, o_b, outs[hd])
    o = jnp.concatenate(outs, axis=1).astype(BF16)
    o_ref[...] = x + jnp.dot(o, wo_ref[...].astype(BF16), preferred_element_type=F32)


def mem_block_sample(x, g, w_q, mk, mv, w_o, layer, *, name):
    n_b, d = x.shape
    gains, lg = g
    dm = w_q.shape[2]
    kv_spec = _layer_spec(layer, mk.shape[1:], lambda i: (0, 0, 0))
    x_spec = pl.BlockSpec((n_b, d), lambda i: (0, 0))
    return pl.pallas_call(
        _mem_sample_kernel,
        grid=(1,),
        in_specs=[x_spec,
                  _layer_spec(lg, (1, d), lambda i: (0, 0)),
                  _layer_spec(layer, (d, dm), lambda i: (0, 0)),
                  kv_spec, kv_spec,
                  _layer_spec(layer, (dm, d), lambda i: (0, 0))],
        out_specs=x_spec,
        out_shape=jax.ShapeDtypeStruct((n_b, d), F32),
        compiler_params=_params("arbitrary"),
        name=name,
    )(x, gains, w_q, mk, mv, w_o)


def _strided_rows(ref, start, dil):
    if dil == 1:
        return ref[pl.ds(start, DIL_BLOCK), :]
    return ref[pl.ds(start, DIL_BLOCK, stride=dil), :]


def _dil_group_attention(q_refs, k_ref, v_ref, o_ref, lse_ref, *, dil, seq):
    span = dil * DIL_BLOCK
    nb = seq // span
    qi = lax.broadcasted_iota(jnp.int32, (DIL_BLOCK, 2 * DIL_BLOCK), 0)
    kj = lax.broadcasted_iota(jnp.int32, (DIL_BLOCK, 2 * DIL_BLOCK), 1)
    dist = qi + DIL_BLOCK - kj
    in_window = (dist >= 0) & (dist <= DIL_BLOCK)
    causal = (lax.broadcasted_iota(jnp.int32, (DIL_BLOCK, DIL_BLOCK), 0)
              >= lax.broadcasted_iota(jnp.int32, (DIL_BLOCK, DIL_BLOCK), 1))
    scale = HEAD_DIM ** -0.5

    def store_rows(ref, lead, start, val):
        rows = pl.ds(start, DIL_BLOCK) if dil == 1 else pl.ds(start, DIL_BLOCK, stride=dil)
        ref[lead + (rows, slice(None))] = val

    def scores(jb):
        if nb > 1:
            r, b = jb // nb, jb % nb
            start = b * span + r
        else:
            b, start = 0, jb
        if dil == 1:
            start = pl.multiple_of(start, DIL_BLOCK)
        q2 = jnp.concatenate([_strided_rows(q, start, dil) for q in q_refs], axis=0).astype(BF16)
        k_cur = _strided_rows(k_ref, start, dil)
        v_cur = _strided_rows(v_ref, start, dil)
        if nb > 1:
            prev = jnp.maximum(b - 1, 0) * span + (start - b * span)
            if dil == 1:
                prev = pl.multiple_of(prev, DIL_BLOCK)
            kcat = jnp.concatenate([_strided_rows(k_ref, prev, dil), k_cur], axis=0).astype(BF16)
            vcat = jnp.concatenate([_strided_rows(v_ref, prev, dil), v_cur], axis=0).astype(BF16)
            first_key = jnp.where(b > 0, 0, DIL_BLOCK)
            valid = in_window & (kj >= first_key)
        else:
            kcat, vcat, valid = k_cur.astype(BF16), v_cur.astype(BF16), causal
        s = lax.dot_general(q2, kcat, NT_DIMS, preferred_element_type=F32) * scale
        return start, jnp.where(jnp.concatenate([valid] * KV_REP, axis=0), s, NEG), vcat

    def block_group(jg, carry):
        staged = [scores(jg * DIL_GROUP + i) for i in range(DIL_GROUP)]
        probs = [_softmax_rows(s) for _, s, _ in staged]
        outs = [jnp.dot(p.astype(BF16), vcat, preferred_element_type=F32)
                for (p, _, _), (_, _, vcat) in zip(probs, staged)]
        for (start, _, _), (_, m, den), o2 in zip(staged, probs, outs):
            lse2 = m + jnp.log(den)
            for hh in range(KV_REP):
                rows = slice(hh * DIL_BLOCK, (hh + 1) * DIL_BLOCK)
                store_rows(o_ref, (hh,), start, o2[rows])
                store_rows(lse_ref, (hh,), start, jnp.broadcast_to(lse2[rows], (DIL_BLOCK, HEAD_DIM)))
        return carry

    lax.fori_loop(0, seq // (DIL_BLOCK * DIL_GROUP), block_group, 0)


def _dil_prompt_kernel(*refs, seq):
    n_g = len(DIL_PATTERNS)
    q_refs = refs[:n_g * KV_REP]
    k_ref, v_ref, out_ref, o_scr, lse_scr = refs[n_g * KV_REP:]
    for gi, (_, dil) in enumerate(DIL_PATTERNS):
        _dil_group_attention(q_refs[gi * KV_REP:(gi + 1) * KV_REP], k_ref, v_ref, o_scr.at[gi], lse_scr.at[gi],
                             dil=dil, seq=seq)

    def merge(c, carry):
        rows = pl.ds(pl.multiple_of(c * MERGE_ROWS, MERGE_ROWS), MERGE_ROWS)
        for hh in range(KV_REP):
            lses = [lse_scr[gi, hh, rows, :] for gi in range(n_g)]
            m = functools.reduce(jnp.maximum, lses)
            es = [jnp.exp(l - m) for l in lses]
            den = functools.reduce(jnp.add, es)
            acc = sum((e / den) * o_scr[gi, hh, rows, :] for gi, e in enumerate(es))
            out_ref[hh, rows, :] = acc.astype(out_ref.dtype)
        return carry

    lax.fori_loop(0, seq // MERGE_ROWS, merge, 0)


def dil_attention_prompt(q, k, v, *, name):
    n, s, _ = q.shape
    n_g = len(DIL_PATTERNS)
    q_map = lambda head: (lambda b, j: (b, 0, head + KV_REP * j))
    q_specs = [pl.BlockSpec((None, s, HEAD_DIM), q_map(gi * HEADS_PER_GROUP + hh))
               for gi in range(n_g) for hh in range(KV_REP)]
    head_block = pl.BlockSpec((None, s, HEAD_DIM), lambda b, j: (b, 0, j))
    return pl.pallas_call(
        functools.partial(_dil_prompt_kernel, seq=s),
        grid=(n, N_KV_HEADS),
        in_specs=q_specs + [head_block, head_block],
        out_specs=pl.BlockSpec((None, KV_REP, s, HEAD_DIM), lambda b, j: (b, j, 0, 0)),
        out_shape=jax.ShapeDtypeStruct((n, HEADS_PER_GROUP, s, HEAD_DIM), BF16),
        scratch_shapes=[pltpu.VMEM((n_g, KV_REP, s, HEAD_DIM), F32), pltpu.VMEM((n_g, KV_REP, s, HEAD_DIM), F32)],
        compiler_params=_params("parallel", "parallel"),
        name=name,
    )(*([q] * (n_g * KV_REP)), k, v)


def _heads_out_mem_kernel(o_ref, w_ref, x_ref, g_ref, wq_ref, k_ref, v_ref, wo_ref, out_ref):
    tm = x_ref.shape[0]
    parts = [slice(c * MEM_ROWS, (c + 1) * MEM_ROWS) for c in range(tm // MEM_ROWS)]
    w = w_ref[...]
    xs = [x_ref[rows, :] + jnp.dot(jnp.concatenate([o_ref[h, rows, :] for h in range(o_ref.shape[0])], axis=1), w,
                                   preferred_element_type=F32) for rows in parts]
    for rows, y in zip(parts, _mem_rows(xs, g_ref, wq_ref, k_ref, v_ref, wo_ref)):
        out_ref[rows, :] = y


def heads_out_mem(o, w, x, g, w_q, mk, mv, w_o, layer, *, seq, tm, name):
    m, d = x.shape
    w, lw = w
    gains, lg = g
    n_mem, dm = mk.shape[2], mk.shape[3]
    tps = seq // tm
    return pl.pallas_call(
        _heads_out_mem_kernel,
        grid=(m // tm,),
        in_specs=[pl.BlockSpec((None,) + o.shape[1:2] + (tm, o.shape[3]), lambda i: (i // tps, 0, i % tps, 0)),
                  _resident_spec(lw, w.shape[1:]),
                  pl.BlockSpec((tm, d), lambda i: (i, 0))]
                 + _mem_specs(layer, lg, d, dm, n_mem, lambda i: i // tps),
        out_specs=pl.BlockSpec((tm, d), lambda i: (i, 0)),
        out_shape=jax.ShapeDtypeStruct((m, d), F32),
        compiler_params=_params("parallel"),
        name=name,
    )(o, w, x, gains, w_q, mk, mv, w_o)


def _dil_sample_kernel(q_ref, kn_ref, vn_ref, k_ref, v_ref, o_ref, *, wbuf):
    scale = HEAD_DIM ** -0.5
    head_row = lax.broadcasted_iota(jnp.int32, (HEADS_PER_GROUP, 1), 0)
    kn = kn_ref[...].astype(BF16).astype(F32)
    vn = vn_ref[...].astype(BF16).astype(F32)
    outs, lses = [], []
    for gi, (window, dil) in enumerate(DIL_PATTERNS):
        qg = q_ref[gi * HEADS_PER_GROUP:(gi + 1) * HEADS_PER_GROUP, :].astype(BF16)
        qf = qg.astype(F32)
        first = wbuf - window
        s = jnp.zeros((HEADS_PER_GROUP, DIL_BLOCK), F32)
        s_new = jnp.zeros((HEADS_PER_GROUP, 1), F32)
        for kv in range(N_KV_HEADS):
            mine = (head_row >= kv * KV_REP) & (head_row < (kv + 1) * KV_REP)
            kc = _strided_rows(k_ref, first * N_KV_HEADS + kv, dil * N_KV_HEADS).astype(BF16)
            s = jnp.where(mine, lax.dot_general(qg, kc, NT_DIMS, preferred_element_type=F32), s)
            s_new = jnp.where(mine, jnp.sum(qf * kn[kv:kv + 1, :], axis=-1, keepdims=True), s_new)
        s = s * scale
        s_new = s_new * scale
        m = jnp.maximum(jnp.max(s, axis=-1, keepdims=True), s_new)
        e = jnp.exp(s - m)
        e_new = jnp.exp(s_new - m)
        den = jnp.sum(e, axis=-1, keepdims=True) + e_new
        p = (e / den).astype(BF16)
        p_new = (e_new / den).astype(BF16).astype(F32)
        o = jnp.zeros((HEADS_PER_GROUP, HEAD_DIM), F32)
        for kv in range(N_KV_HEADS):
            mine = (head_row >= kv * KV_REP) & (head_row < (kv + 1) * KV_REP)
            vc = _strided_rows(v_ref, first * N_KV_HEADS + kv, dil * N_KV_HEADS).astype(BF16)
            o_kv = jnp.dot(p, vc, preferred_element_type=F32) + p_new * vn[kv:kv + 1, :]
            o = jnp.where(mine, o_kv, o)
        outs.append(o)
        lses.append(m + jnp.log(den))
    mm = jnp.maximum(jnp.maximum(lses[0], lses[1]), lses[2])
    es = [jnp.exp(l - mm) for l in lses]
    den = es[0] + es[1] + es[2]
    o_ref[...] = (es[0] / den) * outs[0] + (es[1] / den) * outs[1] + (es[2] / den) * outs[2]


def dil_attention_sample(q, k_new, v_new, cache_k, cache_v, *, name):
    b, qd = q.shape
    wbuf = cache_k.shape[1]
    n_heads = qd // HEAD_DIM
    for window, dil in DIL_PATTERNS:
        assert window // dil == DIL_BLOCK and window <= wbuf
    small = lambda rows: pl.BlockSpec((None, rows, HEAD_DIM), lambda i: (i, 0, 0))
    rows_view = lambda c: c.reshape(b, wbuf * N_KV_HEADS, HEAD_DIM)
    o = pl.pallas_call(
        functools.partial(_dil_sample_kernel, wbuf=wbuf),
        grid=(b,),
        in_specs=[small(n_heads), small(N_KV_HEADS), small(N_KV_HEADS),
                  small(wbuf * N_KV_HEADS), small(wbuf * N_KV_HEADS)],
        out_specs=small(HEADS_PER_GROUP),
        out_shape=jax.ShapeDtypeStruct((b, HEADS_PER_GROUP, HEAD_DIM), F32),
        compiler_params=_params("parallel"),
        name=name,
    )(q.reshape(b, n_heads, HEAD_DIM), k_new.reshape(b, N_KV_HEADS, HEAD_DIM),
      v_new.reshape(b, N_KV_HEADS, HEAD_DIM), rows_view(cache_k), rows_view(cache_v))
    return o.reshape(b, HEADS_PER_GROUP * HEAD_DIM)


def _trunks(xp, xs, mem_p, mem_s, w, *, batch, seq, pool_state, win_k, win_v):
    depth = w["ffn_w_gate"].shape[0]
    n_a = w["pool_w_in"].shape[0]
    d = xp.shape[1]
    dec_batch = xs.shape[0]
    pool_p, pool_s = [], []
    kp = vp = kp_rows = vp_rows = ks = vs = None
    for l in range(depth):
        if l == n_a:
            kv_w = [(w["w_k_shared"], 0), (w["w_v_shared"], 0)]
            kp, vp, kp_rows, vp_rows = (a[0] for a in shared_kv(xp, w["norm_kv"], w["w_k_shared"], w["w_v_shared"],
                                                                 tm=512, name="p_kv"))
            ks, vs = norm_matmul(xs, (w["norm_kv"], 0), kv_w, [F32, F32], tm=1024, tn=512, name="s_kv")
        mem_args = ((w["norm_mem_q"], l), w["mem_w_q"], *mem_p, w["mem_w_o"], l)
        if l < n_a:
            xp, tail = pool_mem_block(xp, (w["norm_mix"], l), w["pool_w_in"], w["pool_w_group"], w["pool_scale"],
                                      w["pool_w_out"], *mem_args, seq=seq, tm=POOL_TM, name=f"p_pool_mem{l}")
            pool_p.append(tail[:, POOL_HALO - POOL_STATE:])
            (u,) = norm_matmul(xs, (w["norm_mix"], l), [(w["pool_w_in"], l)], [F32], tm=1024, tn=512,
                               name=f"s_pool_in{l}")
            state = pool_state[l]
            pool_s.append(jnp.concatenate([state[:, 1:], u[:, None, :]], axis=1))
            z = pool_sample(jnp.swapaxes(state, 0, 1), u, w["pool_w_group"], w["pool_scale"], l, name=f"s_pool{l}")
            xs = matmul_residual(z, (w["pool_w_out"], l), xs, tm=1024, tn=1024, name=f"s_pool_out{l}")
        else:
            j = l - n_a
            q = norm_matmul_resident(xp, (w["norm_mix"], l), (w["dil_w_q"], j), F32, tm=512, name=f"p_dil_q{j}")
            q3, k3, v3 = (a.reshape(batch, seq, -1) for a in (q, kp, vp))
            o = dil_attention_prompt(q3, k3, v3, name=f"p_dil_attn{j}")
            xp = heads_out_mem(o, (w["dil_w_o"], j), xp, *mem_args, seq=seq, tm=512, name=f"p_dil_out_mem{j}")
            (q,) = norm_matmul(xs, (w["norm_mix"], l), [(w["dil_w_q"], j)], [F32], tm=1024, tn=512,
                               name=f"s_dil_q{j}")
            o = dil_attention_sample(q, ks, vs, win_k, win_v, name=f"s_dil_attn{j}")
            xs = matmul_residual(o, (w["dil_w_o"], j), xs, tm=1024, tn=1024, name=f"s_dil_out{j}")
        xs = mem_block_sample(xs, (w["norm_mem_q"], l), w["mem_w_q"], *mem_s, w["mem_w_o"], l, name=f"s_mem{l}")
        xs_pad = jnp.pad(xs, ((0, EXTRA_ROWS - dec_batch), (0, 0)))
        xp, xs_pad = ffn(xp, xs_pad, (w["norm_ffn"], l), w["ffn_w_gate"], w["ffn_w_up"], w["ffn_w_down"], l,
                         tm=1024, tf=256, name=f"ffn{l}", final_gain=w["norm_final"] if l == depth - 1 else None)
        xs = xs_pad[:dec_batch]
    return (xp, jnp.stack(pool_p), kp_rows, vp_rows), (xs, jnp.stack(pool_s), ks, vs)


def kernel(x_prompt, x_sample, mem_prompt, state_pool, cache_win_k, cache_win_v, cache_mem_k, cache_mem_v,
           norm_mix, norm_mem_q, norm_mem_kv, norm_ffn, pool_w_in, pool_w_group, pool_scale, pool_w_out,
           norm_kv, w_k_shared, w_v_shared, dil_w_q, dil_w_o, mem_w_q, mem_w_k, mem_w_v, mem_w_o,
           ffn_w_gate, ffn_w_up, ffn_w_down, norm_final):
    batch, seq, d = x_prompt.shape
    dec_batch, dec_seq, _ = x_sample.shape
    depth = norm_mix.shape[0]
    n_mem = mem_prompt.shape[1]
    rows3 = lambda a: a.reshape(a.shape[0], 1, a.shape[1])
    w = dict(norm_mix=rows3(norm_mix), norm_mem_q=rows3(norm_mem_q), norm_ffn=rows3(norm_ffn),
             norm_kv=norm_kv.reshape(1, 1, d), norm_final=norm_final, pool_scale=rows3(pool_scale),
             pool_w_in=pool_w_in.astype(BF16), pool_w_group=pool_w_group.astype(BF16),
             pool_w_out=pool_w_out.astype(BF16), dil_w_q=dil_w_q.astype(BF16), dil_w_o=dil_w_o.astype(BF16),
             mem_w_q=mem_w_q.astype(BF16), mem_w_o=mem_w_o.astype(BF16),
             w_k_shared=w_k_shared[None], w_v_shared=w_v_shared[None],
             ffn_w_gate=ffn_w_gate, ffn_w_up=ffn_w_up, ffn_w_down=ffn_w_down)

    mk_p, mv_p, mk_rows, mv_rows = shared_kv(mem_prompt.reshape(batch * n_mem, d), rows3(norm_mem_kv),
                                             mem_w_k, mem_w_v, tm=512, name="mem_kv")
    mk_p = mk_p.reshape(depth, batch, n_mem, -1)
    mv_p = mv_p.reshape(depth, batch, n_mem, -1)

    assert dec_seq == 1
    mk_s = cache_mem_k.reshape(depth, dec_batch, n_mem * MEM_HEADS, MEM_HEAD_DIM)
    mv_s = cache_mem_v.reshape(depth, dec_batch, n_mem * MEM_HEADS, MEM_HEAD_DIM)
    (y_p, pool_p, k_p, v_p), (y_s, pool_s, k_s, v_s) = _trunks(
        x_prompt.reshape(batch * seq, d), x_sample.reshape(dec_batch, d), (mk_p, mv_p), (mk_s, mv_s), w,
        batch=batch, seq=seq, pool_state=state_pool, win_k=cache_win_k, win_v=cache_win_v)

    max_window = max(wd for wd, _ in DIL_PATTERNS)
    keep_from = max(0, seq - max_window)
    kv_shape = (N_KV_HEADS, HEAD_DIM)
    mem_shape = (depth, batch, n_mem, MEM_HEADS, MEM_HEAD_DIM)
    return (y_p.reshape(batch, seq, d), y_s.reshape(dec_batch, dec_seq, d), pool_p, pool_s,
            k_p.reshape(batch, seq, *kv_shape)[:, keep_from:], v_p.reshape(batch, seq, *kv_shape)[:, keep_from:],
            k_s.reshape(dec_batch, dec_seq, *kv_shape), v_s.reshape(dec_batch, dec_seq, *kv_shape),
            mk_rows.reshape(mem_shape), mv_rows.reshape(mem_shape))
```

```python
import functools

import jax
import jax.numpy as jnp
from jax import lax
from jax.experimental import pallas as pl
from jax.experimental.pallas import tpu as pltpu

F32 = jnp.float32
BF16 = jnp.bfloat16

RMS_EPS = 1e-6
POOL_WINDOWS = (2, 4, 8, 16)
POOL_STATE = max(POOL_WINDOWS) - 1
POOL_HALO = 16
POOL_CHUNK = 64
POOL_TM = 512
POOL_PART = 256
EXTRA_ROWS = 16
DIL_PATTERNS = ((128, 1), (512, 4), (2048, 16))
DIL_BLOCK = 128
DIL_GROUP = 8
MERGE_ROWS = 256
MEM_ROWS = 256
HEAD_DIM = 128
N_KV_HEADS = 4
HEADS_PER_GROUP = 8
KV_REP = HEADS_PER_GROUP // N_KV_HEADS
MEM_HEADS = 4
MEM_HEAD_DIM = 128
NEG = -1e30
VMEM_LIMIT = 56 * 1024 * 1024

NT_DIMS = (((1,), (1,)), ((), ()))


def _params(*sem):
    return pltpu.CompilerParams(dimension_semantics=sem, vmem_limit_bytes=VMEM_LIMIT)


def _layer_spec(layer, block, index_map):
    return pl.BlockSpec((None,) + tuple(block), lambda *g: (layer,) + tuple(index_map(*g)))


def _resident_spec(layer, block):
    zeros = (0,) * len(block)
    return pl.BlockSpec((None,) + tuple(block), lambda *g: (layer,) + zeros, pipeline_mode=pl.Buffered(1))


def _rms(x, g):
    ms = jnp.mean(x * x, axis=-1, keepdims=True)
    return x * lax.rsqrt(ms + RMS_EPS) * g


def _lhs_scratch_dtype(rows):
    return BF16 if rows % 16 == 0 else F32


def _softmax_rows(s):
    m = jnp.max(s, axis=-1, keepdims=True)
    e = jnp.exp(s - m)
    den = jnp.sum(e, axis=-1, keepdims=True)
    return e / den, m, den


def _norm_matmul_kernel(x_ref, g_ref, *refs, n_w):
    w_refs, o_refs, h_ref = refs[:n_w], refs[n_w:2 * n_w], refs[2 * n_w]

    @pl.when(pl.program_id(1) == 0)
    def _():
        h_ref[...] = _rms(x_ref[...], g_ref[...]).astype(h_ref.dtype)

    h = h_ref[...].astype(BF16)
    for w_ref, o_ref in zip(w_refs, o_refs):
        o_ref[...] = jnp.dot(h, w_ref[...].astype(BF16), preferred_element_type=F32).astype(o_ref.dtype)


def norm_matmul(x, g, ws, out_dtypes, *, tm, tn, name):
    m, d = x.shape
    gains, lg = g
    n = ws[0][0].shape[2]
    tm, tn = min(tm, m), min(tn, n)
    n_w = len(ws)
    return pl.pallas_call(
        functools.partial(_norm_matmul_kernel, n_w=n_w),
        grid=(m // tm, n // tn),
        in_specs=[pl.BlockSpec((tm, d), lambda i, j: (i, 0)),
                  _layer_spec(lg, (1, d), lambda i, j: (0, 0))]
                 + [_layer_spec(lw, (d, tn), lambda i, j: (0, j)) for _, lw in ws],
        out_specs=[pl.BlockSpec((tm, tn), lambda i, j: (i, j))] * n_w,
        out_shape=[jax.ShapeDtypeStruct((m, n), dt) for dt in out_dtypes],
        scratch_shapes=[pltpu.VMEM((tm, d), _lhs_scratch_dtype(tm))],
        compiler_params=_params("parallel", "arbitrary"),
        name=name,
    )(x, gains, *[w for w, _ in ws])


def _shared_kv_kernel(x_ref, g_ref, wk_ref, wv_ref, k_ref, v_ref, kh_ref, vh_ref):
    tm = x_ref.shape[0]
    h = _rms(x_ref[...], g_ref[...]).astype(BF16)
    for w_ref, o_ref, oh_ref in ((wk_ref, k_ref, kh_ref), (wv_ref, v_ref, vh_ref)):
        r = jnp.dot(h, w_ref[...].astype(BF16), preferred_element_type=F32)
        o_ref[...] = r
        for hd in range(N_KV_HEADS):
            oh_ref[pl.ds(hd, tm, stride=N_KV_HEADS), :] = r[:, hd * HEAD_DIM:(hd + 1) * HEAD_DIM]


def shared_kv(x, gains, w_k, w_v, *, tm, name):
    m, d = x.shape
    n_layers, _, n = w_k.shape
    layer_rows = lambda l, i: (l, i, 0)
    per_layer = lambda l, i: (l, 0, 0)
    return pl.pallas_call(
        _shared_kv_kernel,
        grid=(n_layers, m // tm),
        in_specs=[pl.BlockSpec((tm, d), lambda l, i: (i, 0)), pl.BlockSpec((None, 1, d), per_layer),
                  pl.BlockSpec((None, d, n), per_layer), pl.BlockSpec((None, d, n), per_layer)],
        out_specs=[pl.BlockSpec((None, tm, n), layer_rows)] * 2
                  + [pl.BlockSpec((None, tm * N_KV_HEADS, HEAD_DIM), layer_rows)] * 2,
        out_shape=[jax.ShapeDtypeStruct((n_layers, m, n), F32)] * 2
                  + [jax.ShapeDtypeStruct((n_layers, m * N_KV_HEADS, HEAD_DIM), F32)] * 2,
        compiler_params=_params("parallel", "parallel"),
        name=name,
    )(x, gains, w_k, w_v)


def _norm_matmul_resident_kernel(x_ref, g_ref, w_ref, o_ref):
    h = _rms(x_ref[...], g_ref[...]).astype(BF16)
    o_ref[...] = jnp.dot(h, w_ref[...], preferred_element_type=F32).astype(o_ref.dtype)


def norm_matmul_resident(x, g, w, out_dtype, *, tm, name):
    m, d = x.shape
    gains, lg = g
    w, lw = w
    n = w.shape[2]
    return pl.pallas_call(
        _norm_matmul_resident_kernel,
        grid=(m // tm,),
        in_specs=[pl.BlockSpec((tm, d), lambda i: (i, 0)),
                  _layer_spec(lg, (1, d), lambda i: (0, 0)),
                  _resident_spec(lw, (d, n))],
        out_specs=pl.BlockSpec((tm, n), lambda i: (i, 0)),
        out_shape=jax.ShapeDtypeStruct((m, n), out_dtype),
        compiler_params=_params("parallel"),
        name=name,
    )(x, gains, w)


def _matmul_res_kernel(a_ref, w_ref, x_ref, o_ref):
    a = a_ref[...].astype(BF16)
    o_ref[...] = x_ref[...] + jnp.dot(a, w_ref[...].astype(BF16), preferred_element_type=F32)


def matmul_residual(a, w, x, *, tm, tn, name):
    m, k = a.shape
    w, lw = w
    n = w.shape[2]
    tm, tn = min(tm, m), min(tn, n)
    return pl.pallas_call(
        _matmul_res_kernel,
        grid=(m // tm, n // tn),
        in_specs=[pl.BlockSpec((tm, k), lambda i, j: (i, 0)),
                  _layer_spec(lw, (k, tn), lambda i, j: (0, j)),
                  pl.BlockSpec((tm, tn), lambda i, j: (i, j))],
        out_specs=pl.BlockSpec((tm, tn), lambda i, j: (i, j)),
        out_shape=jax.ShapeDtypeStruct((m, n), F32),
        compiler_params=_params("parallel", "parallel"),
        name=name,
    )(a, w, x)


def _ffn_kernel(x_ref, xs_ref, g_ref, wg_ref, wu_ref, wd_ref, *rest, final_norm):
    if final_norm:
        gf_ref, o_ref, os_ref, h_ref = rest
    else:
        o_ref, os_ref, h_ref = rest
    i, j = pl.program_id(0), pl.program_id(1)
    tm = x_ref.shape[0]
    first_tile = i == 0

    @pl.when(j == 0)
    def _():
        x = x_ref[...]
        h_ref[0:tm, :] = _rms(x, g_ref[...]).astype(BF16)
        o_ref[...] = x

    @pl.when((j == 0) & first_tile)
    def _():
        xs = xs_ref[...]
        h_ref[tm:, :] = _rms(xs, g_ref[...]).astype(BF16)
        os_ref[...] = xs

    h = h_ref[...]
    gate = jnp.dot(h, wg_ref[...].astype(BF16), preferred_element_type=F32)
    up = jnp.dot(h, wu_ref[...].astype(BF16), preferred_element_type=F32)
    act = (gate * jax.nn.sigmoid(gate) * up).astype(BF16)
    wd = wd_ref[...].astype(BF16)
    o_ref[...] += jnp.dot(act[:tm], wd, preferred_element_type=F32)

    @pl.when(first_tile)
    def _():
        os_ref[...] += jnp.dot(act[tm:], wd, preferred_element_type=F32)

    if final_norm:
        last = j == pl.num_programs(1) - 1

        @pl.when(last)
        def _():
            o_ref[...] = _rms(o_ref[...], gf_ref[...])

        @pl.when(last & first_tile)
        def _():
            os_ref[...] = _rms(os_ref[...], gf_ref[...])


def ffn(x, xs, g, w_gate, w_up, w_down, layer, *, tm, tf, name, final_gain=None):
    m, d = x.shape
    rows_s = xs.shape[0]
    assert rows_s == EXTRA_ROWS and m % tm == 0
    gains, lg = g
    f = w_gate.shape[2]
    final_norm = final_gain is not None
    extra_specs = [pl.BlockSpec((1, d), lambda i, j: (0, 0))] if final_norm else []
    extra_args = [final_gain.reshape(1, d)] if final_norm else []
    return pl.pallas_call(
        functools.partial(_ffn_kernel, final_norm=final_norm),
        grid=(m // tm, f // tf),
        in_specs=[pl.BlockSpec((tm, d), lambda i, j: (i, 0)),
                  pl.BlockSpec((rows_s, d), lambda i, j: (0, 0)),
                  _layer_spec(lg, (1, d), lambda i, j: (0, 0)),
                  _layer_spec(layer, (d, tf), lambda i, j: (0, j)),
                  _layer_spec(layer, (d, tf), lambda i, j: (0, j)),
                  _layer_spec(layer, (tf, d), lambda i, j: (j, 0))] + extra_specs,
        out_specs=[pl.BlockSpec((tm, d), lambda i, j: (i, 0)), pl.BlockSpec((rows_s, d), lambda i, j: (0, 0))],
        out_shape=[jax.ShapeDtypeStruct((m, d), F32), jax.ShapeDtypeStruct((rows_s, d), F32)],
        scratch_shapes=[pltpu.VMEM((tm + rows_s, d), BF16)],
        compiler_params=_params("arbitrary", "arbitrary"),
        name=name,
    )(x, xs, gains, w_gate, w_up, w_down, *extra_args)


def _pool_block_kernel(x_ref, g_ref, win_ref, wg_ref, scale_ref, wout_ref, gm_ref, wq_ref, k_ref, v_ref, wo_ref,
                       o_ref, tail_ref, ext_ref, p_ref, *, tm, tiles_per_seq):
    d = x_ref.shape[1]
    gdim = d // len(POOL_WINDOWS)
    t_in_seq = pl.program_id(0) % tiles_per_seq

    @pl.when(t_in_seq == 0)
    def _():
        ext_ref[0:POOL_HALO, :] = jnp.zeros((POOL_HALO, d), F32)

    @pl.when(t_in_seq != 0)
    def _():
        ext_ref[0:POOL_HALO, :] = ext_ref[tm:tm + POOL_HALO, :]

    parts = [slice(c * POOL_PART, (c + 1) * POOL_PART) for c in range(tm // POOL_PART)]
    us = [jnp.dot(_rms(x_ref[rows, :], g_ref[...]).astype(BF16), win_ref[...], preferred_element_type=F32)
          for rows in parts]
    for rows, u in zip(parts, us):
        ext_ref[POOL_HALO + rows.start:POOL_HALO + rows.stop, :] = u
    tail_ref[...] = us[-1][POOL_PART - POOL_HALO:, :]
    ys = []
    for rows in parts:
        for c in range(rows.start // POOL_CHUNK, rows.stop // POOL_CHUNK):
            r0 = POOL_HALO + c * POOL_CHUNK
            pos = t_in_seq * tm + c * POOL_CHUNK + lax.broadcasted_iota(jnp.int32, (POOL_CHUNK, 1), 0)
            for gi, w in enumerate(POOL_WINDOWS):
                cols = slice(gi * gdim, (gi + 1) * gdim)
                u_q = ext_ref[r0:r0 + POOL_CHUNK, cols]
                acc = u_q
                for k in range(1, w):
                    acc = acc + ext_ref[r0 - k:r0 - k + POOL_CHUNK, cols]
                inv_cnt = 1.0 / jnp.minimum(pos + 1, w).astype(F32)
                p_ref[c * POOL_CHUNK:(c + 1) * POOL_CHUNK, cols] = (acc * inv_cnt - u_q).astype(BF16)
        zs = []
        for gi in range(len(POOL_WINDOWS)):
            cols = slice(gi * gdim, (gi + 1) * gdim)
            z = jnp.dot(p_ref[rows, cols], wg_ref[gi], preferred_element_type=F32)
            zs.append((z * scale_ref[:, cols]).astype(BF16))
        ys.append(x_ref[rows, :] + jnp.dot(jnp.concatenate(zs, axis=1), wout_ref[...],
                                            preferred_element_type=F32))
    for rows, y in zip(parts, _mem_rows(ys, gm_ref, wq_ref, k_ref, v_ref, wo_ref)):
        o_ref[rows, :] = y


def pool_mem_block(x, g, w_in, w_group, scale, w_out, g_mem, w_q, mk, mv, w_o, layer, *, seq, tm, name):
    m, d = x.shape
    gains, lg = g
    gains_mem, lgm = g_mem
    n_mem, dm = mk.shape[2], mk.shape[3]
    tiles_per_seq = seq // tm
    return pl.pallas_call(
        functools.partial(_pool_block_kernel, tm=tm, tiles_per_seq=tiles_per_seq),
        grid=(m // tm,),
        in_specs=[pl.BlockSpec((tm, d), lambda i: (i, 0)),
                  _layer_spec(lg, (1, d), lambda i: (0, 0)),
                  _resident_spec(layer, (d, d)),
                  _resident_spec(layer, w_group.shape[1:]),
                  _layer_spec(layer, (1, d), lambda i: (0, 0)),
                  _resident_spec(layer, (d, d))]
                 + _mem_specs(layer, lgm, d, dm, n_mem, lambda i: i // tiles_per_seq),
        out_specs=[pl.BlockSpec((tm, d), lambda i: (i, 0)),
                   pl.BlockSpec((None, POOL_HALO, d), lambda i: (i // tiles_per_seq, 0, 0))],
        out_shape=[jax.ShapeDtypeStruct((m, d), F32), jax.ShapeDtypeStruct((m // seq, POOL_HALO, d), F32)],
        scratch_shapes=[pltpu.VMEM((tm + POOL_HALO, d), F32), pltpu.VMEM((tm, d), BF16)],
        compiler_params=_params("arbitrary"),
        name=name,
    )(x, gains, w_in, w_group, scale, w_out, gains_mem, w_q, mk, mv, w_o)


def _pool_sample_kernel(state_ref, u_ref, wg_ref, scale_ref, z_ref):
    n_state, _, d = state_ref.shape
    gdim = d // len(POOL_WINDOWS)
    u = u_ref[...]
    for gi, w in enumerate(POOL_WINDOWS):
        cols = slice(gi * gdim, (gi + 1) * gdim)
        u_q = u[:, cols]
        acc = u_q
        for k in range(1, w):
            acc = acc + state_ref[n_state - k, :, cols]
        cnt = float(min(n_state + 1, w))
        p = (acc / cnt - u_q).astype(BF16)
        z = jnp.dot(p, wg_ref[gi].astype(BF16), preferred_element_type=F32)
        z_ref[:, cols] = z * scale_ref[:, cols]


def pool_sample(state_t, u, w_group, scale, layer, *, name):
    b, d = u.shape
    return pl.pallas_call(
        _pool_sample_kernel,
        grid=(1,),
        in_specs=[pl.BlockSpec(state_t.shape, lambda i: (0, 0, 0)),
                  pl.BlockSpec((b, d), lambda i: (0, 0)),
                  _layer_spec(layer, w_group.shape[1:], lambda i: (0, 0, 0)),
                  _layer_spec(layer, (1, d), lambda i: (0, 0))],
        out_specs=pl.BlockSpec((b, d), lambda i: (0, 0)),
        out_shape=jax.ShapeDtypeStruct((b, d), F32),
        compiler_params=_params("arbitrary"),
        name=name,
    )(state_t, u, w_group, scale)


def _head_cols(hd):
    return slice(hd * MEM_HEAD_DIM, (hd + 1) * MEM_HEAD_DIM)


def _mem_rows(xs, g_ref, wq_ref, k_ref, v_ref, wo_ref):
    wq = wq_ref[...].astype(BF16)
    kb = k_ref[...].astype(BF16)
    vb = v_ref[...].astype(BF16)
    scale = MEM_HEAD_DIM ** -0.5
    hs = [_rms(x, g_ref[...]).astype(BF16) for x in xs]
    qs = [jnp.dot(h, wq, preferred_element_type=F32).astype(BF16) for h in hs]
    ss = [[lax.dot_general(q[:, _head_cols(hd)], kb[:, _head_cols(hd)], NT_DIMS, preferred_element_type=F32) * scale
           for hd in range(MEM_HEADS)] for q in qs]
    ps = [[_softmax_rows(s)[0].astype(BF16) for s in row] for row in ss]
    os_ = [jnp.concatenate([jnp.dot(p, vb[:, _head_cols(hd)], preferred_element_type=F32).astype(BF16)
                            for hd, p in enumerate(row)], axis=1) for row in ps]
    wo = wo_ref[...].astype(BF16)
    return [x + jnp.dot(o, wo, preferred_element_type=F32) for x, o in zip(xs, os_)]


def _mem_specs(layer, gains_layer, d, dm, n_mem, seq_of_step):
    kv_spec = pl.BlockSpec((None, None, n_mem, dm), lambda i: (layer, seq_of_step(i), 0, 0))
    return [_layer_spec(gains_layer, (1, d), lambda i: (0, 0)), _resident_spec(layer, (d, dm)),
            kv_spec, kv_spec, _resident_spec(layer, (dm, d))]


def _mem_sample_kernel(x_ref, g_ref, wq_ref, k_ref, v_ref, wo_ref, o_ref):
    n_b = x_ref.shape[0]
    n_mem = k_ref.shape[1] // MEM_HEADS
    head_rows = lambda ref, b, hd: ref[b, pl.ds(hd, n_mem, stride=MEM_HEADS), :].astype(BF16)
    x = x_ref[...]
    h = _rms(x, g_ref[...]).astype(BF16)
    q = jnp.dot(h, wq_ref[...].astype(BF16), preferred_element_type=F32).astype(BF16)
    scale = MEM_HEAD_DIM ** -0.5
    row = lax.broadcasted_iota(jnp.int32, (n_b, 1), 0)
    pairs = [(b, hd) for b in range(n_b) for hd in range(MEM_HEADS)]
    ss = [lax.dot_general(q[:, _head_cols(hd)], head_rows(k_ref, b, hd), NT_DIMS,
                          preferred_element_type=F32) * scale for b, hd in pairs]
    ps = [_softmax_rows(s)[0].astype(BF16) for s in ss]
    outs = [jnp.zeros((n_b, MEM_HEAD_DIM), F32) for _ in range(MEM_HEADS)]
    for (b, hd), p in zip(pairs, ps):
        o_b = jnp.dot(p, head_rows(v_ref, b, hd), preferred_element_type=F32)
        outs[hd] = jnp.where(row == b, o_b, outs[hd])
    o = jnp.concatenate(outs, axis=1).astype(BF16)
    o_ref[...] = x + jnp.dot(o, wo_ref[...].astype(BF16), preferred_element_type=F32)


def mem_block_sample(x, g, w_q, mk, mv, w_o, layer, *, name):
    n_b, d = x.shape
    gains, lg = g
    dm = w_q.shape[2]
    kv_spec = _layer_spec(layer, mk.shape[1:], lambda i: (0, 0, 0))
    x_spec = pl.BlockSpec((n_b, d), lambda i: (0, 0))
    return pl.pallas_call(
        _mem_sample_kernel,
        grid=(1,),
        in_specs=[x_spec,
                  _layer_spec(lg, (1, d), lambda i: (0, 0)),
                  _layer_spec(layer, (d, dm), lambda i: (0, 0)),
                  kv_spec, kv_spec,
                  _layer_spec(layer, (dm, d), lambda i: (0, 0))],
        out_specs=x_spec,
        out_shape=jax.ShapeDtypeStruct((n_b, d), F32),
        compiler_params=_params("arbitrary"),
        name=name,
    )(x, gains, w_q, mk, mv, w_o)


def _strided_rows(ref, start, dil):
    if dil == 1:
        return ref[pl.ds(start, DIL_BLOCK), :]
    return ref[pl.ds(start, DIL_BLOCK, stride=dil), :]


def _dil_group_attention(q_refs, k_ref, v_ref, o_ref, lse_ref, *, dil, seq):
    span = dil * DIL_BLOCK
    nb = seq // span
    qi = lax.broadcasted_iota(jnp.int32, (DIL_BLOCK, 2 * DIL_BLOCK), 0)
    kj = lax.broadcasted_iota(jnp.int32, (DIL_BLOCK, 2 * DIL_BLOCK), 1)
    dist = qi + DIL_BLOCK - kj
    in_window = (dist >= 0) & (dist <= DIL_BLOCK)
    causal = (lax.broadcasted_iota(jnp.int32, (DIL_BLOCK, DIL_BLOCK), 0)
              >= lax.broadcasted_iota(jnp.int32, (DIL_BLOCK, DIL_BLOCK), 1))
    scale = HEAD_DIM ** -0.5

    def store_rows(ref, lead, start, val):
        rows = pl.ds(start, DIL_BLOCK) if dil == 1 else pl.ds(start, DIL_BLOCK, stride=dil)
        ref[lead + (rows, slice(None))] = val

    masks = {False: jnp.concatenate([in_window] * KV_REP, axis=0),
             True: jnp.concatenate([in_window & (kj >= DIL_BLOCK)] * KV_REP, axis=0),
             None: jnp.concatenate([causal] * KV_REP, axis=0)}

    def scores(jg, i):
        if nb == 1:
            b, start, first = 0, jg * DIL_GROUP + i, None
        elif nb <= DIL_GROUP:
            r, b = jg * (DIL_GROUP // nb) + i // nb, i % nb
            start, first = b * span + r, b == 0
        else:
            r, b = jg // (nb // DIL_GROUP), (jg % (nb // DIL_GROUP)) * DIL_GROUP + i
            start, first = b * span + r, (b == 0 if i == 0 else False)
        if dil == 1:
            start = pl.multiple_of(start, DIL_BLOCK)
        q2 = jnp.concatenate([_strided_rows(q, start, dil) for q in q_refs], axis=0).astype(BF16)
        k_cur = _strided_rows(k_ref, start, dil)
        v_cur = _strided_rows(v_ref, start, dil)
        if nb > 1:
            prev = jnp.maximum(b - 1, 0) * span + (start - b * span)
            if dil == 1:
                prev = pl.multiple_of(prev, DIL_BLOCK)
            kcat = jnp.concatenate([_strided_rows(k_ref, prev, dil), k_cur], axis=0).astype(BF16)
            vcat = jnp.concatenate([_strided_rows(v_ref, prev, dil), v_cur], axis=0).astype(BF16)
        else:
            kcat, vcat = k_cur.astype(BF16), v_cur.astype(BF16)
        s = lax.dot_general(q2, kcat, NT_DIMS, preferred_element_type=F32) * scale
        if isinstance(first, (bool, type(None))):
            valid = masks[first]
        else:
            valid = masks[False] & (jnp.concatenate([kj] * KV_REP, axis=0) >= jnp.where(first, DIL_BLOCK, 0))
        return start, jnp.where(valid, s, NEG), vcat

    def block_group(jg, carry):
        staged = [scores(jg, i) for i in range(DIL_GROUP)]
        probs = [_softmax_rows(s) for _, s, _ in staged]
        outs = [jnp.dot(p.astype(BF16), vcat, preferred_element_type=F32)
                for (p, _, _), (_, _, vcat) in zip(probs, staged)]
        for (start, _, _), (_, m, den), o2 in zip(staged, probs, outs):
            lse2 = m + jnp.log(den)
            for hh in range(KV_REP):
                rows = slice(hh * DIL_BLOCK, (hh + 1) * DIL_BLOCK)
                store_rows(o_ref, (hh,), start, o2[rows])
                store_rows(lse_ref, (hh,), start, jnp.broadcast_to(lse2[rows], (DIL_BLOCK, HEAD_DIM)))
        return carry

    lax.fori_loop(0, seq // (DIL_BLOCK * DIL_GROUP), block_group, 0)


def _dil_prompt_kernel(*refs, seq):
    n_g = len(DIL_PATTERNS)
    q_refs = refs[:n_g * KV_REP]
    k_ref, v_ref, out_ref, o_scr, lse_scr = refs[n_g * KV_REP:]
    for gi, (_, dil) in enumerate(DIL_PATTERNS):
        _dil_group_attention(q_refs[gi * KV_REP:(gi + 1) * KV_REP], k_ref, v_ref, o_scr.at[gi], lse_scr.at[gi],
                             dil=dil, seq=seq)

    def merge(c, carry):
        rows = pl.ds(pl.multiple_of(c * MERGE_ROWS, MERGE_ROWS), MERGE_ROWS)
        for hh in range(KV_REP):
            lses = [lse_scr[gi, hh, rows, :] for gi in range(n_g)]
            m = functools.reduce(jnp.maximum, lses)
            es = [jnp.exp(l - m) for l in lses]
            den = functools.reduce(jnp.add, es)
            acc = sum((e / den) * o_scr[gi, hh, rows, :] for gi, e in enumerate(es))
            out_ref[hh, rows, :] = acc.astype(out_ref.dtype)
        return carry

    lax.fori_loop(0, seq // MERGE_ROWS, merge, 0)


def dil_attention_prompt(q, k, v, *, name):
    n, s, _ = q.shape
    n_g = len(DIL_PATTERNS)
    q_map = lambda head: (lambda b, j: (b, 0, head + KV_REP * j))
    q_specs = [pl.BlockSpec((None, s, HEAD_DIM), q_map(gi * HEADS_PER_GROUP + hh))
               for gi in range(n_g) for hh in range(KV_REP)]
    head_block = pl.BlockSpec((None, s, HEAD_DIM), lambda b, j: (b, 0, j))
    return pl.pallas_call(
        functools.partial(_dil_prompt_kernel, seq=s),
        grid=(n, N_KV_HEADS),
        in_specs=q_specs + [head_block, head_block],
        out_specs=pl.BlockSpec((None, KV_REP, s, HEAD_DIM), lambda b, j: (b, j, 0, 0)),
        out_shape=jax.ShapeDtypeStruct((n, HEADS_PER_GROUP, s, HEAD_DIM), BF16),
        scratch_shapes=[pltpu.VMEM((n_g, KV_REP, s, HEAD_DIM), F32), pltpu.VMEM((n_g, KV_REP, s, HEAD_DIM), F32)],
        compiler_params=_params("parallel", "parallel"),
        name=name,
    )(*([q] * (n_g * KV_REP)), k, v)


def _heads_out_mem_kernel(o_ref, w_ref, x_ref, g_ref, wq_ref, k_ref, v_ref, wo_ref, out_ref):
    tm = x_ref.shape[0]
    parts = [slice(c * MEM_ROWS, (c + 1) * MEM_ROWS) for c in range(tm // MEM_ROWS)]
    w = w_ref[...]
    xs = [x_ref[rows, :] + jnp.dot(jnp.concatenate([o_ref[h, rows, :] for h in range(o_ref.shape[0])], axis=1), w,
                                   preferred_element_type=F32) for rows in parts]
    for rows, y in zip(parts, _mem_rows(xs, g_ref, wq_ref, k_ref, v_ref, wo_ref)):
        out_ref[rows, :] = y


def heads_out_mem(o, w, x, g, w_q, mk, mv, w_o, layer, *, seq, tm, name):
    m, d = x.shape
    w, lw = w
    gains, lg = g
    n_mem, dm = mk.shape[2], mk.shape[3]
    tps = seq // tm
    return pl.pallas_call(
        _heads_out_mem_kernel,
        grid=(m // tm,),
        in_specs=[pl.BlockSpec((None,) + o.shape[1:2] + (tm, o.shape[3]), lambda i: (i // tps, 0, i % tps, 0)),
                  _resident_spec(lw, w.shape[1:]),
                  pl.BlockSpec((tm, d), lambda i: (i, 0))]
                 + _mem_specs(layer, lg, d, dm, n_mem, lambda i: i // tps),
        out_specs=pl.BlockSpec((tm, d), lambda i: (i, 0)),
        out_shape=jax.ShapeDtypeStruct((m, d), F32),
        compiler_params=_params("parallel"),
        name=name,
    )(o, w, x, gains, w_q, mk, mv, w_o)


def _dil_sample_kernel(q_ref, kn_ref, vn_ref, *refs):
    n_g = len(DIL_PATTERNS)
    k_refs, v_refs, o_ref = refs[:n_g], refs[n_g:2 * n_g], refs[2 * n_g]

    def head_rows(ref, kv):
        if len(ref.shape) == 2:
            return _strided_rows(ref, kv, N_KV_HEADS).astype(BF16)
        return ref[:, kv, :].astype(BF16)

    scale = HEAD_DIM ** -0.5
    head_row = lax.broadcasted_iota(jnp.int32, (HEADS_PER_GROUP, 1), 0)
    kn = kn_ref[...].astype(BF16).astype(F32)
    vn = vn_ref[...].astype(BF16).astype(F32)
    outs, lses = [], []
    for gi, (window, dil) in enumerate(DIL_PATTERNS):
        qg = q_ref[gi * HEADS_PER_GROUP:(gi + 1) * HEADS_PER_GROUP, :].astype(BF16)
        qf = qg.astype(F32)
        s = jnp.zeros((HEADS_PER_GROUP, DIL_BLOCK), F32)
        s_new = jnp.zeros((HEADS_PER_GROUP, 1), F32)
        for kv in range(N_KV_HEADS):
            mine = (head_row >= kv * KV_REP) & (head_row < (kv + 1) * KV_REP)
            kc = head_rows(k_refs[gi], kv)
            s = jnp.where(mine, lax.dot_general(qg, kc, NT_DIMS, preferred_element_type=F32), s)
            s_new = jnp.where(mine, jnp.sum(qf * kn[kv:kv + 1, :], axis=-1, keepdims=True), s_new)
        s = s * scale
        s_new = s_new * scale
        m = jnp.maximum(jnp.max(s, axis=-1, keepdims=True), s_new)
        e = jnp.exp(s - m)
        e_new = jnp.exp(s_new - m)
        den = jnp.sum(e, axis=-1, keepdims=True) + e_new
        p = (e / den).astype(BF16)
        p_new = (e_new / den).astype(BF16).astype(F32)
        o = jnp.zeros((HEADS_PER_GROUP, HEAD_DIM), F32)
        for kv in range(N_KV_HEADS):
            mine = (head_row >= kv * KV_REP) & (head_row < (kv + 1) * KV_REP)
            vc = head_rows(v_refs[gi], kv)
            o_kv = jnp.dot(p, vc, preferred_element_type=F32) + p_new * vn[kv:kv + 1, :]
            o = jnp.where(mine, o_kv, o)
        outs.append(o)
        lses.append(m + jnp.log(den))
    mm = jnp.maximum(jnp.maximum(lses[0], lses[1]), lses[2])
    es = [jnp.exp(l - mm) for l in lses]
    den = es[0] + es[1] + es[2]
    o_ref[...] = (es[0] / den) * outs[0] + (es[1] / den) * outs[1] + (es[2] / den) * outs[2]


def _cache_rows_spec(b, wbuf, window, dil):
    first = wbuf - window
    assert window // dil == DIL_BLOCK and window <= wbuf and wbuf % dil == 0 and first % (dil * DIL_BLOCK) == 0
    rows_per_token = N_KV_HEADS
    if dil * rows_per_token < 8:
        blk = first // DIL_BLOCK
        return ((b, wbuf * rows_per_token, HEAD_DIM),
                pl.BlockSpec((None, DIL_BLOCK * rows_per_token, HEAD_DIM), lambda i: (i, blk, 0)))
    assert (dil * rows_per_token) % 8 == 0
    blk = first // dil // DIL_BLOCK
    return ((b, wbuf // dil, dil * rows_per_token // 8, 8, HEAD_DIM),
            pl.BlockSpec((None, DIL_BLOCK, None, 8, HEAD_DIM), lambda i: (i, blk, 0, 0, 0)))


def dil_attention_sample(q, k_new, v_new, cache_k, cache_v, *, name):
    b, qd = q.shape
    wbuf = cache_k.shape[1]
    n_heads = qd // HEAD_DIM
    small = lambda rows: pl.BlockSpec((None, rows, HEAD_DIM), lambda i: (i, 0, 0))
    views_k, views_v, specs = [], [], []
    for window, dil in DIL_PATTERNS:
        view, spec = _cache_rows_spec(b, wbuf, window, dil)
        views_k.append(cache_k.reshape(view))
        views_v.append(cache_v.reshape(view))
        specs.append(spec)
    o = pl.pallas_call(
        _dil_sample_kernel,
        grid=(b,),
        in_specs=[small(n_heads), small(N_KV_HEADS), small(N_KV_HEADS)] + specs + specs,
        out_specs=small(HEADS_PER_GROUP),
        out_shape=jax.ShapeDtypeStruct((b, HEADS_PER_GROUP, HEAD_DIM), F32),
        compiler_params=_params("parallel"),
        name=name,
    )(q.reshape(b, n_heads, HEAD_DIM), k_new.reshape(b, N_KV_HEADS, HEAD_DIM),
      v_new.reshape(b, N_KV_HEADS, HEAD_DIM), *views_k, *views_v)
    return o.reshape(b, HEADS_PER_GROUP * HEAD_DIM)


def _trunks(xp, xs, mem_p, mem_s, w, *, batch, seq, pool_state, win_k, win_v):
    depth = w["ffn_w_gate"].shape[0]
    n_a = w["pool_w_in"].shape[0]
    d = xp.shape[1]
    dec_batch = xs.shape[0]
    pool_p, pool_s = [], []
    kp = vp = kp_rows = vp_rows = ks = vs = None
    for l in range(depth):
        if l == n_a:
            kv_w = [(w["w_k_shared"], 0), (w["w_v_shared"], 0)]
            kp, vp, kp_rows, vp_rows = (a[0] for a in shared_kv(xp, w["norm_kv"], w["w_k_shared"], w["w_v_shared"],
                                                                 tm=512, name="p_kv"))
            ks, vs = norm_matmul(xs, (w["norm_kv"], 0), kv_w, [F32, F32], tm=1024, tn=512, name="s_kv")
        mem_args = ((w["norm_mem_q"], l), w["mem_w_q"], *mem_p, w["mem_w_o"], l)
        if l < n_a:
            xp, tail = pool_mem_block(xp, (w["norm_mix"], l), w["pool_w_in"], w["pool_w_group"], w["pool_scale"],
                                      w["pool_w_out"], *mem_args, seq=seq, tm=POOL_TM, name=f"p_pool_mem{l}")
            pool_p.append(tail[:, POOL_HALO - POOL_STATE:])
            (u,) = norm_matmul(xs, (w["norm_mix"], l), [(w["pool_w_in"], l)], [F32], tm=1024, tn=512,
                               name=f"s_pool_in{l}")
            state = pool_state[l]
            pool_s.append(jnp.concatenate([state[:, 1:], u[:, None, :]], axis=1))
            z = pool_sample(jnp.swapaxes(state, 0, 1), u, w["pool_w_group"], w["pool_scale"], l, name=f"s_pool{l}")
            xs = matmul_residual(z, (w["pool_w_out"], l), xs, tm=1024, tn=1024, name=f"s_pool_out{l}")
        else:
            j = l - n_a
            q = norm_matmul_resident(xp, (w["norm_mix"], l), (w["dil_w_q"], j), F32, tm=512, name=f"p_dil_q{j}")
            q3, k3, v3 = (a.reshape(batch, seq, -1) for a in (q, kp, vp))
            o = dil_attention_prompt(q3, k3, v3, name=f"p_dil_attn{j}")
            xp = heads_out_mem(o, (w["dil_w_o"], j), xp, *mem_args, seq=seq, tm=512, name=f"p_dil_out_mem{j}")
            (q,) = norm_matmul(xs, (w["norm_mix"], l), [(w["dil_w_q"], j)], [F32], tm=1024, tn=512,
                               name=f"s_dil_q{j}")
            o = dil_attention_sample(q, ks, vs, win_k, win_v, name=f"s_dil_attn{j}")
            xs = matmul_residual(o, (w["dil_w_o"], j), xs, tm=1024, tn=1024, name=f"s_dil_out{j}")
        xs = mem_block_sample(xs, (w["norm_mem_q"], l), w["mem_w_q"], *mem_s, w["mem_w_o"], l, name=f"s_mem{l}")
        xs_pad = jnp.pad(xs, ((0, EXTRA_ROWS - dec_batch), (0, 0)))
        xp, xs_pad = ffn(xp, xs_pad, (w["norm_ffn"], l), w["ffn_w_gate"], w["ffn_w_up"], w["ffn_w_down"], l,
                         tm=1024, tf=256, name=f"ffn{l}", final_gain=w["norm_final"] if l == depth - 1 else None)
        xs = xs_pad[:dec_batch]
    return (xp, jnp.stack(pool_p), kp_rows, vp_rows), (xs, jnp.stack(pool_s), ks, vs)


def kernel(x_prompt, x_sample, mem_prompt, state_pool, cache_win_k, cache_win_v, cache_mem_k, cache_mem_v,
           norm_mix, norm_mem_q, norm_mem_kv, norm_ffn, pool_w_in, pool_w_group, pool_scale, pool_w_out,
           norm_kv, w_k_shared, w_v_shared, dil_w_q, dil_w_o, mem_w_q, mem_w_k, mem_w_v, mem_w_o,
           ffn_w_gate, ffn_w_up, ffn_w_down, norm_final):
    batch, seq, d = x_prompt.shape
    dec_batch, dec_seq, _ = x_sample.shape
    depth = norm_mix.shape[0]
    n_mem = mem_prompt.shape[1]
    rows3 = lambda a: a.reshape(a.shape[0], 1, a.shape[1])
    w = dict(norm_mix=rows3(norm_mix), norm_mem_q=rows3(norm_mem_q), norm_ffn=rows3(norm_ffn),
             norm_kv=norm_kv.reshape(1, 1, d), norm_final=norm_final, pool_scale=rows3(pool_scale),
             pool_w_in=pool_w_in.astype(BF16), pool_w_group=pool_w_group.astype(BF16),
             pool_w_out=pool_w_out.astype(BF16), dil_w_q=dil_w_q.astype(BF16), dil_w_o=dil_w_o.astype(BF16),
             mem_w_q=mem_w_q.astype(BF16), mem_w_o=mem_w_o.astype(BF16),
             w_k_shared=w_k_shared[None], w_v_shared=w_v_shared[None],
             ffn_w_gate=ffn_w_gate, ffn_w_up=ffn_w_up, ffn_w_down=ffn_w_down)

    mk_p, mv_p, mk_rows, mv_rows = shared_kv(mem_prompt.reshape(batch * n_mem, d), rows3(norm_mem_kv),
                                             mem_w_k, mem_w_v, tm=512, name="mem_kv")
    mk_p = mk_p.reshape(depth, batch, n_mem, -1)
    mv_p = mv_p.reshape(depth, batch, n_mem, -1)

    assert dec_seq == 1
    mk_s = cache_mem_k.reshape(depth, dec_batch, n_mem * MEM_HEADS, MEM_HEAD_DIM)
    mv_s = cache_mem_v.reshape(depth, dec_batch, n_mem * MEM_HEADS, MEM_HEAD_DIM)
    (y_p, pool_p, k_p, v_p), (y_s, pool_s, k_s, v_s) = _trunks(
        x_prompt.reshape(batch * seq, d), x_sample.reshape(dec_batch, d), (mk_p, mv_p), (mk_s, mv_s), w,
        batch=batch, seq=seq, pool_state=state_pool, win_k=cache_win_k, win_v=cache_win_v)

    max_window = max(wd for wd, _ in DIL_PATTERNS)
    keep_from = max(0, seq - max_window)
    kv_shape = (N_KV_HEADS, HEAD_DIM)
    mem_shape = (depth, batch, n_mem, MEM_HEADS, MEM_HEAD_DIM)
    return (y_p.reshape(batch, seq, d), y_s.reshape(dec_batch, dec_seq, d), pool_p, pool_s,
            k_p.reshape(batch, seq, *kv_shape)[:, keep_from:], v_p.reshape(batch, seq, *kv_shape)[:, keep_from:],
            k_s.reshape(dec_batch, dec_seq, *kv_shape), v_s.reshape(dec_batch, dec_seq, *kv_shape),
            mk_rows.reshape(mem_shape), mv_rows.reshape(mem_shape))
```

```python
import functools

import jax
import jax.numpy as jnp
from jax import lax
from jax.experimental import pallas as pl
from jax.experimental.pallas import tpu as pltpu

F32 = jnp.float32
BF16 = jnp.bfloat16

RMS_EPS = 1e-6
POOL_WINDOWS = (2, 4, 8, 16)
POOL_STATE = max(POOL_WINDOWS) - 1
POOL_HALO = 16
POOL_CHUNK = 64
POOL_TM = 512
POOL_PART = 256
EXTRA_ROWS = 8
DIL_PATTERNS = ((128, 1), (512, 4), (2048, 16))
DIL_BLOCK = 128
DIL_GROUP = 8
MERGE_ROWS = 256
MEM_ROWS = 256
HEAD_DIM = 128
N_KV_HEADS = 4
HEADS_PER_GROUP = 8
KV_REP = HEADS_PER_GROUP // N_KV_HEADS
MEM_HEADS = 4
MEM_HEAD_DIM = 128
NEG = -1e30
VMEM_LIMIT = 56 * 1024 * 1024

NT_DIMS = (((1,), (1,)), ((), ()))


def _params(*sem):
    return pltpu.CompilerParams(dimension_semantics=sem, vmem_limit_bytes=VMEM_LIMIT)


def _layer_spec(layer, block, index_map):
    return pl.BlockSpec((None,) + tuple(block), lambda *g: (layer,) + tuple(index_map(*g)))


def _resident_spec(layer, block):
    zeros = (0,) * len(block)
    return pl.BlockSpec((None,) + tuple(block), lambda *g: (layer,) + zeros, pipeline_mode=pl.Buffered(1))


def _rms(x, g):
    ms = jnp.mean(x * x, axis=-1, keepdims=True)
    return x * lax.rsqrt(ms + RMS_EPS) * g


def _lhs_scratch_dtype(rows):
    return BF16 if rows % 16 == 0 else F32


def _softmax_rows(s):
    m = jnp.max(s, axis=-1, keepdims=True)
    e = jnp.exp(s - m)
    den = jnp.sum(e, axis=-1, keepdims=True)
    return e / den, m, den


def _norm_matmul_kernel(x_ref, g_ref, *refs, n_w):
    w_refs, o_refs, h_ref = refs[:n_w], refs[n_w:2 * n_w], refs[2 * n_w]

    @pl.when(pl.program_id(1) == 0)
    def _():
        h_ref[...] = _rms(x_ref[...], g_ref[...]).astype(h_ref.dtype)

    h = h_ref[...].astype(BF16)
    for w_ref, o_ref in zip(w_refs, o_refs):
        o_ref[...] = jnp.dot(h, w_ref[...].astype(BF16), preferred_element_type=F32).astype(o_ref.dtype)


def norm_matmul(x, g, ws, out_dtypes, *, tm, tn, name):
    m, d = x.shape
    gains, lg = g
    n = ws[0][0].shape[2]
    tm, tn = min(tm, m), min(tn, n)
    n_w = len(ws)
    return pl.pallas_call(
        functools.partial(_norm_matmul_kernel, n_w=n_w),
        grid=(m // tm, n // tn),
        in_specs=[pl.BlockSpec((tm, d), lambda i, j: (i, 0)),
                  _layer_spec(lg, (1, d), lambda i, j: (0, 0))]
                 + [_layer_spec(lw, (d, tn), lambda i, j: (0, j)) for _, lw in ws],
        out_specs=[pl.BlockSpec((tm, tn), lambda i, j: (i, j))] * n_w,
        out_shape=[jax.ShapeDtypeStruct((m, n), dt) for dt in out_dtypes],
        scratch_shapes=[pltpu.VMEM((tm, d), _lhs_scratch_dtype(tm))],
        compiler_params=_params("parallel", "arbitrary"),
        name=name,
    )(x, gains, *[w for w, _ in ws])


def _shared_kv_kernel(x_ref, g_ref, wk_ref, wv_ref, k_ref, v_ref, kh_ref, vh_ref):
    tm = x_ref.shape[0]
    h = _rms(x_ref[...], g_ref[...]).astype(BF16)
    for w_ref, o_ref, oh_ref in ((wk_ref, k_ref, kh_ref), (wv_ref, v_ref, vh_ref)):
        r = jnp.dot(h, w_ref[...].astype(BF16), preferred_element_type=F32)
        o_ref[...] = r
        for hd in range(N_KV_HEADS):
            oh_ref[pl.ds(hd, tm, stride=N_KV_HEADS), :] = r[:, hd * HEAD_DIM:(hd + 1) * HEAD_DIM]


def shared_kv(x, gains, w_k, w_v, *, tm, name):
    m, d = x.shape
    n_layers, _, n = w_k.shape
    layer_rows = lambda l, i: (l, i, 0)
    per_layer = lambda l, i: (l, 0, 0)
    return pl.pallas_call(
        _shared_kv_kernel,
        grid=(n_layers, m // tm),
        in_specs=[pl.BlockSpec((tm, d), lambda l, i: (i, 0)), pl.BlockSpec((None, 1, d), per_layer),
                  pl.BlockSpec((None, d, n), per_layer), pl.BlockSpec((None, d, n), per_layer)],
        out_specs=[pl.BlockSpec((None, tm, n), layer_rows)] * 2
                  + [pl.BlockSpec((None, tm * N_KV_HEADS, HEAD_DIM), layer_rows)] * 2,
        out_shape=[jax.ShapeDtypeStruct((n_layers, m, n), F32)] * 2
                  + [jax.ShapeDtypeStruct((n_layers, m * N_KV_HEADS, HEAD_DIM), F32)] * 2,
        compiler_params=_params("parallel", "parallel"),
        name=name,
    )(x, gains, w_k, w_v)


def _norm_matmul_resident_kernel(x_ref, g_ref, w_ref, o_ref):
    h = _rms(x_ref[...], g_ref[...]).astype(BF16)
    o_ref[...] = jnp.dot(h, w_ref[...], preferred_element_type=F32).astype(o_ref.dtype)


def norm_matmul_resident(x, g, w, out_dtype, *, tm, name):
    m, d = x.shape
    gains, lg = g
    w, lw = w
    n = w.shape[2]
    return pl.pallas_call(
        _norm_matmul_resident_kernel,
        grid=(m // tm,),
        in_specs=[pl.BlockSpec((tm, d), lambda i: (i, 0)),
                  _layer_spec(lg, (1, d), lambda i: (0, 0)),
                  _resident_spec(lw, (d, n))],
        out_specs=pl.BlockSpec((tm, n), lambda i: (i, 0)),
        out_shape=jax.ShapeDtypeStruct((m, n), out_dtype),
        compiler_params=_params("parallel"),
        name=name,
    )(x, gains, w)


def _matmul_res_kernel(a_ref, w_ref, x_ref, o_ref):
    a = a_ref[...].astype(BF16)
    o_ref[...] = x_ref[...] + jnp.dot(a, w_ref[...].astype(BF16), preferred_element_type=F32)


def matmul_residual(a, w, x, *, tm, tn, name):
    m, k = a.shape
    w, lw = w
    n = w.shape[2]
    tm, tn = min(tm, m), min(tn, n)
    return pl.pallas_call(
        _matmul_res_kernel,
        grid=(m // tm, n // tn),
        in_specs=[pl.BlockSpec((tm, k), lambda i, j: (i, 0)),
                  _layer_spec(lw, (k, tn), lambda i, j: (0, j)),
                  pl.BlockSpec((tm, tn), lambda i, j: (i, j))],
        out_specs=pl.BlockSpec((tm, tn), lambda i, j: (i, j)),
        out_shape=jax.ShapeDtypeStruct((m, n), F32),
        compiler_params=_params("parallel", "parallel"),
        name=name,
    )(a, w, x)


def _ffn_kernel(x_ref, xs_ref, g_ref, wg_ref, wu_ref, wd_ref, *rest, final_norm):
    if final_norm:
        gf_ref, o_ref, os_ref, h_ref = rest
    else:
        o_ref, os_ref, h_ref = rest
    i, j = pl.program_id(0), pl.program_id(1)
    tm = x_ref.shape[0]
    first_tile = i == 0

    @pl.when(j == 0)
    def _():
        x = x_ref[...]
        h_ref[0:tm, :] = _rms(x, g_ref[...]).astype(BF16)
        o_ref[...] = x

    @pl.when((j == 0) & first_tile)
    def _():
        xs = xs_ref[...]
        h_ref[tm:, :] = _rms(xs, g_ref[...]).astype(BF16)
        os_ref[...] = xs

    h = h_ref[...]
    gate = jnp.dot(h, wg_ref[...].astype(BF16), preferred_element_type=F32)
    up = jnp.dot(h, wu_ref[...].astype(BF16), preferred_element_type=F32)
    act = (gate * jax.nn.sigmoid(gate) * up).astype(BF16)
    wd = wd_ref[...].astype(BF16)
    o_ref[...] += jnp.dot(act[:tm], wd, preferred_element_type=F32)

    @pl.when(first_tile)
    def _():
        os_ref[...] += jnp.dot(act[tm:], wd, preferred_element_type=F32)

    if final_norm:
        last = j == pl.num_programs(1) - 1

        @pl.when(last)
        def _():
            o_ref[...] = _rms(o_ref[...], gf_ref[...])

        @pl.when(last & first_tile)
        def _():
            os_ref[...] = _rms(os_ref[...], gf_ref[...])


def ffn(x, xs, g, w_gate, w_up, w_down, layer, *, tm, tf, name, final_gain=None):
    m, d = x.shape
    rows_s = xs.shape[0]
    assert rows_s == EXTRA_ROWS and m % tm == 0
    gains, lg = g
    f = w_gate.shape[2]
    final_norm = final_gain is not None
    extra_specs = [pl.BlockSpec((1, d), lambda i, j: (0, 0))] if final_norm else []
    extra_args = [final_gain.reshape(1, d)] if final_norm else []
    return pl.pallas_call(
        functools.partial(_ffn_kernel, final_norm=final_norm),
        grid=(m // tm, f // tf),
        in_specs=[pl.BlockSpec((tm, d), lambda i, j: (i, 0)),
                  pl.BlockSpec((rows_s, d), lambda i, j: (0, 0)),
                  _layer_spec(lg, (1, d), lambda i, j: (0, 0)),
                  _layer_spec(layer, (d, tf), lambda i, j: (0, j)),
                  _layer_spec(layer, (d, tf), lambda i, j: (0, j)),
                  _layer_spec(layer, (tf, d), lambda i, j: (j, 0))] + extra_specs,
        out_specs=[pl.BlockSpec((tm, d), lambda i, j: (i, 0)), pl.BlockSpec((rows_s, d), lambda i, j: (0, 0))],
        out_shape=[jax.ShapeDtypeStruct((m, d), F32), jax.ShapeDtypeStruct((rows_s, d), F32)],
        scratch_shapes=[pltpu.VMEM((tm + rows_s, d), BF16)],
        compiler_params=_params("arbitrary", "arbitrary"),
        name=name,
    )(x, xs, gains, w_gate, w_up, w_down, *extra_args)


def _pool_block_kernel(x_ref, g_ref, win_ref, wg_ref, scale_ref, wout_ref, gm_ref, wq_ref, k_ref, v_ref, wo_ref,
                       o_ref, tail_ref, ext_ref, p_ref, *, tm, tiles_per_seq):
    d = x_ref.shape[1]
    gdim = d // len(POOL_WINDOWS)
    t_in_seq = pl.program_id(0) % tiles_per_seq

    @pl.when(t_in_seq == 0)
    def _():
        ext_ref[0:POOL_HALO, :] = jnp.zeros((POOL_HALO, d), F32)

    @pl.when(t_in_seq != 0)
    def _():
        ext_ref[0:POOL_HALO, :] = ext_ref[tm:tm + POOL_HALO, :]

    parts = [slice(c * POOL_PART, (c + 1) * POOL_PART) for c in range(tm // POOL_PART)]
    us = [jnp.dot(_rms(x_ref[rows, :], g_ref[...]).astype(BF16), win_ref[...], preferred_element_type=F32)
          for rows in parts]
    for rows, u in zip(parts, us):
        ext_ref[POOL_HALO + rows.start:POOL_HALO + rows.stop, :] = u
    tail_ref[...] = us[-1][POOL_PART - POOL_HALO:, :]
    ys = []
    for rows in parts:
        for c in range(rows.start // POOL_CHUNK, rows.stop // POOL_CHUNK):
            r0 = POOL_HALO + c * POOL_CHUNK
            pos = t_in_seq * tm + c * POOL_CHUNK + lax.broadcasted_iota(jnp.int32, (POOL_CHUNK, 1), 0)
            for gi, w in enumerate(POOL_WINDOWS):
                cols = slice(gi * gdim, (gi + 1) * gdim)
                u_q = ext_ref[r0:r0 + POOL_CHUNK, cols]
                acc = u_q
                for k in range(1, w):
                    acc = acc + ext_ref[r0 - k:r0 - k + POOL_CHUNK, cols]
                inv_cnt = 1.0 / jnp.minimum(pos + 1, w).astype(F32)
                p_ref[c * POOL_CHUNK:(c + 1) * POOL_CHUNK, cols] = (acc * inv_cnt - u_q).astype(BF16)
        zs = []
        for gi in range(len(POOL_WINDOWS)):
            cols = slice(gi * gdim, (gi + 1) * gdim)
            z = jnp.dot(p_ref[rows, cols], wg_ref[gi], preferred_element_type=F32)
            zs.append((z * scale_ref[:, cols]).astype(BF16))
        ys.append(x_ref[rows, :] + jnp.dot(jnp.concatenate(zs, axis=1), wout_ref[...],
                                            preferred_element_type=F32))
    for rows, y in zip(parts, _mem_rows(ys, gm_ref, wq_ref, k_ref, v_ref, wo_ref)):
        o_ref[rows, :] = y


def pool_mem_block(x, g, w_in, w_group, scale, w_out, g_mem, w_q, mk, mv, w_o, layer, *, seq, tm, name):
    m, d = x.shape
    gains, lg = g
    gains_mem, lgm = g_mem
    n_mem, dm = mk.shape[2], mk.shape[3]
    tiles_per_seq = seq // tm
    return pl.pallas_call(
        functools.partial(_pool_block_kernel, tm=tm, tiles_per_seq=tiles_per_seq),
        grid=(m // tm,),
        in_specs=[pl.BlockSpec((tm, d), lambda i: (i, 0)),
                  _layer_spec(lg, (1, d), lambda i: (0, 0)),
                  _resident_spec(layer, (d, d)),
                  _resident_spec(layer, w_group.shape[1:]),
                  _layer_spec(layer, (1, d), lambda i: (0, 0)),
                  _resident_spec(layer, (d, d))]
                 + _mem_specs(layer, lgm, d, dm, n_mem, lambda i: i // tiles_per_seq),
        out_specs=[pl.BlockSpec((tm, d), lambda i: (i, 0)),
                   pl.BlockSpec((None, POOL_HALO, d), lambda i: (i // tiles_per_seq, 0, 0))],
        out_shape=[jax.ShapeDtypeStruct((m, d), F32), jax.ShapeDtypeStruct((m // seq, POOL_HALO, d), F32)],
        scratch_shapes=[pltpu.VMEM((tm + POOL_HALO, d), F32), pltpu.VMEM((tm, d), BF16)],
        compiler_params=_params("arbitrary"),
        name=name,
    )(x, gains, w_in, w_group, scale, w_out, gains_mem, w_q, mk, mv, w_o)


def _pool_sample_kernel(state_ref, u_ref, wg_ref, scale_ref, z_ref):
    n_state, _, d = state_ref.shape
    gdim = d // len(POOL_WINDOWS)
    u = u_ref[...]
    for gi, w in enumerate(POOL_WINDOWS):
        cols = slice(gi * gdim, (gi + 1) * gdim)
        u_q = u[:, cols]
        acc = u_q
        for k in range(1, w):
            acc = acc + state_ref[n_state - k, :, cols]
        cnt = float(min(n_state + 1, w))
        p = (acc / cnt - u_q).astype(BF16)
        z = jnp.dot(p, wg_ref[gi].astype(BF16), preferred_element_type=F32)
        z_ref[:, cols] = z * scale_ref[:, cols]


def pool_sample(state_t, u, w_group, scale, layer, *, name):
    b, d = u.shape
    return pl.pallas_call(
        _pool_sample_kernel,
        grid=(1,),
        in_specs=[pl.BlockSpec(state_t.shape, lambda i: (0, 0, 0)),
                  pl.BlockSpec((b, d), lambda i: (0, 0)),
                  _layer_spec(layer, w_group.shape[1:], lambda i: (0, 0, 0)),
                  _layer_spec(layer, (1, d), lambda i: (0, 0))],
        out_specs=pl.BlockSpec((b, d), lambda i: (0, 0)),
        out_shape=jax.ShapeDtypeStruct((b, d), F32),
        compiler_params=_params("arbitrary"),
        name=name,
    )(state_t, u, w_group, scale)


def _head_cols(hd):
    return slice(hd * MEM_HEAD_DIM, (hd + 1) * MEM_HEAD_DIM)


def _mem_rows(xs, g_ref, wq_ref, k_ref, v_ref, wo_ref):
    wq = wq_ref[...].astype(BF16)
    kb = k_ref[...].astype(BF16)
    vb = v_ref[...].astype(BF16)
    scale = MEM_HEAD_DIM ** -0.5
    hs = [_rms(x, g_ref[...]).astype(BF16) for x in xs]
    qs = [jnp.dot(h, wq, preferred_element_type=F32).astype(BF16) for h in hs]
    ss = [[lax.dot_general(q[:, _head_cols(hd)], kb[:, _head_cols(hd)], NT_DIMS, preferred_element_type=F32) * scale
           for hd in range(MEM_HEADS)] for q in qs]
    ps = [[_softmax_rows(s)[0].astype(BF16) for s in row] for row in ss]
    os_ = [jnp.concatenate([jnp.dot(p, vb[:, _head_cols(hd)], preferred_element_type=F32).astype(BF16)
                            for hd, p in enumerate(row)], axis=1) for row in ps]
    wo = wo_ref[...].astype(BF16)
    return [x + jnp.dot(o, wo, preferred_element_type=F32) for x, o in zip(xs, os_)]


def _mem_specs(layer, gains_layer, d, dm, n_mem, seq_of_step):
    kv_spec = pl.BlockSpec((None, None, n_mem, dm), lambda i: (layer, seq_of_step(i), 0, 0))
    return [_layer_spec(gains_layer, (1, d), lambda i: (0, 0)), _resident_spec(layer, (d, dm)),
            kv_spec, kv_spec, _resident_spec(layer, (dm, d))]


def _mem_sample_kernel(x_ref, g_ref, wq_ref, k_ref, v_ref, wo_ref, o_ref):
    n_b = x_ref.shape[0]
    n_mem = k_ref.shape[1] // MEM_HEADS
    head_rows = lambda ref, b, hd: ref[b, pl.ds(hd, n_mem, stride=MEM_HEADS), :].astype(BF16)
    x = x_ref[...]
    h = _rms(x, g_ref[...]).astype(BF16)
    q = jnp.dot(h, wq_ref[...].astype(BF16), preferred_element_type=F32).astype(BF16)
    scale = MEM_HEAD_DIM ** -0.5
    row = lax.broadcasted_iota(jnp.int32, (n_b, 1), 0)
    pairs = [(b, hd) for b in range(n_b) for hd in range(MEM_HEADS)]
    ss = [lax.dot_general(q[:, _head_cols(hd)], head_rows(k_ref, b, hd), NT_DIMS,
                          preferred_element_type=F32) * scale for b, hd in pairs]
    ps = [_softmax_rows(s)[0].astype(BF16) for s in ss]
    outs = [jnp.zeros((n_b, MEM_HEAD_DIM), F32) for _ in range(MEM_HEADS)]
    for (b, hd), p in zip(pairs, ps):
        o_b = jnp.dot(p, head_rows(v_ref, b, hd), preferred_element_type=F32)
        outs[hd] = jnp.where(row == b, o_b, outs[hd])
    o = jnp.concatenate(outs, axis=1).astype(BF16)
    o_ref[...] = x + jnp.dot(o, wo_ref[...].astype(BF16), preferred_element_type=F32)


def mem_block_sample(x, g, w_q, mk, mv, w_o, layer, *, name):
    n_b, d = x.shape
    gains, lg = g
    dm = w_q.shape[2]
    kv_spec = _layer_spec(layer, mk.shape[1:], lambda i: (0, 0, 0))
    x_spec = pl.BlockSpec((n_b, d), lambda i: (0, 0))
    return pl.pallas_call(
        _mem_sample_kernel,
        grid=(1,),
        in_specs=[x_spec,
                  _layer_spec(lg, (1, d), lambda i: (0, 0)),
                  _layer_spec(layer, (d, dm), lambda i: (0, 0)),
                  kv_spec, kv_spec,
                  _layer_spec(layer, (dm, d), lambda i: (0, 0))],
        out_specs=x_spec,
        out_shape=jax.ShapeDtypeStruct((n_b, d), F32),
        compiler_params=_params("arbitrary"),
        name=name,
    )(x, gains, w_q, mk, mv, w_o)


def _strided_rows(ref, start, dil):
    if dil == 1:
        return ref[pl.ds(start, DIL_BLOCK), :]
    return ref[pl.ds(start, DIL_BLOCK, stride=dil), :]


def _dil_group_attention(q_refs, k_ref, v_ref, o_ref, lse_ref, *, dil, seq):
    span = dil * DIL_BLOCK
    nb = seq // span
    qi = lax.broadcasted_iota(jnp.int32, (DIL_BLOCK, 2 * DIL_BLOCK), 0)
    kj = lax.broadcasted_iota(jnp.int32, (DIL_BLOCK, 2 * DIL_BLOCK), 1)
    dist = qi + DIL_BLOCK - kj
    in_window = (dist >= 0) & (dist <= DIL_BLOCK)
    causal = (lax.broadcasted_iota(jnp.int32, (DIL_BLOCK, DIL_BLOCK), 0)
              >= lax.broadcasted_iota(jnp.int32, (DIL_BLOCK, DIL_BLOCK), 1))
    scale = HEAD_DIM ** -0.5

    def store_rows(ref, lead, start, val):
        rows = pl.ds(start, DIL_BLOCK) if dil == 1 else pl.ds(start, DIL_BLOCK, stride=dil)
        ref[lead + (rows, slice(None))] = val

    masks = {False: jnp.concatenate([in_window] * KV_REP, axis=0),
             True: jnp.concatenate([in_window & (kj >= DIL_BLOCK)] * KV_REP, axis=0),
             None: jnp.concatenate([causal] * KV_REP, axis=0)}

    def scores(jg, i):
        if nb == 1:
            b, start, first = 0, jg * DIL_GROUP + i, None
        elif nb <= DIL_GROUP:
            r, b = jg * (DIL_GROUP // nb) + i // nb, i % nb
            start, first = b * span + r, b == 0
        else:
            r, b = jg // (nb // DIL_GROUP), (jg % (nb // DIL_GROUP)) * DIL_GROUP + i
            start, first = b * span + r, (b == 0 if i == 0 else False)
        if dil == 1:
            start = pl.multiple_of(start, DIL_BLOCK)
        q2 = jnp.concatenate([_strided_rows(q, start, dil) for q in q_refs], axis=0).astype(BF16)
        k_cur = _strided_rows(k_ref, start, dil)
        v_cur = _strided_rows(v_ref, start, dil)
        if nb > 1:
            prev = jnp.maximum(b - 1, 0) * span + (start - b * span)
            if dil == 1:
                prev = pl.multiple_of(prev, DIL_BLOCK)
            kcat = jnp.concatenate([_strided_rows(k_ref, prev, dil), k_cur], axis=0).astype(BF16)
            vcat = jnp.concatenate([_strided_rows(v_ref, prev, dil), v_cur], axis=0).astype(BF16)
        else:
            kcat, vcat = k_cur.astype(BF16), v_cur.astype(BF16)
        s = lax.dot_general(q2, kcat, NT_DIMS, preferred_element_type=F32) * scale
        if isinstance(first, (bool, type(None))):
            valid = masks[first]
        else:
            valid = masks[False] & (jnp.concatenate([kj] * KV_REP, axis=0) >= jnp.where(first, DIL_BLOCK, 0))
        return start, jnp.where(valid, s, NEG), vcat

    def block_group(jg, carry):
        staged = [scores(jg, i) for i in range(DIL_GROUP)]
        probs = [_softmax_rows(s) for _, s, _ in staged]
        outs = [jnp.dot(p.astype(BF16), vcat, preferred_element_type=F32)
                for (p, _, _), (_, _, vcat) in zip(probs, staged)]
        for (start, _, _), (_, m, den), o2 in zip(staged, probs, outs):
            lse2 = m + jnp.log(den)
            for hh in range(KV_REP):
                rows = slice(hh * DIL_BLOCK, (hh + 1) * DIL_BLOCK)
                store_rows(o_ref, (hh,), start, o2[rows])
                store_rows(lse_ref, (hh,), start, jnp.broadcast_to(lse2[rows], (DIL_BLOCK, HEAD_DIM)))
        return carry

    lax.fori_loop(0, seq // (DIL_BLOCK * DIL_GROUP), block_group, 0)


def _dil_prompt_kernel(*refs, seq):
    n_g = len(DIL_PATTERNS)
    q_refs = refs[:n_g * KV_REP]
    k_ref, v_ref, out_ref, o_scr, lse_scr = refs[n_g * KV_REP:]
    for gi, (_, dil) in enumerate(DIL_PATTERNS):
        _dil_group_attention(q_refs[gi * KV_REP:(gi + 1) * KV_REP], k_ref, v_ref, o_scr.at[gi], lse_scr.at[gi],
                             dil=dil, seq=seq)

    def merge(c, carry):
        rows = pl.ds(pl.multiple_of(c * MERGE_ROWS, MERGE_ROWS), MERGE_ROWS)
        for hh in range(KV_REP):
            lses = [lse_scr[gi, hh, rows, :] for gi in range(n_g)]
            m = functools.reduce(jnp.maximum, lses)
            es = [jnp.exp(l - m) for l in lses]
            den = functools.reduce(jnp.add, es)
            acc = sum((e / den) * o_scr[gi, hh, rows, :] for gi, e in enumerate(es))
            out_ref[hh, rows, :] = acc.astype(out_ref.dtype)
        return carry

    lax.fori_loop(0, seq // MERGE_ROWS, merge, 0)


def dil_attention_prompt(q, k, v, *, name):
    n, s, _ = q.shape
    n_g = len(DIL_PATTERNS)
    q_map = lambda head: (lambda b, j: (b, 0, head + KV_REP * j))
    q_specs = [pl.BlockSpec((None, s, HEAD_DIM), q_map(gi * HEADS_PER_GROUP + hh))
               for gi in range(n_g) for hh in range(KV_REP)]
    head_block = pl.BlockSpec((None, s, HEAD_DIM), lambda b, j: (b, 0, j))
    return pl.pallas_call(
        functools.partial(_dil_prompt_kernel, seq=s),
        grid=(n, N_KV_HEADS),
        in_specs=q_specs + [head_block, head_block],
        out_specs=pl.BlockSpec((None, KV_REP, s, HEAD_DIM), lambda b, j: (b, j, 0, 0)),
        out_shape=jax.ShapeDtypeStruct((n, HEADS_PER_GROUP, s, HEAD_DIM), BF16),
        scratch_shapes=[pltpu.VMEM((n_g, KV_REP, s, HEAD_DIM), F32), pltpu.VMEM((n_g, KV_REP, s, HEAD_DIM), F32)],
        compiler_params=_params("parallel", "parallel"),
        name=name,
    )(*([q] * (n_g * KV_REP)), k, v)


def _heads_out_mem_kernel(o_ref, w_ref, x_ref, g_ref, wq_ref, k_ref, v_ref, wo_ref, out_ref):
    tm = x_ref.shape[0]
    parts = [slice(c * MEM_ROWS, (c + 1) * MEM_ROWS) for c in range(tm // MEM_ROWS)]
    w = w_ref[...]
    xs = [x_ref[rows, :] + jnp.dot(jnp.concatenate([o_ref[h, rows, :] for h in range(o_ref.shape[0])], axis=1), w,
                                   preferred_element_type=F32) for rows in parts]
    for rows, y in zip(parts, _mem_rows(xs, g_ref, wq_ref, k_ref, v_ref, wo_ref)):
        out_ref[rows, :] = y


def heads_out_mem(o, w, x, g, w_q, mk, mv, w_o, layer, *, seq, tm, name):
    m, d = x.shape
    w, lw = w
    gains, lg = g
    n_mem, dm = mk.shape[2], mk.shape[3]
    tps = seq // tm
    return pl.pallas_call(
        _heads_out_mem_kernel,
        grid=(m // tm,),
        in_specs=[pl.BlockSpec((None,) + o.shape[1:2] + (tm, o.shape[3]), lambda i: (i // tps, 0, i % tps, 0)),
                  _resident_spec(lw, w.shape[1:]),
                  pl.BlockSpec((tm, d), lambda i: (i, 0))]
                 + _mem_specs(layer, lg, d, dm, n_mem, lambda i: i // tps),
        out_specs=pl.BlockSpec((tm, d), lambda i: (i, 0)),
        out_shape=jax.ShapeDtypeStruct((m, d), F32),
        compiler_params=_params("parallel"),
        name=name,
    )(o, w, x, gains, w_q, mk, mv, w_o)


def _dil_sample_kernel(q_ref, kn_ref, vn_ref, *refs):
    n_g = len(DIL_PATTERNS)
    k_refs, v_refs, o_ref = refs[:n_g], refs[n_g:2 * n_g], refs[2 * n_g]

    def head_rows(ref, kv):
        if len(ref.shape) == 2:
            return _strided_rows(ref, kv, N_KV_HEADS).astype(BF16)
        return ref[:, kv, :].astype(BF16)

    scale = HEAD_DIM ** -0.5
    head_row = lax.broadcasted_iota(jnp.int32, (HEADS_PER_GROUP, 1), 0)
    kn = kn_ref[...].astype(BF16).astype(F32)
    vn = vn_ref[...].astype(BF16).astype(F32)
    outs, lses = [], []
    for gi, (window, dil) in enumerate(DIL_PATTERNS):
        qg = q_ref[gi * HEADS_PER_GROUP:(gi + 1) * HEADS_PER_GROUP, :].astype(BF16)
        qf = qg.astype(F32)
        s = jnp.zeros((HEADS_PER_GROUP, DIL_BLOCK), F32)
        s_new = jnp.zeros((HEADS_PER_GROUP, 1), F32)
        for kv in range(N_KV_HEADS):
            mine = (head_row >= kv * KV_REP) & (head_row < (kv + 1) * KV_REP)
            kc = head_rows(k_refs[gi], kv)
            s = jnp.where(mine, lax.dot_general(qg, kc, NT_DIMS, preferred_element_type=F32), s)
            s_new = jnp.where(mine, jnp.sum(qf * kn[kv:kv + 1, :], axis=-1, keepdims=True), s_new)
        s = s * scale
        s_new = s_new * scale
        m = jnp.maximum(jnp.max(s, axis=-1, keepdims=True), s_new)
        e = jnp.exp(s - m)
        e_new = jnp.exp(s_new - m)
        den = jnp.sum(e, axis=-1, keepdims=True) + e_new
        p = (e / den).astype(BF16)
        p_new = (e_new / den).astype(BF16).astype(F32)
        o = jnp.zeros((HEADS_PER_GROUP, HEAD_DIM), F32)
        for kv in range(N_KV_HEADS):
            mine = (head_row >= kv * KV_REP) & (head_row < (kv + 1) * KV_REP)
            vc = head_rows(v_refs[gi], kv)
            o_kv = jnp.dot(p, vc, preferred_element_type=F32) + p_new * vn[kv:kv + 1, :]
            o = jnp.where(mine, o_kv, o)
        outs.append(o)
        lses.append(m + jnp.log(den))
    mm = jnp.maximum(jnp.maximum(lses[0], lses[1]), lses[2])
    es = [jnp.exp(l - mm) for l in lses]
    den = es[0] + es[1] + es[2]
    o_ref[...] = (es[0] / den) * outs[0] + (es[1] / den) * outs[1] + (es[2] / den) * outs[2]


def _cache_rows_spec(b, wbuf, window, dil):
    first = wbuf - window
    assert window // dil == DIL_BLOCK and window <= wbuf and wbuf % dil == 0 and first % (dil * DIL_BLOCK) == 0
    rows_per_token = N_KV_HEADS
    if dil * rows_per_token < 8:
        blk = first // DIL_BLOCK
        return ((b, wbuf * rows_per_token, HEAD_DIM),
                pl.BlockSpec((None, DIL_BLOCK * rows_per_token, HEAD_DIM), lambda i: (i, blk, 0)))
    assert (dil * rows_per_token) % 8 == 0
    blk = first // dil // DIL_BLOCK
    return ((b, wbuf // dil, dil * rows_per_token // 8, 8, HEAD_DIM),
            pl.BlockSpec((None, DIL_BLOCK, None, 8, HEAD_DIM), lambda i: (i, blk, 0, 0, 0)))


def dil_attention_sample(q, k_new, v_new, cache_k, cache_v, *, name):
    b, qd = q.shape
    wbuf = cache_k.shape[1]
    n_heads = qd // HEAD_DIM
    small = lambda rows: pl.BlockSpec((None, rows, HEAD_DIM), lambda i: (i, 0, 0))
    views_k, views_v, specs = [], [], []
    for window, dil in DIL_PATTERNS:
        view, spec = _cache_rows_spec(b, wbuf, window, dil)
        views_k.append(cache_k.reshape(view))
        views_v.append(cache_v.reshape(view))
        specs.append(spec)
    o = pl.pallas_call(
        _dil_sample_kernel,
        grid=(b,),
        in_specs=[small(n_heads), small(N_KV_HEADS), small(N_KV_HEADS)] + specs + specs,
        out_specs=small(HEADS_PER_GROUP),
        out_shape=jax.ShapeDtypeStruct((b, HEADS_PER_GROUP, HEAD_DIM), F32),
        compiler_params=_params("parallel"),
        name=name,
    )(q.reshape(b, n_heads, HEAD_DIM), k_new.reshape(b, N_KV_HEADS, HEAD_DIM),
      v_new.reshape(b, N_KV_HEADS, HEAD_DIM), *views_k, *views_v)
    return o.reshape(b, HEADS_PER_GROUP * HEAD_DIM)


def _trunks(xp, xs, mem_p, mem_s, w, *, batch, seq, pool_state, win_k, win_v):
    depth = w["ffn_w_gate"].shape[0]
    n_a = w["pool_w_in"].shape[0]
    d = xp.shape[1]
    dec_batch = xs.shape[0]
    pool_p, pool_s = [], []
    kp = vp = kp_rows = vp_rows = ks = vs = None
    for l in range(depth):
        if l == n_a:
            kv_w = [(w["w_k_shared"], 0), (w["w_v_shared"], 0)]
            kp, vp, kp_rows, vp_rows = (a[0] for a in shared_kv(xp, w["norm_kv"], w["w_k_shared"], w["w_v_shared"],
                                                                 tm=512, name="p_kv"))
            ks, vs = norm_matmul(xs, (w["norm_kv"], 0), kv_w, [F32, F32], tm=1024, tn=512, name="s_kv")
        mem_args = ((w["norm_mem_q"], l), w["mem_w_q"], *mem_p, w["mem_w_o"], l)
        if l < n_a:
            xp, tail = pool_mem_block(xp, (w["norm_mix"], l), w["pool_w_in"], w["pool_w_group"], w["pool_scale"],
                                      w["pool_w_out"], *mem_args, seq=seq, tm=POOL_TM, name=f"p_pool_mem{l}")
            pool_p.append(tail[:, POOL_HALO - POOL_STATE:])
            (u,) = norm_matmul(xs, (w["norm_mix"], l), [(w["pool_w_in"], l)], [F32], tm=1024, tn=512,
                               name=f"s_pool_in{l}")
            state = pool_state[l]
            pool_s.append(jnp.concatenate([state[:, 1:], u[:, None, :]], axis=1))
            z = pool_sample(jnp.swapaxes(state, 0, 1), u, w["pool_w_group"], w["pool_scale"], l, name=f"s_pool{l}")
            xs = matmul_residual(z, (w["pool_w_out"], l), xs, tm=1024, tn=1024, name=f"s_pool_out{l}")
        else:
            j = l - n_a
            q = norm_matmul_resident(xp, (w["norm_mix"], l), (w["dil_w_q"], j), F32, tm=512, name=f"p_dil_q{j}")
            q3, k3, v3 = (a.reshape(batch, seq, -1) for a in (q, kp, vp))
            o = dil_attention_prompt(q3, k3, v3, name=f"p_dil_attn{j}")
            xp = heads_out_mem(o, (w["dil_w_o"], j), xp, *mem_args, seq=seq, tm=512, name=f"p_dil_out_mem{j}")
            (q,) = norm_matmul(xs, (w["norm_mix"], l), [(w["dil_w_q"], j)], [F32], tm=1024, tn=512,
                               name=f"s_dil_q{j}")
            o = dil_attention_sample(q, ks, vs, win_k, win_v, name=f"s_dil_attn{j}")
            xs = matmul_residual(o, (w["dil_w_o"], j), xs, tm=1024, tn=1024, name=f"s_dil_out{j}")
        xs = mem_block_sample(xs, (w["norm_mem_q"], l), w["mem_w_q"], *mem_s, w["mem_w_o"], l, name=f"s_mem{l}")
        xs_pad = jnp.pad(xs, ((0, EXTRA_ROWS - dec_batch), (0, 0)))
        xp, xs_pad = ffn(xp, xs_pad, (w["norm_ffn"], l), w["ffn_w_gate"], w["ffn_w_up"], w["ffn_w_down"], l,
                         tm=1024, tf=256, name=f"ffn{l}", final_gain=w["norm_final"] if l == depth - 1 else None)
        xs = xs_pad[:dec_batch]
    return (xp, jnp.stack(pool_p), kp_rows, vp_rows), (xs, jnp.stack(pool_s), ks, vs)


def kernel(x_prompt, x_sample, mem_prompt, state_pool, cache_win_k, cache_win_v, cache_mem_k, cache_mem_v,
           norm_mix, norm_mem_q, norm_mem_kv, norm_ffn, pool_w_in, pool_w_group, pool_scale, pool_w_out,
           norm_kv, w_k_shared, w_v_shared, dil_w_q, dil_w_o, mem_w_q, mem_w_k, mem_w_v, mem_w_o,
           ffn_w_gate, ffn_w_up, ffn_w_down, norm_final):
    batch, seq, d = x_prompt.shape
    dec_batch, dec_seq, _ = x_sample.shape
    depth = norm_mix.shape[0]
    n_mem = mem_prompt.shape[1]
    rows3 = lambda a: a.reshape(a.shape[0], 1, a.shape[1])
    w = dict(norm_mix=rows3(norm_mix), norm_mem_q=rows3(norm_mem_q), norm_ffn=rows3(norm_ffn),
             norm_kv=norm_kv.reshape(1, 1, d), norm_final=norm_final, pool_scale=rows3(pool_scale),
             pool_w_in=pool_w_in.astype(BF16), pool_w_group=pool_w_group.astype(BF16),
             pool_w_out=pool_w_out.astype(BF16), dil_w_q=dil_w_q.astype(BF16), dil_w_o=dil_w_o.astype(BF16),
             mem_w_q=mem_w_q.astype(BF16), mem_w_o=mem_w_o.astype(BF16),
             w_k_shared=w_k_shared[None], w_v_shared=w_v_shared[None],
             ffn_w_gate=ffn_w_gate, ffn_w_up=ffn_w_up, ffn_w_down=ffn_w_down)

    mk_p, mv_p, mk_rows, mv_rows = shared_kv(mem_prompt.reshape(batch * n_mem, d), rows3(norm_mem_kv),
                                             mem_w_k, mem_w_v, tm=512, name="mem_kv")
    mk_p = mk_p.reshape(depth, batch, n_mem, -1)
    mv_p = mv_p.reshape(depth, batch, n_mem, -1)

    assert dec_seq == 1
    mk_s = cache_mem_k.reshape(depth, dec_batch, n_mem * MEM_HEADS, MEM_HEAD_DIM)
    mv_s = cache_mem_v.reshape(depth, dec_batch, n_mem * MEM_HEADS, MEM_HEAD_DIM)
    (y_p, pool_p, k_p, v_p), (y_s, pool_s, k_s, v_s) = _trunks(
        x_prompt.reshape(batch * seq, d), x_sample.reshape(dec_batch, d), (mk_p, mv_p), (mk_s, mv_s), w,
        batch=batch, seq=seq, pool_state=state_pool, win_k=cache_win_k, win_v=cache_win_v)

    max_window = max(wd for wd, _ in DIL_PATTERNS)
    keep_from = max(0, seq - max_window)
    kv_shape = (N_KV_HEADS, HEAD_DIM)
    mem_shape = (depth, batch, n_mem, MEM_HEADS, MEM_HEAD_DIM)
    return (y_p.reshape(batch, seq, d), y_s.reshape(dec_batch, dec_seq, d), pool_p, pool_s,
            k_p.reshape(batch, seq, *kv_shape)[:, keep_from:], v_p.reshape(batch, seq, *kv_shape)[:, keep_from:],
            k_s.reshape(dec_batch, dec_seq, *kv_shape), v_s.reshape(dec_batch, dec_seq, *kv_shape),
            mk_rows.reshape(mem_shape), mv_rows.reshape(mem_shape))
```

```python
import functools

import jax
import jax.numpy as jnp
from jax import lax
from jax.experimental import pallas as pl
from jax.experimental.pallas import tpu as pltpu

F32 = jnp.float32
BF16 = jnp.bfloat16

RMS_EPS = 1e-6
POOL_WINDOWS = (2, 4, 8, 16)
POOL_STATE = max(POOL_WINDOWS) - 1
POOL_HALO = 16
POOL_CHUNK = 64
POOL_TM = 512
POOL_PART = 256
EXTRA_ROWS = 8
DIL_PATTERNS = ((128, 1), (512, 4), (2048, 16))
DIL_BLOCK = 128
DIL_GROUP = 8
MERGE_ROWS = 256
MEM_ROWS = 256
HEAD_DIM = 128
N_KV_HEADS = 4
HEADS_PER_GROUP = 8
KV_REP = HEADS_PER_GROUP // N_KV_HEADS
MEM_HEADS = 4
MEM_HEAD_DIM = 128
NEG = -1e30
VMEM_LIMIT = 56 * 1024 * 1024

NT_DIMS = (((1,), (1,)), ((), ()))


def _params(*sem):
    return pltpu.CompilerParams(dimension_semantics=sem, vmem_limit_bytes=VMEM_LIMIT)


def _layer_spec(layer, block, index_map):
    return pl.BlockSpec((None,) + tuple(block), lambda *g: (layer,) + tuple(index_map(*g)))


def _resident_spec(layer, block):
    zeros = (0,) * len(block)
    return pl.BlockSpec((None,) + tuple(block), lambda *g: (layer,) + zeros, pipeline_mode=pl.Buffered(1))


def _rms(x, g):
    ms = jnp.mean(x * x, axis=-1, keepdims=True)
    return x * lax.rsqrt(ms + RMS_EPS) * g


def _lhs_scratch_dtype(rows):
    return BF16 if rows % 16 == 0 else F32


def _softmax_rows(s):
    m = jnp.max(s, axis=-1, keepdims=True)
    e = jnp.exp(s - m)
    den = jnp.sum(e, axis=-1, keepdims=True)
    return e / den, m, den


def _norm_matmul_kernel(x_ref, g_ref, *refs, n_w):
    w_refs, o_refs, h_ref = refs[:n_w], refs[n_w:2 * n_w], refs[2 * n_w]

    @pl.when(pl.program_id(1) == 0)
    def _():
        h_ref[...] = _rms(x_ref[...], g_ref[...]).astype(h_ref.dtype)

    h = h_ref[...].astype(BF16)
    for w_ref, o_ref in zip(w_refs, o_refs):
        o_ref[...] = jnp.dot(h, w_ref[...].astype(BF16), preferred_element_type=F32).astype(o_ref.dtype)


def norm_matmul(x, g, ws, out_dtypes, *, tm, tn, name):
    m, d = x.shape
    gains, lg = g
    n = ws[0][0].shape[2]
    tm, tn = min(tm, m), min(tn, n)
    n_w = len(ws)
    return pl.pallas_call(
        functools.partial(_norm_matmul_kernel, n_w=n_w),
        grid=(m // tm, n // tn),
        in_specs=[pl.BlockSpec((tm, d), lambda i, j: (i, 0)),
                  _layer_spec(lg, (1, d), lambda i, j: (0, 0))]
                 + [_layer_spec(lw, (d, tn), lambda i, j: (0, j)) for _, lw in ws],
        out_specs=[pl.BlockSpec((tm, tn), lambda i, j: (i, j))] * n_w,
        out_shape=[jax.ShapeDtypeStruct((m, n), dt) for dt in out_dtypes],
        scratch_shapes=[pltpu.VMEM((tm, d), _lhs_scratch_dtype(tm))],
        compiler_params=_params("parallel", "arbitrary"),
        name=name,
    )(x, gains, *[w for w, _ in ws])


def _emit_kv(h, wk_ref, wv_ref, k_ref, v_ref, kh_ref, vh_ref):
    tm = h.shape[0]
    for w_ref, o_ref, oh_ref in ((wk_ref, k_ref, kh_ref), (wv_ref, v_ref, vh_ref)):
        r = jnp.dot(h, w_ref[...].astype(BF16), preferred_element_type=F32)
        o_ref[...] = r
        for hd in range(N_KV_HEADS):
            oh_ref[pl.ds(hd, tm, stride=N_KV_HEADS), :] = r[:, hd * HEAD_DIM:(hd + 1) * HEAD_DIM]


def _shared_kv_kernel(x_ref, g_ref, wk_ref, wv_ref, k_ref, v_ref, kh_ref, vh_ref):
    _emit_kv(_rms(x_ref[...], g_ref[...]).astype(BF16), wk_ref, wv_ref, k_ref, v_ref, kh_ref, vh_ref)


def shared_kv(x, gains, w_k, w_v, *, tm, name):
    m, d = x.shape
    n_layers, _, n = w_k.shape
    layer_rows = lambda l, i: (l, i, 0)
    per_layer = lambda l, i: (l, 0, 0)
    return pl.pallas_call(
        _shared_kv_kernel,
        grid=(n_layers, m // tm),
        in_specs=[pl.BlockSpec((tm, d), lambda l, i: (i, 0)), pl.BlockSpec((None, 1, d), per_layer),
                  pl.BlockSpec((None, d, n), per_layer), pl.BlockSpec((None, d, n), per_layer)],
        out_specs=[pl.BlockSpec((None, tm, n), layer_rows)] * 2
                  + [pl.BlockSpec((None, tm * N_KV_HEADS, HEAD_DIM), layer_rows)] * 2,
        out_shape=[jax.ShapeDtypeStruct((n_layers, m, n), F32)] * 2
                  + [jax.ShapeDtypeStruct((n_layers, m * N_KV_HEADS, HEAD_DIM), F32)] * 2,
        compiler_params=_params("parallel", "parallel"),
        name=name,
    )(x, gains, w_k, w_v)


def _norm_matmul_resident_kernel(x_ref, g_ref, w_ref, *rest, with_kv):
    x = x_ref[...]
    xn = x * lax.rsqrt(jnp.mean(x * x, axis=-1, keepdims=True) + RMS_EPS)
    o_ref = rest[3] if with_kv else rest[0]
    o_ref[...] = jnp.dot((xn * g_ref[...]).astype(BF16), w_ref[...], preferred_element_type=F32).astype(o_ref.dtype)
    if with_kv:
        gkv_ref, wk_ref, wv_ref = rest[:3]
        _emit_kv((xn * gkv_ref[...]).astype(BF16), wk_ref, wv_ref, *rest[4:])


def norm_matmul_resident(x, g, w, out_dtype, *, tm, name, kv=None):
    m, d = x.shape
    gains, lg = g
    w, lw = w
    n = w.shape[2]
    row = lambda i: (i, 0)
    in_specs = [pl.BlockSpec((tm, d), row), _layer_spec(lg, (1, d), lambda i: (0, 0)), _resident_spec(lw, (d, n))]
    out_specs = [pl.BlockSpec((tm, n), row)]
    out_shape = [jax.ShapeDtypeStruct((m, n), out_dtype)]
    args = [x, gains, w]
    if kv is not None:
        gains_kv, w_k, w_v = kv
        nk = w_k.shape[2]
        in_specs += [_layer_spec(0, (1, d), lambda i: (0, 0)), _resident_spec(0, (d, nk)), _resident_spec(0, (d, nk))]
        out_specs += [pl.BlockSpec((tm, nk), row)] * 2 + [pl.BlockSpec((tm * N_KV_HEADS, HEAD_DIM), row)] * 2
        out_shape += [jax.ShapeDtypeStruct((m, nk), F32)] * 2 + [jax.ShapeDtypeStruct((m * N_KV_HEADS, HEAD_DIM), F32)] * 2
        args += [gains_kv, w_k, w_v]
    return pl.pallas_call(
        functools.partial(_norm_matmul_resident_kernel, with_kv=kv is not None),
        grid=(m // tm,),
        in_specs=in_specs,
        out_specs=out_specs,
        out_shape=out_shape,
        compiler_params=_params("parallel"),
        name=name,
    )(*args)


def _matmul_res_kernel(a_ref, w_ref, x_ref, o_ref):
    a = a_ref[...].astype(BF16)
    o_ref[...] = x_ref[...] + jnp.dot(a, w_ref[...].astype(BF16), preferred_element_type=F32)


def matmul_residual(a, w, x, *, tm, tn, name):
    m, k = a.shape
    w, lw = w
    n = w.shape[2]
    tm, tn = min(tm, m), min(tn, n)
    return pl.pallas_call(
        _matmul_res_kernel,
        grid=(m // tm, n // tn),
        in_specs=[pl.BlockSpec((tm, k), lambda i, j: (i, 0)),
                  _layer_spec(lw, (k, tn), lambda i, j: (0, j)),
                  pl.BlockSpec((tm, tn), lambda i, j: (i, j))],
        out_specs=pl.BlockSpec((tm, tn), lambda i, j: (i, j)),
        out_shape=jax.ShapeDtypeStruct((m, n), F32),
        compiler_params=_params("parallel", "parallel"),
        name=name,
    )(a, w, x)


def _ffn_kernel(x_ref, xs_ref, g_ref, wg_ref, wu_ref, wd_ref, *rest, final_norm):
    if final_norm:
        gf_ref, o_ref, os_ref, h_ref = rest
    else:
        o_ref, os_ref, h_ref = rest
    i, j = pl.program_id(0), pl.program_id(1)
    tm = x_ref.shape[0]
    first_tile = i == 0

    @pl.when(j == 0)
    def _():
        x = x_ref[...]
        h_ref[0:tm, :] = _rms(x, g_ref[...]).astype(BF16)
        o_ref[...] = x

    @pl.when((j == 0) & first_tile)
    def _():
        xs = xs_ref[...]
        h_ref[tm:, :] = _rms(xs, g_ref[...]).astype(BF16)
        os_ref[...] = xs

    h = h_ref[...]
    gate = jnp.dot(h, wg_ref[...].astype(BF16), preferred_element_type=F32)
    up = jnp.dot(h, wu_ref[...].astype(BF16), preferred_element_type=F32)
    act = (gate * jax.nn.sigmoid(gate) * up).astype(BF16)
    wd = wd_ref[...].astype(BF16)
    o_ref[...] += jnp.dot(act[:tm], wd, preferred_element_type=F32)

    @pl.when(first_tile)
    def _():
        os_ref[...] += jnp.dot(act[tm:], wd, preferred_element_type=F32)

    if final_norm:
        last = j == pl.num_programs(1) - 1

        @pl.when(last)
        def _():
            o_ref[...] = _rms(o_ref[...], gf_ref[...])

        @pl.when(last & first_tile)
        def _():
            os_ref[...] = _rms(os_ref[...], gf_ref[...])


def ffn(x, xs, g, w_gate, w_up, w_down, layer, *, tm, tf, name, final_gain=None):
    m, d = x.shape
    rows_s = xs.shape[0]
    assert rows_s == EXTRA_ROWS and m % tm == 0
    gains, lg = g
    f = w_gate.shape[2]
    final_norm = final_gain is not None
    extra_specs = [pl.BlockSpec((1, d), lambda i, j: (0, 0))] if final_norm else []
    extra_args = [final_gain.reshape(1, d)] if final_norm else []
    return pl.pallas_call(
        functools.partial(_ffn_kernel, final_norm=final_norm),
        grid=(m // tm, f // tf),
        in_specs=[pl.BlockSpec((tm, d), lambda i, j: (i, 0)),
                  pl.BlockSpec((rows_s, d), lambda i, j: (0, 0)),
                  _layer_spec(lg, (1, d), lambda i, j: (0, 0)),
                  _layer_spec(layer, (d, tf), lambda i, j: (0, j)),
                  _layer_spec(layer, (d, tf), lambda i, j: (0, j)),
                  _layer_spec(layer, (tf, d), lambda i, j: (j, 0))] + extra_specs,
        out_specs=[pl.BlockSpec((tm, d), lambda i, j: (i, 0)), pl.BlockSpec((rows_s, d), lambda i, j: (0, 0))],
        out_shape=[jax.ShapeDtypeStruct((m, d), F32), jax.ShapeDtypeStruct((rows_s, d), F32)],
        scratch_shapes=[pltpu.VMEM((tm + rows_s, d), BF16)],
        compiler_params=_params("arbitrary", "arbitrary"),
        name=name,
    )(x, xs, gains, w_gate, w_up, w_down, *extra_args)


def _pool_block_kernel(x_ref, g_ref, win_ref, wg_ref, scale_ref, wout_ref, gm_ref, wq_ref, k_ref, v_ref, wo_ref,
                       o_ref, tail_ref, ext_ref, p_ref, *, tm, tiles_per_seq):
    d = x_ref.shape[1]
    gdim = d // len(POOL_WINDOWS)
    t_in_seq = pl.program_id(0) % tiles_per_seq

    @pl.when(t_in_seq == 0)
    def _():
        ext_ref[0:POOL_HALO, :] = jnp.zeros((POOL_HALO, d), F32)

    @pl.when(t_in_seq != 0)
    def _():
        ext_ref[0:POOL_HALO, :] = ext_ref[tm:tm + POOL_HALO, :]

    parts = [slice(c * POOL_PART, (c + 1) * POOL_PART) for c in range(tm // POOL_PART)]
    us = [jnp.dot(_rms(x_ref[rows, :], g_ref[...]).astype(BF16), win_ref[...], preferred_element_type=F32)
          for rows in parts]
    for rows, u in zip(parts, us):
        ext_ref[POOL_HALO + rows.start:POOL_HALO + rows.stop, :] = u
    tail_ref[...] = us[-1][POOL_PART - POOL_HALO:, :]
    ys = []
    for rows in parts:
        for c in range(rows.start // POOL_CHUNK, rows.stop // POOL_CHUNK):
            r0 = POOL_HALO + c * POOL_CHUNK
            pos = t_in_seq * tm + c * POOL_CHUNK + lax.broadcasted_iota(jnp.int32, (POOL_CHUNK, 1), 0)
            for gi, w in enumerate(POOL_WINDOWS):
                cols = slice(gi * gdim, (gi + 1) * gdim)
                u_q = ext_ref[r0:r0 + POOL_CHUNK, cols]
                acc = u_q
                for k in range(1, w):
                    acc = acc + ext_ref[r0 - k:r0 - k + POOL_CHUNK, cols]
                inv_cnt = 1.0 / jnp.minimum(pos + 1, w).astype(F32)
                p_ref[c * POOL_CHUNK:(c + 1) * POOL_CHUNK, cols] = (acc * inv_cnt - u_q).astype(BF16)
        zs = []
        for gi in range(len(POOL_WINDOWS)):
            cols = slice(gi * gdim, (gi + 1) * gdim)
            z = jnp.dot(p_ref[rows, cols], wg_ref[gi], preferred_element_type=F32)
            zs.append((z * scale_ref[:, cols]).astype(BF16))
        ys.append(x_ref[rows, :] + jnp.dot(jnp.concatenate(zs, axis=1), wout_ref[...],
                                            preferred_element_type=F32))
    for rows, y in zip(parts, _mem_rows(ys, gm_ref, wq_ref, k_ref, v_ref, wo_ref)):
        o_ref[rows, :] = y


def pool_mem_block(x, g, w_in, w_group, scale, w_out, g_mem, w_q, mk, mv, w_o, layer, *, seq, tm, name):
    m, d = x.shape
    gains, lg = g
    gains_mem, lgm = g_mem
    n_mem, dm = mk.shape[2], mk.shape[3]
    tiles_per_seq = seq // tm
    return pl.pallas_call(
        functools.partial(_pool_block_kernel, tm=tm, tiles_per_seq=tiles_per_seq),
        grid=(m // tm,),
        in_specs=[pl.BlockSpec((tm, d), lambda i: (i, 0)),
                  _layer_spec(lg, (1, d), lambda i: (0, 0)),
                  _resident_spec(layer, (d, d)),
                  _resident_spec(layer, w_group.shape[1:]),
                  _layer_spec(layer, (1, d), lambda i: (0, 0)),
                  _resident_spec(layer, (d, d))]
                 + _mem_specs(layer, lgm, d, dm, n_mem, lambda i: i // tiles_per_seq),
        out_specs=[pl.BlockSpec((tm, d), lambda i: (i, 0)),
                   pl.BlockSpec((None, POOL_HALO, d), lambda i: (i // tiles_per_seq, 0, 0))],
        out_shape=[jax.ShapeDtypeStruct((m, d), F32), jax.ShapeDtypeStruct((m // seq, POOL_HALO, d), F32)],
        scratch_shapes=[pltpu.VMEM((tm + POOL_HALO, d), F32), pltpu.VMEM((tm, d), BF16)],
        compiler_params=_params("arbitrary"),
        name=name,
    )(x, gains, w_in, w_group, scale, w_out, gains_mem, w_q, mk, mv, w_o)


def _pool_sample_kernel(state_ref, u_ref, wg_ref, scale_ref, z_ref):
    n_state, _, d = state_ref.shape
    gdim = d // len(POOL_WINDOWS)
    u = u_ref[...]
    for gi, w in enumerate(POOL_WINDOWS):
        cols = slice(gi * gdim, (gi + 1) * gdim)
        u_q = u[:, cols]
        acc = u_q
        for k in range(1, w):
            acc = acc + state_ref[n_state - k, :, cols]
        cnt = float(min(n_state + 1, w))
        p = (acc / cnt - u_q).astype(BF16)
        z = jnp.dot(p, wg_ref[gi].astype(BF16), preferred_element_type=F32)
        z_ref[:, cols] = z * scale_ref[:, cols]


def pool_sample(state_t, u, w_group, scale, layer, *, name):
    b, d = u.shape
    return pl.pallas_call(
        _pool_sample_kernel,
        grid=(1,),
        in_specs=[pl.BlockSpec(state_t.shape, lambda i: (0, 0, 0)),
                  pl.BlockSpec((b, d), lambda i: (0, 0)),
                  _layer_spec(layer, w_group.shape[1:], lambda i: (0, 0, 0)),
                  _layer_spec(layer, (1, d), lambda i: (0, 0))],
        out_specs=pl.BlockSpec((b, d), lambda i: (0, 0)),
        out_shape=jax.ShapeDtypeStruct((b, d), F32),
        compiler_params=_params("arbitrary"),
        name=name,
    )(state_t, u, w_group, scale)


def _head_cols(hd):
    return slice(hd * MEM_HEAD_DIM, (hd + 1) * MEM_HEAD_DIM)


def _mem_rows(xs, g_ref, wq_ref, k_ref, v_ref, wo_ref):
    wq = wq_ref[...].astype(BF16)
    kb = k_ref[...].astype(BF16)
    vb = v_ref[...].astype(BF16)
    scale = MEM_HEAD_DIM ** -0.5
    hs = [_rms(x, g_ref[...]).astype(BF16) for x in xs]
    qs = [jnp.dot(h, wq, preferred_element_type=F32).astype(BF16) for h in hs]
    ss = [[lax.dot_general(q[:, _head_cols(hd)], kb[:, _head_cols(hd)], NT_DIMS, preferred_element_type=F32) * scale
           for hd in range(MEM_HEADS)] for q in qs]
    ps = [[_softmax_rows(s)[0].astype(BF16) for s in row] for row in ss]
    os_ = [jnp.concatenate([jnp.dot(p, vb[:, _head_cols(hd)], preferred_element_type=F32).astype(BF16)
                            for hd, p in enumerate(row)], axis=1) for row in ps]
    wo = wo_ref[...].astype(BF16)
    return [x + jnp.dot(o, wo, preferred_element_type=F32) for x, o in zip(xs, os_)]


def _mem_specs(layer, gains_layer, d, dm, n_mem, seq_of_step):
    kv_spec = pl.BlockSpec((None, None, n_mem, dm), lambda i: (layer, seq_of_step(i), 0, 0))
    return [_layer_spec(gains_layer, (1, d), lambda i: (0, 0)), _resident_spec(layer, (d, dm)),
            kv_spec, kv_spec, _resident_spec(layer, (dm, d))]


def _mem_sample_kernel(x_ref, g_ref, wq_ref, k_ref, v_ref, wo_ref, o_ref):
    n_b = x_ref.shape[0]
    n_mem = k_ref.shape[1] // MEM_HEADS
    head_rows = lambda ref, b, hd: ref[b, pl.ds(hd, n_mem, stride=MEM_HEADS), :].astype(BF16)
    x = x_ref[...]
    h = _rms(x, g_ref[...]).astype(BF16)
    q = jnp.dot(h, wq_ref[...].astype(BF16), preferred_element_type=F32).astype(BF16)
    scale = MEM_HEAD_DIM ** -0.5
    row = lax.broadcasted_iota(jnp.int32, (n_b, 1), 0)
    pairs = [(b, hd) for b in range(n_b) for hd in range(MEM_HEADS)]
    ss = [lax.dot_general(q[:, _head_cols(hd)], head_rows(k_ref, b, hd), NT_DIMS,
                          preferred_element_type=F32) * scale for b, hd in pairs]
    ps = [_softmax_rows(s)[0].astype(BF16) for s in ss]
    outs = [jnp.zeros((n_b, MEM_HEAD_DIM), F32) for _ in range(MEM_HEADS)]
    for (b, hd), p in zip(pairs, ps):
        o_b = jnp.dot(p, head_rows(v_ref, b, hd), preferred_element_type=F32)
        outs[hd] = jnp.where(row == b, o_b, outs[hd])
    o = jnp.concatenate(outs, axis=1).astype(BF16)
    o_ref[...] = x + jnp.dot(o, wo_ref[...].astype(BF16), preferred_element_type=F32)


def mem_block_sample(x, g, w_q, mk, mv, w_o, layer, *, name):
    n_b, d = x.shape
    gains, lg = g
    dm = w_q.shape[2]
    kv_spec = _layer_spec(layer, mk.shape[1:], lambda i: (0, 0, 0))
    x_spec = pl.BlockSpec((n_b, d), lambda i: (0, 0))
    return pl.pallas_call(
        _mem_sample_kernel,
        grid=(1,),
        in_specs=[x_spec,
                  _layer_spec(lg, (1, d), lambda i: (0, 0)),
                  _layer_spec(layer, (d, dm), lambda i: (0, 0)),
                  kv_spec, kv_spec,
                  _layer_spec(layer, (dm, d), lambda i: (0, 0))],
        out_specs=x_spec,
        out_shape=jax.ShapeDtypeStruct((n_b, d), F32),
        compiler_params=_params("arbitrary"),
        name=name,
    )(x, gains, w_q, mk, mv, w_o)


def _strided_rows(ref, start, dil):
    if dil == 1:
        return ref[pl.ds(start, DIL_BLOCK), :]
    return ref[pl.ds(start, DIL_BLOCK, stride=dil), :]


def _dil_group_attention(q_refs, k_ref, v_ref, o_ref, lse_ref, *, dil, seq):
    span = dil * DIL_BLOCK
    nb = seq // span
    qi = lax.broadcasted_iota(jnp.int32, (DIL_BLOCK, 2 * DIL_BLOCK), 0)
    kj = lax.broadcasted_iota(jnp.int32, (DIL_BLOCK, 2 * DIL_BLOCK), 1)
    dist = qi + DIL_BLOCK - kj
    in_window = (dist >= 0) & (dist <= DIL_BLOCK)
    causal = (lax.broadcasted_iota(jnp.int32, (DIL_BLOCK, DIL_BLOCK), 0)
              >= lax.broadcasted_iota(jnp.int32, (DIL_BLOCK, DIL_BLOCK), 1))
    scale = HEAD_DIM ** -0.5

    def store_rows(ref, lead, start, val):
        rows = pl.ds(start, DIL_BLOCK) if dil == 1 else pl.ds(start, DIL_BLOCK, stride=dil)
        ref[lead + (rows, slice(None))] = val

    masks = {False: jnp.concatenate([in_window] * KV_REP, axis=0),
             True: jnp.concatenate([in_window & (kj >= DIL_BLOCK)] * KV_REP, axis=0),
             None: jnp.concatenate([causal] * KV_REP, axis=0)}

    def scores(jg, i):
        if nb == 1:
            b, start, first = 0, jg * DIL_GROUP + i, None
        elif nb <= DIL_GROUP:
            r, b = jg * (DIL_GROUP // nb) + i // nb, i % nb
            start, first = b * span + r, b == 0
        else:
            r, b = jg // (nb // DIL_GROUP), (jg % (nb // DIL_GROUP)) * DIL_GROUP + i
            start, first = b * span + r, (b == 0 if i == 0 else False)
        if dil == 1:
            start = pl.multiple_of(start, DIL_BLOCK)
        q2 = jnp.concatenate([_strided_rows(q, start, dil) for q in q_refs], axis=0).astype(BF16)
        k_cur = _strided_rows(k_ref, start, dil)
        v_cur = _strided_rows(v_ref, start, dil)
        if nb > 1:
            prev = jnp.maximum(b - 1, 0) * span + (start - b * span)
            if dil == 1:
                prev = pl.multiple_of(prev, DIL_BLOCK)
            kcat = jnp.concatenate([_strided_rows(k_ref, prev, dil), k_cur], axis=0).astype(BF16)
            vcat = jnp.concatenate([_strided_rows(v_ref, prev, dil), v_cur], axis=0).astype(BF16)
        else:
            kcat, vcat = k_cur.astype(BF16), v_cur.astype(BF16)
        s = lax.dot_general(q2, kcat, NT_DIMS, preferred_element_type=F32) * scale
        if isinstance(first, (bool, type(None))):
            valid = masks[first]
        else:
            valid = masks[False] & (jnp.concatenate([kj] * KV_REP, axis=0) >= jnp.where(first, DIL_BLOCK, 0))
        return start, jnp.where(valid, s, NEG), vcat

    def block_group(jg, carry):
        staged = [scores(jg, i) for i in range(DIL_GROUP)]
        probs = [_softmax_rows(s) for _, s, _ in staged]
        outs = [jnp.dot(p.astype(BF16), vcat, preferred_element_type=F32)
                for (p, _, _), (_, _, vcat) in zip(probs, staged)]
        for (start, _, _), (_, m, den), o2 in zip(staged, probs, outs):
            lse2 = m + jnp.log(den)
            for hh in range(KV_REP):
                rows = slice(hh * DIL_BLOCK, (hh + 1) * DIL_BLOCK)
                store_rows(o_ref, (hh,), start, o2[rows])
                store_rows(lse_ref, (hh,), start, jnp.broadcast_to(lse2[rows], (DIL_BLOCK, HEAD_DIM)))
        return carry

    lax.fori_loop(0, seq // (DIL_BLOCK * DIL_GROUP), block_group, 0)


def _dil_prompt_kernel(*refs, seq):
    n_g = len(DIL_PATTERNS)
    q_refs = refs[:n_g * KV_REP]
    k_ref, v_ref, out_ref, o_scr, lse_scr = refs[n_g * KV_REP:]
    for gi, (_, dil) in enumerate(DIL_PATTERNS):
        _dil_group_attention(q_refs[gi * KV_REP:(gi + 1) * KV_REP], k_ref, v_ref, o_scr.at[gi], lse_scr.at[gi],
                             dil=dil, seq=seq)

    def merge(c, carry):
        rows = pl.ds(pl.multiple_of(c * MERGE_ROWS, MERGE_ROWS), MERGE_ROWS)
        for hh in range(KV_REP):
            lses = [lse_scr[gi, hh, rows, :] for gi in range(n_g)]
            m = functools.reduce(jnp.maximum, lses)
            es = [jnp.exp(l - m) for l in lses]
            den = functools.reduce(jnp.add, es)
            acc = sum((e / den) * o_scr[gi, hh, rows, :] for gi, e in enumerate(es))
            out_ref[hh, rows, :] = acc.astype(out_ref.dtype)
        return carry

    lax.fori_loop(0, seq // MERGE_ROWS, merge, 0)


def dil_attention_prompt(q, k, v, *, name):
    n, s, _ = q.shape
    n_g = len(DIL_PATTERNS)
    q_map = lambda head: (lambda b, j: (b, 0, head + KV_REP * j))
    q_specs = [pl.BlockSpec((None, s, HEAD_DIM), q_map(gi * HEADS_PER_GROUP + hh))
               for gi in range(n_g) for hh in range(KV_REP)]
    head_block = pl.BlockSpec((None, s, HEAD_DIM), lambda b, j: (b, 0, j))
    return pl.pallas_call(
        functools.partial(_dil_prompt_kernel, seq=s),
        grid=(n, N_KV_HEADS),
        in_specs=q_specs + [head_block, head_block],
        out_specs=pl.BlockSpec((None, KV_REP, s, HEAD_DIM), lambda b, j: (b, j, 0, 0)),
        out_shape=jax.ShapeDtypeStruct((n, HEADS_PER_GROUP, s, HEAD_DIM), BF16),
        scratch_shapes=[pltpu.VMEM((n_g, KV_REP, s, HEAD_DIM), F32), pltpu.VMEM((n_g, KV_REP, s, HEAD_DIM), F32)],
        compiler_params=_params("parallel", "parallel"),
        name=name,
    )(*([q] * (n_g * KV_REP)), k, v)


def _heads_out_mem_kernel(o_ref, w_ref, x_ref, g_ref, wq_ref, k_ref, v_ref, wo_ref, out_ref):
    tm = x_ref.shape[0]
    parts = [slice(c * MEM_ROWS, (c + 1) * MEM_ROWS) for c in range(tm // MEM_ROWS)]
    w = w_ref[...]
    xs = [x_ref[rows, :] + jnp.dot(jnp.concatenate([o_ref[h, rows, :] for h in range(o_ref.shape[0])], axis=1), w,
                                   preferred_element_type=F32) for rows in parts]
    for rows, y in zip(parts, _mem_rows(xs, g_ref, wq_ref, k_ref, v_ref, wo_ref)):
        out_ref[rows, :] = y


def heads_out_mem(o, w, x, g, w_q, mk, mv, w_o, layer, *, seq, tm, name):
    m, d = x.shape
    w, lw = w
    gains, lg = g
    n_mem, dm = mk.shape[2], mk.shape[3]
    tps = seq // tm
    return pl.pallas_call(
        _heads_out_mem_kernel,
        grid=(m // tm,),
        in_specs=[pl.BlockSpec((None,) + o.shape[1:2] + (tm, o.shape[3]), lambda i: (i // tps, 0, i % tps, 0)),
                  _resident_spec(lw, w.shape[1:]),
                  pl.BlockSpec((tm, d), lambda i: (i, 0))]
                 + _mem_specs(layer, lg, d, dm, n_mem, lambda i: i // tps),
        out_specs=pl.BlockSpec((tm, d), lambda i: (i, 0)),
        out_shape=jax.ShapeDtypeStruct((m, d), F32),
        compiler_params=_params("parallel"),
        name=name,
    )(o, w, x, gains, w_q, mk, mv, w_o)


def _dil_sample_kernel(q_ref, kn_ref, vn_ref, *refs):
    n_g = len(DIL_PATTERNS)
    k_refs, v_refs, o_ref = refs[:n_g], refs[n_g:2 * n_g], refs[2 * n_g]

    def head_rows(ref, kv):
        if len(ref.shape) == 2:
            return _strided_rows(ref, kv, N_KV_HEADS).astype(BF16)
        return ref[:, kv, :].astype(BF16)

    scale = HEAD_DIM ** -0.5
    head_row = lax.broadcasted_iota(jnp.int32, (HEADS_PER_GROUP, 1), 0)
    kn = kn_ref[...].astype(BF16).astype(F32)
    vn = vn_ref[...].astype(BF16).astype(F32)
    outs, lses = [], []
    for gi, (window, dil) in enumerate(DIL_PATTERNS):
        qg = q_ref[gi * HEADS_PER_GROUP:(gi + 1) * HEADS_PER_GROUP, :].astype(BF16)
        qf = qg.astype(F32)
        s = jnp.zeros((HEADS_PER_GROUP, DIL_BLOCK), F32)
        s_new = jnp.zeros((HEADS_PER_GROUP, 1), F32)
        for kv in range(N_KV_HEADS):
            mine = (head_row >= kv * KV_REP) & (head_row < (kv + 1) * KV_REP)
            kc = head_rows(k_refs[gi], kv)
            s = jnp.where(mine, lax.dot_general(qg, kc, NT_DIMS, preferred_element_type=F32), s)
            s_new = jnp.where(mine, jnp.sum(qf * kn[kv:kv + 1, :], axis=-1, keepdims=True), s_new)
        s = s * scale
        s_new = s_new * scale
        m = jnp.maximum(jnp.max(s, axis=-1, keepdims=True), s_new)
        e = jnp.exp(s - m)
        e_new = jnp.exp(s_new - m)
        den = jnp.sum(e, axis=-1, keepdims=True) + e_new
        p = (e / den).astype(BF16)
        p_new = (e_new / den).astype(BF16).astype(F32)
        o = jnp.zeros((HEADS_PER_GROUP, HEAD_DIM), F32)
        for kv in range(N_KV_HEADS):
            mine = (head_row >= kv * KV_REP) & (head_row < (kv + 1) * KV_REP)
            vc = head_rows(v_refs[gi], kv)
            o_kv = jnp.dot(p, vc, preferred_element_type=F32) + p_new * vn[kv:kv + 1, :]
            o = jnp.where(mine, o_kv, o)
        outs.append(o)
        lses.append(m + jnp.log(den))
    mm = jnp.maximum(jnp.maximum(lses[0], lses[1]), lses[2])
    es = [jnp.exp(l - mm) for l in lses]
    den = es[0] + es[1] + es[2]
    o_ref[...] = (es[0] / den) * outs[0] + (es[1] / den) * outs[1] + (es[2] / den) * outs[2]


def _cache_rows_spec(b, wbuf, window, dil):
    first = wbuf - window
    assert window // dil == DIL_BLOCK and window <= wbuf and wbuf % dil == 0 and first % (dil * DIL_BLOCK) == 0
    rows_per_token = N_KV_HEADS
    if dil * rows_per_token < 8:
        blk = first // DIL_BLOCK
        return ((b, wbuf * rows_per_token, HEAD_DIM),
                pl.BlockSpec((None, DIL_BLOCK * rows_per_token, HEAD_DIM), lambda i: (i, blk, 0)))
    assert (dil * rows_per_token) % 8 == 0
    blk = first // dil // DIL_BLOCK
    return ((b, wbuf // dil, dil * rows_per_token // 8, 8, HEAD_DIM),
            pl.BlockSpec((None, DIL_BLOCK, None, 8, HEAD_DIM), lambda i: (i, blk, 0, 0, 0)))


def dil_attention_sample(q, k_new, v_new, cache_k, cache_v, *, name):
    b, qd = q.shape
    wbuf = cache_k.shape[1]
    n_heads = qd // HEAD_DIM
    small = lambda rows: pl.BlockSpec((None, rows, HEAD_DIM), lambda i: (i, 0, 0))
    views_k, views_v, specs = [], [], []
    for window, dil in DIL_PATTERNS:
        view, spec = _cache_rows_spec(b, wbuf, window, dil)
        views_k.append(cache_k.reshape(view))
        views_v.append(cache_v.reshape(view))
        specs.append(spec)
    o = pl.pallas_call(
        _dil_sample_kernel,
        grid=(b,),
        in_specs=[small(n_heads), small(N_KV_HEADS), small(N_KV_HEADS)] + specs + specs,
        out_specs=small(HEADS_PER_GROUP),
        out_shape=jax.ShapeDtypeStruct((b, HEADS_PER_GROUP, HEAD_DIM), F32),
        compiler_params=_params("parallel"),
        name=name,
    )(q.reshape(b, n_heads, HEAD_DIM), k_new.reshape(b, N_KV_HEADS, HEAD_DIM),
      v_new.reshape(b, N_KV_HEADS, HEAD_DIM), *views_k, *views_v)
    return o.reshape(b, HEADS_PER_GROUP * HEAD_DIM)


def _trunks(xp, xs, mem_p, mem_s, w, *, batch, seq, pool_state, win_k, win_v):
    depth = w["ffn_w_gate"].shape[0]
    n_a = w["pool_w_in"].shape[0]
    d = xp.shape[1]
    dec_batch = xs.shape[0]
    pool_p, pool_s = [], []
    kp = vp = kp_rows = vp_rows = ks = vs = None
    for l in range(depth):
        if l == n_a:
            kv_w = [(w["w_k_shared"], 0), (w["w_v_shared"], 0)]
            ks, vs = norm_matmul(xs, (w["norm_kv"], 0), kv_w, [F32, F32], tm=1024, tn=512, name="s_kv")
        mem_args = ((w["norm_mem_q"], l), w["mem_w_q"], *mem_p, w["mem_w_o"], l)
        if l < n_a:
            xp, tail = pool_mem_block(xp, (w["norm_mix"], l), w["pool_w_in"], w["pool_w_group"], w["pool_scale"],
                                      w["pool_w_out"], *mem_args, seq=seq, tm=POOL_TM, name=f"p_pool_mem{l}")
            pool_p.append(tail[:, POOL_HALO - POOL_STATE:])
            (u,) = norm_matmul(xs, (w["norm_mix"], l), [(w["pool_w_in"], l)], [F32], tm=1024, tn=512,
                               name=f"s_pool_in{l}")
            state = pool_state[l]
            pool_s.append(jnp.concatenate([state[:, 1:], u[:, None, :]], axis=1))
            z = pool_sample(jnp.swapaxes(state, 0, 1), u, w["pool_w_group"], w["pool_scale"], l, name=f"s_pool{l}")
            xs = matmul_residual(z, (w["pool_w_out"], l), xs, tm=1024, tn=1024, name=f"s_pool_out{l}")
        else:
            j = l - n_a
            kv = (w["norm_kv"], w["w_k_shared"], w["w_v_shared"]) if j == 0 else None
            q, *kv_out = norm_matmul_resident(xp, (w["norm_mix"], l), (w["dil_w_q"], j), F32, tm=512,
                                              name=f"p_dil_q{j}", kv=kv)
            if kv_out:
                kp, vp, kp_rows, vp_rows = kv_out
            q3, k3, v3 = (a.reshape(batch, seq, -1) for a in (q, kp, vp))
            o = dil_attention_prompt(q3, k3, v3, name=f"p_dil_attn{j}")
            xp = heads_out_mem(o, (w["dil_w_o"], j), xp, *mem_args, seq=seq, tm=512, name=f"p_dil_out_mem{j}")
            (q,) = norm_matmul(xs, (w["norm_mix"], l), [(w["dil_w_q"], j)], [F32], tm=1024, tn=512,
                               name=f"s_dil_q{j}")
            o = dil_attention_sample(q, ks, vs, win_k, win_v, name=f"s_dil_attn{j}")
            xs = matmul_residual(o, (w["dil_w_o"], j), xs, tm=1024, tn=1024, name=f"s_dil_out{j}")
        xs = mem_block_sample(xs, (w["norm_mem_q"], l), w["mem_w_q"], *mem_s, w["mem_w_o"], l, name=f"s_mem{l}")
        xs_pad = jnp.pad(xs, ((0, EXTRA_ROWS - dec_batch), (0, 0)))
        xp, xs_pad = ffn(xp, xs_pad, (w["norm_ffn"], l), w["ffn_w_gate"], w["ffn_w_up"], w["ffn_w_down"], l,
                         tm=1024, tf=256, name=f"ffn{l}", final_gain=w["norm_final"] if l == depth - 1 else None)
        xs = xs_pad[:dec_batch]
    return (xp, jnp.stack(pool_p), kp_rows, vp_rows), (xs, jnp.stack(pool_s), ks, vs)


def kernel(x_prompt, x_sample, mem_prompt, state_pool, cache_win_k, cache_win_v, cache_mem_k, cache_mem_v,
           norm_mix, norm_mem_q, norm_mem_kv, norm_ffn, pool_w_in, pool_w_group, pool_scale, pool_w_out,
           norm_kv, w_k_shared, w_v_shared, dil_w_q, dil_w_o, mem_w_q, mem_w_k, mem_w_v, mem_w_o,
           ffn_w_gate, ffn_w_up, ffn_w_down, norm_final):
    batch, seq, d = x_prompt.shape
    dec_batch, dec_seq, _ = x_sample.shape
    depth = norm_mix.shape[0]
    n_mem = mem_prompt.shape[1]
    rows3 = lambda a: a.reshape(a.shape[0], 1, a.shape[1])
    w = dict(norm_mix=rows3(norm_mix), norm_mem_q=rows3(norm_mem_q), norm_ffn=rows3(norm_ffn),
             norm_kv=norm_kv.reshape(1, 1, d), norm_final=norm_final, pool_scale=rows3(pool_scale),
             pool_w_in=pool_w_in.astype(BF16), pool_w_group=pool_w_group.astype(BF16),
             pool_w_out=pool_w_out.astype(BF16), dil_w_q=dil_w_q.astype(BF16), dil_w_o=dil_w_o.astype(BF16),
             mem_w_q=mem_w_q.astype(BF16), mem_w_o=mem_w_o.astype(BF16),
             w_k_shared=w_k_shared[None], w_v_shared=w_v_shared[None],
             ffn_w_gate=ffn_w_gate, ffn_w_up=ffn_w_up, ffn_w_down=ffn_w_down)

    mk_p, mv_p, mk_rows, mv_rows = shared_kv(mem_prompt.reshape(batch * n_mem, d), rows3(norm_mem_kv),
                                             mem_w_k, mem_w_v, tm=512, name="mem_kv")
    mk_p = mk_p.reshape(depth, batch, n_mem, -1)
    mv_p = mv_p.reshape(depth, batch, n_mem, -1)

    assert dec_seq == 1
    mk_s = cache_mem_k.reshape(depth, dec_batch, n_mem * MEM_HEADS, MEM_HEAD_DIM)
    mv_s = cache_mem_v.reshape(depth, dec_batch, n_mem * MEM_HEADS, MEM_HEAD_DIM)
    (y_p, pool_p, k_p, v_p), (y_s, pool_s, k_s, v_s) = _trunks(
        x_prompt.reshape(batch * seq, d), x_sample.reshape(dec_batch, d), (mk_p, mv_p), (mk_s, mv_s), w,
        batch=batch, seq=seq, pool_state=state_pool, win_k=cache_win_k, win_v=cache_win_v)

    max_window = max(wd for wd, _ in DIL_PATTERNS)
    keep_from = max(0, seq - max_window)
    kv_shape = (N_KV_HEADS, HEAD_DIM)
    mem_shape = (depth, batch, n_mem, MEM_HEADS, MEM_HEAD_DIM)
    return (y_p.reshape(batch, seq, d), y_s.reshape(dec_batch, dec_seq, d), pool_p, pool_s,
            k_p.reshape(batch, seq, *kv_shape)[:, keep_from:], v_p.reshape(batch, seq, *kv_shape)[:, keep_from:],
            k_s.reshape(dec_batch, dec_seq, *kv_shape), v_s.reshape(dec_batch, dec_seq, *kv_shape),
            mk_rows.reshape(mem_shape), mv_rows.reshape(mem_shape))
```

```python
import functools

import jax
import jax.numpy as jnp
from jax import lax
from jax.experimental import pallas as pl
from jax.experimental.pallas import tpu as pltpu

F32 = jnp.float32
BF16 = jnp.bfloat16

RMS_EPS = 1e-6
POOL_WINDOWS = (2, 4, 8, 16)
POOL_STATE = max(POOL_WINDOWS) - 1
POOL_HALO = 16
POOL_CHUNK = 64
POOL_TM = 512
POOL_PART = 256
EXTRA_ROWS = 8
DIL_PATTERNS = ((128, 1), (512, 4), (2048, 16))
DIL_BLOCK = 128
DIL_GROUP = 8
MERGE_ROWS = 256
MEM_ROWS = 256
HEAD_DIM = 128
N_KV_HEADS = 4
HEADS_PER_GROUP = 8
KV_REP = HEADS_PER_GROUP // N_KV_HEADS
MEM_HEADS = 4
MEM_HEAD_DIM = 128
NEG = -1e30
VMEM_LIMIT = 56 * 1024 * 1024

NT_DIMS = (((1,), (1,)), ((), ()))


def _params(*sem):
    return pltpu.CompilerParams(dimension_semantics=sem, vmem_limit_bytes=VMEM_LIMIT)


def _layer_spec(layer, block, index_map):
    return pl.BlockSpec((None,) + tuple(block), lambda *g: (layer,) + tuple(index_map(*g)))


def _resident_spec(layer, block):
    zeros = (0,) * len(block)
    return pl.BlockSpec((None,) + tuple(block), lambda *g: (layer,) + zeros, pipeline_mode=pl.Buffered(1))


def _rms(x, g):
    ms = jnp.mean(x * x, axis=-1, keepdims=True)
    return x * lax.rsqrt(ms + RMS_EPS) * g


def _lhs_scratch_dtype(rows):
    return BF16 if rows % 16 == 0 else F32


def _softmax_rows(s):
    m = jnp.max(s, axis=-1, keepdims=True)
    e = jnp.exp(s - m)
    den = jnp.sum(e, axis=-1, keepdims=True)
    return e / den, m, den


def _norm_matmul_kernel(x_ref, g_ref, *refs, n_w):
    w_refs, o_refs, h_ref = refs[:n_w], refs[n_w:2 * n_w], refs[2 * n_w]

    @pl.when(pl.program_id(1) == 0)
    def _():
        h_ref[...] = _rms(x_ref[...], g_ref[...]).astype(h_ref.dtype)

    h = h_ref[...].astype(BF16)
    for w_ref, o_ref in zip(w_refs, o_refs):
        o_ref[...] = jnp.dot(h, w_ref[...].astype(BF16), preferred_element_type=F32).astype(o_ref.dtype)


def norm_matmul(x, g, ws, out_dtypes, *, tm, tn, name):
    m, d = x.shape
    gains, lg = g
    n = ws[0][0].shape[2]
    tm, tn = min(tm, m), min(tn, n)
    n_w = len(ws)
    return pl.pallas_call(
        functools.partial(_norm_matmul_kernel, n_w=n_w),
        grid=(m // tm, n // tn),
        in_specs=[pl.BlockSpec((tm, d), lambda i, j: (i, 0)),
                  _layer_spec(lg, (1, d), lambda i, j: (0, 0))]
                 + [_layer_spec(lw, (d, tn), lambda i, j: (0, j)) for _, lw in ws],
        out_specs=[pl.BlockSpec((tm, tn), lambda i, j: (i, j))] * n_w,
        out_shape=[jax.ShapeDtypeStruct((m, n), dt) for dt in out_dtypes],
        scratch_shapes=[pltpu.VMEM((tm, d), _lhs_scratch_dtype(tm))],
        compiler_params=_params("parallel", "arbitrary"),
        name=name,
    )(x, gains, *[w for w, _ in ws])


def _emit_kv(h, wk_ref, wv_ref, k_ref, v_ref, kh_ref, vh_ref):
    tm = h.shape[0]
    for w_ref, o_ref, oh_ref in ((wk_ref, k_ref, kh_ref), (wv_ref, v_ref, vh_ref)):
        r = jnp.dot(h, w_ref[...].astype(BF16), preferred_element_type=F32)
        o_ref[...] = r
        for hd in range(N_KV_HEADS):
            oh_ref[pl.ds(hd, tm, stride=N_KV_HEADS), :] = r[:, hd * HEAD_DIM:(hd + 1) * HEAD_DIM]


def _shared_kv_kernel(x_ref, g_ref, wk_ref, wv_ref, k_ref, v_ref, kh_ref, vh_ref):
    _emit_kv(_rms(x_ref[...], g_ref[...]).astype(BF16), wk_ref, wv_ref, k_ref, v_ref, kh_ref, vh_ref)


def shared_kv(x, gains, w_k, w_v, *, tm, name):
    m, d = x.shape
    n_layers, _, n = w_k.shape
    layer_rows = lambda l, i: (l, i, 0)
    per_layer = lambda l, i: (l, 0, 0)
    return pl.pallas_call(
        _shared_kv_kernel,
        grid=(n_layers, m // tm),
        in_specs=[pl.BlockSpec((tm, d), lambda l, i: (i, 0)), pl.BlockSpec((None, 1, d), per_layer),
                  pl.BlockSpec((None, d, n), per_layer), pl.BlockSpec((None, d, n), per_layer)],
        out_specs=[pl.BlockSpec((None, tm, n), layer_rows)] * 2
                  + [pl.BlockSpec((None, tm * N_KV_HEADS, HEAD_DIM), layer_rows)] * 2,
        out_shape=[jax.ShapeDtypeStruct((n_layers, m, n), F32)] * 2
                  + [jax.ShapeDtypeStruct((n_layers, m * N_KV_HEADS, HEAD_DIM), F32)] * 2,
        compiler_params=_params("parallel", "parallel"),
        name=name,
    )(x, gains, w_k, w_v)


def _norm_matmul_resident_kernel(x_ref, g_ref, w_ref, *rest, with_kv):
    x = x_ref[...]
    xn = x * lax.rsqrt(jnp.mean(x * x, axis=-1, keepdims=True) + RMS_EPS)
    o_ref = rest[3] if with_kv else rest[0]
    o_ref[...] = jnp.dot((xn * g_ref[...]).astype(BF16), w_ref[...], preferred_element_type=F32).astype(o_ref.dtype)
    if with_kv:
        gkv_ref, wk_ref, wv_ref = rest[:3]
        _emit_kv((xn * gkv_ref[...]).astype(BF16), wk_ref, wv_ref, *rest[4:])


def norm_matmul_resident(x, g, w, out_dtype, *, tm, name, kv=None):
    m, d = x.shape
    gains, lg = g
    w, lw = w
    n = w.shape[2]
    row = lambda i: (i, 0)
    in_specs = [pl.BlockSpec((tm, d), row), _layer_spec(lg, (1, d), lambda i: (0, 0)), _resident_spec(lw, (d, n))]
    out_specs = [pl.BlockSpec((tm, n), row)]
    out_shape = [jax.ShapeDtypeStruct((m, n), out_dtype)]
    args = [x, gains, w]
    if kv is not None:
        gains_kv, w_k, w_v = kv
        nk = w_k.shape[2]
        in_specs += [_layer_spec(0, (1, d), lambda i: (0, 0)), _resident_spec(0, (d, nk)), _resident_spec(0, (d, nk))]
        out_specs += [pl.BlockSpec((tm, nk), row)] * 2 + [pl.BlockSpec((tm * N_KV_HEADS, HEAD_DIM), row)] * 2
        out_shape += [jax.ShapeDtypeStruct((m, nk), F32)] * 2 + [jax.ShapeDtypeStruct((m * N_KV_HEADS, HEAD_DIM), F32)] * 2
        args += [gains_kv, w_k, w_v]
    return pl.pallas_call(
        functools.partial(_norm_matmul_resident_kernel, with_kv=kv is not None),
        grid=(m // tm,),
        in_specs=in_specs,
        out_specs=out_specs,
        out_shape=out_shape,
        compiler_params=_params("parallel"),
        name=name,
    )(*args)


def _matmul_res_kernel(a_ref, w_ref, x_ref, o_ref):
    a = a_ref[...].astype(BF16)
    o_ref[...] = x_ref[...] + jnp.dot(a, w_ref[...].astype(BF16), preferred_element_type=F32)


def matmul_residual(a, w, x, *, tm, tn, name):
    m, k = a.shape
    w, lw = w
    n = w.shape[2]
    tm, tn = min(tm, m), min(tn, n)
    return pl.pallas_call(
        _matmul_res_kernel,
        grid=(m // tm, n // tn),
        in_specs=[pl.BlockSpec((tm, k), lambda i, j: (i, 0)),
                  _layer_spec(lw, (k, tn), lambda i, j: (0, j)),
                  pl.BlockSpec((tm, tn), lambda i, j: (i, j))],
        out_specs=pl.BlockSpec((tm, tn), lambda i, j: (i, j)),
        out_shape=jax.ShapeDtypeStruct((m, n), F32),
        compiler_params=_params("parallel", "parallel"),
        name=name,
    )(a, w, x)


def _ffn_kernel(x_ref, xs_ref, g_ref, wg_ref, wu_ref, wd_ref, *rest, final_norm):
    if final_norm:
        gf_ref, o_ref, os_ref, h_ref = rest
    else:
        o_ref, os_ref, h_ref = rest
    i, j = pl.program_id(0), pl.program_id(1)
    tm = x_ref.shape[0]
    first_tile = i == 0

    @pl.when(j == 0)
    def _():
        x = x_ref[...]
        h_ref[0:tm, :] = _rms(x, g_ref[...]).astype(BF16)
        o_ref[...] = x

    @pl.when((j == 0) & first_tile)
    def _():
        xs = xs_ref[...]
        h_ref[tm:, :] = _rms(xs, g_ref[...]).astype(BF16)
        os_ref[...] = xs

    h = h_ref[...]
    gate = jnp.dot(h, wg_ref[...].astype(BF16), preferred_element_type=F32)
    up = jnp.dot(h, wu_ref[...].astype(BF16), preferred_element_type=F32)
    act = (gate * jax.nn.sigmoid(gate) * up).astype(BF16)
    wd = wd_ref[...].astype(BF16)
    o_ref[...] += jnp.dot(act[:tm], wd, preferred_element_type=F32)

    @pl.when(first_tile)
    def _():
        os_ref[...] += jnp.dot(act[tm:], wd, preferred_element_type=F32)

    if final_norm:
        last = j == pl.num_programs(1) - 1

        @pl.when(last)
        def _():
            o_ref[...] = _rms(o_ref[...], gf_ref[...])

        @pl.when(last & first_tile)
        def _():
            os_ref[...] = _rms(os_ref[...], gf_ref[...])


def ffn(x, xs, g, w_gate, w_up, w_down, layer, *, tm, tf, name, final_gain=None):
    m, d = x.shape
    rows_s = xs.shape[0]
    assert rows_s == EXTRA_ROWS and m % tm == 0
    gains, lg = g
    f = w_gate.shape[2]
    final_norm = final_gain is not None
    extra_specs = [pl.BlockSpec((1, d), lambda i, j: (0, 0))] if final_norm else []
    extra_args = [final_gain.reshape(1, d)] if final_norm else []
    return pl.pallas_call(
        functools.partial(_ffn_kernel, final_norm=final_norm),
        grid=(m // tm, f // tf),
        in_specs=[pl.BlockSpec((tm, d), lambda i, j: (i, 0)),
                  pl.BlockSpec((rows_s, d), lambda i, j: (0, 0)),
                  _layer_spec(lg, (1, d), lambda i, j: (0, 0)),
                  _layer_spec(layer, (d, tf), lambda i, j: (0, j)),
                  _layer_spec(layer, (d, tf), lambda i, j: (0, j)),
                  _layer_spec(layer, (tf, d), lambda i, j: (j, 0))] + extra_specs,
        out_specs=[pl.BlockSpec((tm, d), lambda i, j: (i, 0)), pl.BlockSpec((rows_s, d), lambda i, j: (0, 0))],
        out_shape=[jax.ShapeDtypeStruct((m, d), F32), jax.ShapeDtypeStruct((rows_s, d), F32)],
        scratch_shapes=[pltpu.VMEM((tm + rows_s, d), BF16)],
        compiler_params=_params("arbitrary", "arbitrary"),
        name=name,
    )(x, xs, gains, w_gate, w_up, w_down, *extra_args)


def _pool_block_kernel(x_ref, g_ref, win_ref, wg_ref, scale_ref, wout_ref, gm_ref, wq_ref, k_ref, v_ref, wo_ref,
                       o_ref, tail_ref, ext_ref, p_ref, *, tm, tiles_per_seq):
    d = x_ref.shape[1]
    gdim = d // len(POOL_WINDOWS)
    t_in_seq = pl.program_id(0) % tiles_per_seq

    @pl.when(t_in_seq == 0)
    def _():
        ext_ref[0:POOL_HALO, :] = jnp.zeros((POOL_HALO, d), F32)

    @pl.when(t_in_seq != 0)
    def _():
        ext_ref[0:POOL_HALO, :] = ext_ref[tm:tm + POOL_HALO, :]

    parts = [slice(c * POOL_PART, (c + 1) * POOL_PART) for c in range(tm // POOL_PART)]
    us = [jnp.dot(_rms(x_ref[rows, :], g_ref[...]).astype(BF16), win_ref[...], preferred_element_type=F32)
          for rows in parts]
    for rows, u in zip(parts, us):
        ext_ref[POOL_HALO + rows.start:POOL_HALO + rows.stop, :] = u
    tail_ref[...] = us[-1][POOL_PART - POOL_HALO:, :]
    ys = []
    for rows in parts:
        for c in range(rows.start // POOL_CHUNK, rows.stop // POOL_CHUNK):
            r0 = POOL_HALO + c * POOL_CHUNK
            pos = t_in_seq * tm + c * POOL_CHUNK + lax.broadcasted_iota(jnp.int32, (POOL_CHUNK, 1), 0)
            for gi, w in enumerate(POOL_WINDOWS):
                cols = slice(gi * gdim, (gi + 1) * gdim)
                u_q = ext_ref[r0:r0 + POOL_CHUNK, cols]
                acc = u_q
                for k in range(1, w):
                    acc = acc + ext_ref[r0 - k:r0 - k + POOL_CHUNK, cols]
                inv_cnt = 1.0 / jnp.minimum(pos + 1, w).astype(F32)
                p_ref[c * POOL_CHUNK:(c + 1) * POOL_CHUNK, cols] = (acc * inv_cnt - u_q).astype(BF16)
        zs = []
        for gi in range(len(POOL_WINDOWS)):
            cols = slice(gi * gdim, (gi + 1) * gdim)
            z = jnp.dot(p_ref[rows, cols], wg_ref[gi], preferred_element_type=F32)
            zs.append((z * scale_ref[:, cols]).astype(BF16))
        ys.append(x_ref[rows, :] + jnp.dot(jnp.concatenate(zs, axis=1), wout_ref[...],
                                            preferred_element_type=F32))
    for rows, y in zip(parts, _mem_rows(ys, gm_ref, wq_ref, k_ref, v_ref, wo_ref)):
        o_ref[rows, :] = y


def pool_mem_block(x, g, w_in, w_group, scale, w_out, g_mem, w_q, mk, mv, w_o, layer, *, seq, tm, name):
    m, d = x.shape
    gains, lg = g
    gains_mem, lgm = g_mem
    n_mem, dm = mk.shape[2], mk.shape[3]
    tiles_per_seq = seq // tm
    return pl.pallas_call(
        functools.partial(_pool_block_kernel, tm=tm, tiles_per_seq=tiles_per_seq),
        grid=(m // tm,),
        in_specs=[pl.BlockSpec((tm, d), lambda i: (i, 0)),
                  _layer_spec(lg, (1, d), lambda i: (0, 0)),
                  _resident_spec(layer, (d, d)),
                  _resident_spec(layer, w_group.shape[1:]),
                  _layer_spec(layer, (1, d), lambda i: (0, 0)),
                  _resident_spec(layer, (d, d))]
                 + _mem_specs(layer, lgm, d, dm, n_mem, lambda i: i // tiles_per_seq),
        out_specs=[pl.BlockSpec((tm, d), lambda i: (i, 0)),
                   pl.BlockSpec((None, POOL_HALO, d), lambda i: (i // tiles_per_seq, 0, 0))],
        out_shape=[jax.ShapeDtypeStruct((m, d), F32), jax.ShapeDtypeStruct((m // seq, POOL_HALO, d), F32)],
        scratch_shapes=[pltpu.VMEM((tm + POOL_HALO, d), F32), pltpu.VMEM((tm, d), BF16)],
        compiler_params=_params("arbitrary"),
        name=name,
    )(x, gains, w_in, w_group, scale, w_out, gains_mem, w_q, mk, mv, w_o)


def _pool_sample_kernel(state_ref, u_ref, wg_ref, scale_ref, z_ref):
    n_state, _, d = state_ref.shape
    gdim = d // len(POOL_WINDOWS)
    u = u_ref[...]
    for gi, w in enumerate(POOL_WINDOWS):
        cols = slice(gi * gdim, (gi + 1) * gdim)
        u_q = u[:, cols]
        acc = u_q
        for k in range(1, w):
            acc = acc + state_ref[n_state - k, :, cols]
        cnt = float(min(n_state + 1, w))
        p = (acc / cnt - u_q).astype(BF16)
        z = jnp.dot(p, wg_ref[gi].astype(BF16), preferred_element_type=F32)
        z_ref[:, cols] = z * scale_ref[:, cols]


def pool_sample(state_t, u, w_group, scale, layer, *, name):
    b, d = u.shape
    return pl.pallas_call(
        _pool_sample_kernel,
        grid=(1,),
        in_specs=[pl.BlockSpec(state_t.shape, lambda i: (0, 0, 0)),
                  pl.BlockSpec((b, d), lambda i: (0, 0)),
                  _layer_spec(layer, w_group.shape[1:], lambda i: (0, 0, 0)),
                  _layer_spec(layer, (1, d), lambda i: (0, 0))],
        out_specs=pl.BlockSpec((b, d), lambda i: (0, 0)),
        out_shape=jax.ShapeDtypeStruct((b, d), F32),
        compiler_params=_params("arbitrary"),
        name=name,
    )(state_t, u, w_group, scale)


def _head_cols(hd):
    return slice(hd * MEM_HEAD_DIM, (hd + 1) * MEM_HEAD_DIM)


def _mem_rows(xs, g_ref, wq_ref, k_ref, v_ref, wo_ref):
    wq = wq_ref[...].astype(BF16)
    kb = k_ref[...].astype(BF16)
    vb = v_ref[...].astype(BF16)
    scale = MEM_HEAD_DIM ** -0.5
    hs = [_rms(x, g_ref[...]).astype(BF16) for x in xs]
    qs = [jnp.dot(h, wq, preferred_element_type=F32).astype(BF16) for h in hs]
    ss = [[lax.dot_general(q[:, _head_cols(hd)], kb[:, _head_cols(hd)], NT_DIMS, preferred_element_type=F32) * scale
           for hd in range(MEM_HEADS)] for q in qs]
    ps = [[_softmax_rows(s)[0].astype(BF16) for s in row] for row in ss]
    os_ = [jnp.concatenate([jnp.dot(p, vb[:, _head_cols(hd)], preferred_element_type=F32).astype(BF16)
                            for hd, p in enumerate(row)], axis=1) for row in ps]
    wo = wo_ref[...].astype(BF16)
    return [x + jnp.dot(o, wo, preferred_element_type=F32) for x, o in zip(xs, os_)]


def _mem_specs(layer, gains_layer, d, dm, n_mem, seq_of_step):
    kv_spec = pl.BlockSpec((None, None, n_mem, dm), lambda i: (layer, seq_of_step(i), 0, 0))
    return [_layer_spec(gains_layer, (1, d), lambda i: (0, 0)), _resident_spec(layer, (d, dm)),
            kv_spec, kv_spec, _resident_spec(layer, (dm, d))]


def _mem_sample_kernel(x_ref, g_ref, wq_ref, k_ref, v_ref, wo_ref, o_ref):
    n_b = x_ref.shape[0]
    n_mem = k_ref.shape[1] // MEM_HEADS
    head_rows = lambda ref, b, hd: ref[b, pl.ds(hd, n_mem, stride=MEM_HEADS), :].astype(BF16)
    x = x_ref[...]
    h = _rms(x, g_ref[...]).astype(BF16)
    q = jnp.dot(h, wq_ref[...].astype(BF16), preferred_element_type=F32).astype(BF16)
    scale = MEM_HEAD_DIM ** -0.5
    row = lax.broadcasted_iota(jnp.int32, (n_b, 1), 0)
    pairs = [(b, hd) for b in range(n_b) for hd in range(MEM_HEADS)]
    ss = [lax.dot_general(q[:, _head_cols(hd)], head_rows(k_ref, b, hd), NT_DIMS,
                          preferred_element_type=F32) * scale for b, hd in pairs]
    ps = [_softmax_rows(s)[0].astype(BF16) for s in ss]
    outs = [jnp.zeros((n_b, MEM_HEAD_DIM), F32) for _ in range(MEM_HEADS)]
    for (b, hd), p in zip(pairs, ps):
        o_b = jnp.dot(p, head_rows(v_ref, b, hd), preferred_element_type=F32)
        outs[hd] = jnp.where(row == b, o_b, outs[hd])
    o = jnp.concatenate(outs, axis=1).astype(BF16)
    o_ref[...] = x + jnp.dot(o, wo_ref[...].astype(BF16), preferred_element_type=F32)


def mem_block_sample(x, g, w_q, mk, mv, w_o, layer, *, name):
    n_b, d = x.shape
    gains, lg = g
    dm = w_q.shape[2]
    kv_spec = _layer_spec(layer, mk.shape[1:], lambda i: (0, 0, 0))
    x_spec = pl.BlockSpec((n_b, d), lambda i: (0, 0))
    return pl.pallas_call(
        _mem_sample_kernel,
        grid=(1,),
        in_specs=[x_spec,
                  _layer_spec(lg, (1, d), lambda i: (0, 0)),
                  _layer_spec(layer, (d, dm), lambda i: (0, 0)),
                  kv_spec, kv_spec,
                  _layer_spec(layer, (dm, d), lambda i: (0, 0))],
        out_specs=x_spec,
        out_shape=jax.ShapeDtypeStruct((n_b, d), F32),
        compiler_params=_params("arbitrary"),
        name=name,
    )(x, gains, w_q, mk, mv, w_o)


def _strided_rows(ref, start, dil):
    if dil == 1:
        return ref[pl.ds(start, DIL_BLOCK), :]
    return ref[pl.ds(start, DIL_BLOCK, stride=dil), :]


def _dil_group_attention(q_refs, k_ref, v_ref, o_ref, lse_ref, *, dil, seq):
    span = dil * DIL_BLOCK
    nb = seq // span
    qi = lax.broadcasted_iota(jnp.int32, (DIL_BLOCK, 2 * DIL_BLOCK), 0)
    kj = lax.broadcasted_iota(jnp.int32, (DIL_BLOCK, 2 * DIL_BLOCK), 1)
    dist = qi + DIL_BLOCK - kj
    in_window = (dist >= 0) & (dist <= DIL_BLOCK)
    causal = (lax.broadcasted_iota(jnp.int32, (DIL_BLOCK, DIL_BLOCK), 0)
              >= lax.broadcasted_iota(jnp.int32, (DIL_BLOCK, DIL_BLOCK), 1))
    scale = HEAD_DIM ** -0.5

    def store_rows(ref, lead, start, val):
        rows = pl.ds(start, DIL_BLOCK) if dil == 1 else pl.ds(start, DIL_BLOCK, stride=dil)
        ref[lead + (rows, slice(None))] = val

    masks = {False: jnp.concatenate([in_window] * KV_REP, axis=0),
             True: jnp.concatenate([in_window & (kj >= DIL_BLOCK)] * KV_REP, axis=0),
             None: jnp.concatenate([causal] * KV_REP, axis=0)}

    def scores(jg, i):
        if nb == 1:
            b, start, first = 0, jg * DIL_GROUP + i, None
        elif nb <= DIL_GROUP:
            r, b = jg * (DIL_GROUP // nb) + i // nb, i % nb
            start, first = b * span + r, b == 0
        else:
            r, b = jg // (nb // DIL_GROUP), (jg % (nb // DIL_GROUP)) * DIL_GROUP + i
            start, first = b * span + r, (b == 0 if i == 0 else False)
        if dil == 1:
            start = pl.multiple_of(start, DIL_BLOCK)
        q2 = jnp.concatenate([_strided_rows(q, start, dil) for q in q_refs], axis=0).astype(BF16)
        k_cur = _strided_rows(k_ref, start, dil)
        v_cur = _strided_rows(v_ref, start, dil)
        if nb > 1:
            prev = jnp.maximum(b - 1, 0) * span + (start - b * span)
            if dil == 1:
                prev = pl.multiple_of(prev, DIL_BLOCK)
            kcat = jnp.concatenate([_strided_rows(k_ref, prev, dil), k_cur], axis=0).astype(BF16)
            vcat = jnp.concatenate([_strided_rows(v_ref, prev, dil), v_cur], axis=0).astype(BF16)
        else:
            kcat, vcat = k_cur.astype(BF16), v_cur.astype(BF16)
        s = lax.dot_general(q2, kcat, NT_DIMS, preferred_element_type=F32) * scale
        if isinstance(first, (bool, type(None))):
            valid = masks[first]
        else:
            valid = masks[False] & (jnp.concatenate([kj] * KV_REP, axis=0) >= jnp.where(first, DIL_BLOCK, 0))
        return start, jnp.where(valid, s, NEG), vcat

    def block_group(jg, carry):
        staged = [scores(jg, i) for i in range(DIL_GROUP)]
        probs = [_softmax_rows(s) for _, s, _ in staged]
        outs = [jnp.dot(p.astype(BF16), vcat, preferred_element_type=F32)
                for (p, _, _), (_, _, vcat) in zip(probs, staged)]
        for (start, _, _), (_, m, den), o2 in zip(staged, probs, outs):
            lse2 = m + jnp.log(den)
            for hh in range(KV_REP):
                rows = slice(hh * DIL_BLOCK, (hh + 1) * DIL_BLOCK)
                store_rows(o_ref, (hh,), start, o2[rows])
                store_rows(lse_ref, (hh,), start, jnp.broadcast_to(lse2[rows], (DIL_BLOCK, HEAD_DIM)))
        return carry

    for jg in range(seq // (DIL_BLOCK * DIL_GROUP)):
        block_group(jg, 0)


def _dil_prompt_kernel(*refs, seq):
    n_g = len(DIL_PATTERNS)
    q_refs = refs[:n_g * KV_REP]
    k_ref, v_ref, out_ref, o_scr, lse_scr = refs[n_g * KV_REP:]
    for gi, (_, dil) in enumerate(DIL_PATTERNS):
        _dil_group_attention(q_refs[gi * KV_REP:(gi + 1) * KV_REP], k_ref, v_ref, o_scr.at[gi], lse_scr.at[gi],
                             dil=dil, seq=seq)

    def merge(c, carry):
        rows = pl.ds(pl.multiple_of(c * MERGE_ROWS, MERGE_ROWS), MERGE_ROWS)
        for hh in range(KV_REP):
            lses = [lse_scr[gi, hh, rows, :] for gi in range(n_g)]
            m = functools.reduce(jnp.maximum, lses)
            es = [jnp.exp(l - m) for l in lses]
            den = functools.reduce(jnp.add, es)
            acc = sum((e / den) * o_scr[gi, hh, rows, :] for gi, e in enumerate(es))
            out_ref[hh, rows, :] = acc.astype(out_ref.dtype)
        return carry

    lax.fori_loop(0, seq // MERGE_ROWS, merge, 0)


def dil_attention_prompt(q, k, v, *, name):
    n, s, _ = q.shape
    n_g = len(DIL_PATTERNS)
    q_map = lambda head: (lambda b, j: (b, 0, head + KV_REP * j))
    q_specs = [pl.BlockSpec((None, s, HEAD_DIM), q_map(gi * HEADS_PER_GROUP + hh))
               for gi in range(n_g) for hh in range(KV_REP)]
    head_block = pl.BlockSpec((None, s, HEAD_DIM), lambda b, j: (b, 0, j))
    return pl.pallas_call(
        functools.partial(_dil_prompt_kernel, seq=s),
        grid=(n, N_KV_HEADS),
        in_specs=q_specs + [head_block, head_block],
        out_specs=pl.BlockSpec((None, KV_REP, s, HEAD_DIM), lambda b, j: (b, j, 0, 0)),
        out_shape=jax.ShapeDtypeStruct((n, HEADS_PER_GROUP, s, HEAD_DIM), BF16),
        scratch_shapes=[pltpu.VMEM((n_g, KV_REP, s, HEAD_DIM), F32), pltpu.VMEM((n_g, KV_REP, s, HEAD_DIM), F32)],
        compiler_params=_params("parallel", "parallel"),
        name=name,
    )(*([q] * (n_g * KV_REP)), k, v)


def _heads_out_mem_kernel(o_ref, w_ref, x_ref, g_ref, wq_ref, k_ref, v_ref, wo_ref, out_ref):
    tm = x_ref.shape[0]
    parts = [slice(c * MEM_ROWS, (c + 1) * MEM_ROWS) for c in range(tm // MEM_ROWS)]
    w = w_ref[...]
    xs = [x_ref[rows, :] + jnp.dot(jnp.concatenate([o_ref[h, rows, :] for h in range(o_ref.shape[0])], axis=1), w,
                                   preferred_element_type=F32) for rows in parts]
    for rows, y in zip(parts, _mem_rows(xs, g_ref, wq_ref, k_ref, v_ref, wo_ref)):
        out_ref[rows, :] = y


def heads_out_mem(o, w, x, g, w_q, mk, mv, w_o, layer, *, seq, tm, name):
    m, d = x.shape
    w, lw = w
    gains, lg = g
    n_mem, dm = mk.shape[2], mk.shape[3]
    tps = seq // tm
    return pl.pallas_call(
        _heads_out_mem_kernel,
        grid=(m // tm,),
        in_specs=[pl.BlockSpec((None,) + o.shape[1:2] + (tm, o.shape[3]), lambda i: (i // tps, 0, i % tps, 0)),
                  _resident_spec(lw, w.shape[1:]),
                  pl.BlockSpec((tm, d), lambda i: (i, 0))]
                 + _mem_specs(layer, lg, d, dm, n_mem, lambda i: i // tps),
        out_specs=pl.BlockSpec((tm, d), lambda i: (i, 0)),
        out_shape=jax.ShapeDtypeStruct((m, d), F32),
        compiler_params=_params("parallel"),
        name=name,
    )(o, w, x, gains, w_q, mk, mv, w_o)


def _dil_sample_kernel(q_ref, kn_ref, vn_ref, *refs):
    n_g = len(DIL_PATTERNS)
    k_refs, v_refs, o_ref = refs[:n_g], refs[n_g:2 * n_g], refs[2 * n_g]

    def head_rows(ref, kv):
        if len(ref.shape) == 2:
            return _strided_rows(ref, kv, N_KV_HEADS).astype(BF16)
        return ref[:, kv, :].astype(BF16)

    scale = HEAD_DIM ** -0.5
    head_row = lax.broadcasted_iota(jnp.int32, (HEADS_PER_GROUP, 1), 0)
    kn = kn_ref[...].astype(BF16).astype(F32)
    vn = vn_ref[...].astype(BF16).astype(F32)
    outs, lses = [], []
    for gi, (window, dil) in enumerate(DIL_PATTERNS):
        qg = q_ref[gi * HEADS_PER_GROUP:(gi + 1) * HEADS_PER_GROUP, :].astype(BF16)
        qf = qg.astype(F32)
        s = jnp.zeros((HEADS_PER_GROUP, DIL_BLOCK), F32)
        s_new = jnp.zeros((HEADS_PER_GROUP, 1), F32)
        for kv in range(N_KV_HEADS):
            mine = (head_row >= kv * KV_REP) & (head_row < (kv + 1) * KV_REP)
            kc = head_rows(k_refs[gi], kv)
            s = jnp.where(mine, lax.dot_general(qg, kc, NT_DIMS, preferred_element_type=F32), s)
            s_new = jnp.where(mine, jnp.sum(qf * kn[kv:kv + 1, :], axis=-1, keepdims=True), s_new)
        s = s * scale
        s_new = s_new * scale
        m = jnp.maximum(jnp.max(s, axis=-1, keepdims=True), s_new)
        e = jnp.exp(s - m)
        e_new = jnp.exp(s_new - m)
        den = jnp.sum(e, axis=-1, keepdims=True) + e_new
        p = (e / den).astype(BF16)
        p_new = (e_new / den).astype(BF16).astype(F32)
        o = jnp.zeros((HEADS_PER_GROUP, HEAD_DIM), F32)
        for kv in range(N_KV_HEADS):
            mine = (head_row >= kv * KV_REP) & (head_row < (kv + 1) * KV_REP)
            vc = head_rows(v_refs[gi], kv)
            o_kv = jnp.dot(p, vc, preferred_element_type=F32) + p_new * vn[kv:kv + 1, :]
            o = jnp.where(mine, o_kv, o)
        outs.append(o)
        lses.append(m + jnp.log(den))
    mm = jnp.maximum(jnp.maximum(lses[0], lses[1]), lses[2])
    es = [jnp.exp(l - mm) for l in lses]
    den = es[0] + es[1] + es[2]
    o_ref[...] = (es[0] / den) * outs[0] + (es[1] / den) * outs[1] + (es[2] / den) * outs[2]


def _cache_rows_spec(b, wbuf, window, dil):
    first = wbuf - window
    assert window // dil == DIL_BLOCK and window <= wbuf and wbuf % dil == 0 and first % (dil * DIL_BLOCK) == 0
    rows_per_token = N_KV_HEADS
    if dil * rows_per_token < 8:
        blk = first // DIL_BLOCK
        return ((b, wbuf * rows_per_token, HEAD_DIM),
                pl.BlockSpec((None, DIL_BLOCK * rows_per_token, HEAD_DIM), lambda i: (i, blk, 0)))
    assert (dil * rows_per_token) % 8 == 0
    blk = first // dil // DIL_BLOCK
    return ((b, wbuf // dil, dil * rows_per_token // 8, 8, HEAD_DIM),
            pl.BlockSpec((None, DIL_BLOCK, None, 8, HEAD_DIM), lambda i: (i, blk, 0, 0, 0)))


def dil_attention_sample(q, k_new, v_new, cache_k, cache_v, *, name):
    b, qd = q.shape
    wbuf = cache_k.shape[1]
    n_heads = qd // HEAD_DIM
    small = lambda rows: pl.BlockSpec((None, rows, HEAD_DIM), lambda i: (i, 0, 0))
    views_k, views_v, specs = [], [], []
    for window, dil in DIL_PATTERNS:
        view, spec = _cache_rows_spec(b, wbuf, window, dil)
        views_k.append(cache_k.reshape(view))
        views_v.append(cache_v.reshape(view))
        specs.append(spec)
    o = pl.pallas_call(
        _dil_sample_kernel,
        grid=(b,),
        in_specs=[small(n_heads), small(N_KV_HEADS), small(N_KV_HEADS)] + specs + specs,
        out_specs=small(HEADS_PER_GROUP),
        out_shape=jax.ShapeDtypeStruct((b, HEADS_PER_GROUP, HEAD_DIM), F32),
        compiler_params=_params("parallel"),
        name=name,
    )(q.reshape(b, n_heads, HEAD_DIM), k_new.reshape(b, N_KV_HEADS, HEAD_DIM),
      v_new.reshape(b, N_KV_HEADS, HEAD_DIM), *views_k, *views_v)
    return o.reshape(b, HEADS_PER_GROUP * HEAD_DIM)


def _trunks(xp, xs, mem_p, mem_s, w, *, batch, seq, pool_state, win_k, win_v):
    depth = w["ffn_w_gate"].shape[0]
    n_a = w["pool_w_in"].shape[0]
    d = xp.shape[1]
    dec_batch = xs.shape[0]
    pool_p, pool_s = [], []
    kp = vp = kp_rows = vp_rows = ks = vs = None
    for l in range(depth):
        if l == n_a:
            kv_w = [(w["w_k_shared"], 0), (w["w_v_shared"], 0)]
            ks, vs = norm_matmul(xs, (w["norm_kv"], 0), kv_w, [F32, F32], tm=1024, tn=512, name="s_kv")
        mem_args = ((w["norm_mem_q"], l), w["mem_w_q"], *mem_p, w["mem_w_o"], l)
        if l < n_a:
            xp, tail = pool_mem_block(xp, (w["norm_mix"], l), w["pool_w_in"], w["pool_w_group"], w["pool_scale"],
                                      w["pool_w_out"], *mem_args, seq=seq, tm=POOL_TM, name=f"p_pool_mem{l}")
            pool_p.append(tail[:, POOL_HALO - POOL_STATE:])
            (u,) = norm_matmul(xs, (w["norm_mix"], l), [(w["pool_w_in"], l)], [F32], tm=1024, tn=512,
                               name=f"s_pool_in{l}")
            state = pool_state[l]
            pool_s.append(jnp.concatenate([state[:, 1:], u[:, None, :]], axis=1))
            z = pool_sample(jnp.swapaxes(state, 0, 1), u, w["pool_w_group"], w["pool_scale"], l, name=f"s_pool{l}")
            xs = matmul_residual(z, (w["pool_w_out"], l), xs, tm=1024, tn=1024, name=f"s_pool_out{l}")
        else:
            j = l - n_a
            kv = (w["norm_kv"], w["w_k_shared"], w["w_v_shared"]) if j == 0 else None
            q, *kv_out = norm_matmul_resident(xp, (w["norm_mix"], l), (w["dil_w_q"], j), F32, tm=512,
                                              name=f"p_dil_q{j}", kv=kv)
            if kv_out:
                kp, vp, kp_rows, vp_rows = kv_out
            q3, k3, v3 = (a.reshape(batch, seq, -1) for a in (q, kp, vp))
            o = dil_attention_prompt(q3, k3, v3, name=f"p_dil_attn{j}")
            xp = heads_out_mem(o, (w["dil_w_o"], j), xp, *mem_args, seq=seq, tm=512, name=f"p_dil_out_mem{j}")
            (q,) = norm_matmul(xs, (w["norm_mix"], l), [(w["dil_w_q"], j)], [F32], tm=1024, tn=512,
                               name=f"s_dil_q{j}")
            o = dil_attention_sample(q, ks, vs, win_k, win_v, name=f"s_dil_attn{j}")
            xs = matmul_residual(o, (w["dil_w_o"], j), xs, tm=1024, tn=1024, name=f"s_dil_out{j}")
        xs = mem_block_sample(xs, (w["norm_mem_q"], l), w["mem_w_q"], *mem_s, w["mem_w_o"], l, name=f"s_mem{l}")
        xs_pad = jnp.pad(xs, ((0, EXTRA_ROWS - dec_batch), (0, 0)))
        xp, xs_pad = ffn(xp, xs_pad, (w["norm_ffn"], l), w["ffn_w_gate"], w["ffn_w_up"], w["ffn_w_down"], l,
                         tm=1024, tf=256, name=f"ffn{l}", final_gain=w["norm_final"] if l == depth - 1 else None)
        xs = xs_pad[:dec_batch]
    return (xp, jnp.stack(pool_p), kp_rows, vp_rows), (xs, jnp.stack(pool_s), ks, vs)


def kernel(x_prompt, x_sample, mem_prompt, state_pool, cache_win_k, cache_win_v, cache_mem_k, cache_mem_v,
           norm_mix, norm_mem_q, norm_mem_kv, norm_ffn, pool_w_in, pool_w_group, pool_scale, pool_w_out,
           norm_kv, w_k_shared, w_v_shared, dil_w_q, dil_w_o, mem_w_q, mem_w_k, mem_w_v, mem_w_o,
           ffn_w_gate, ffn_w_up, ffn_w_down, norm_final):
    batch, seq, d = x_prompt.shape
    dec_batch, dec_seq, _ = x_sample.shape
    depth = norm_mix.shape[0]
    n_mem = mem_prompt.shape[1]
    rows3 = lambda a: a.reshape(a.shape[0], 1, a.shape[1])
    w = dict(norm_mix=rows3(norm_mix), norm_mem_q=rows3(norm_mem_q), norm_ffn=rows3(norm_ffn),
             norm_kv=norm_kv.reshape(1, 1, d), norm_final=norm_final, pool_scale=rows3(pool_scale),
             pool_w_in=pool_w_in.astype(BF16), pool_w_group=pool_w_group.astype(BF16),
             pool_w_out=pool_w_out.astype(BF16), dil_w_q=dil_w_q.astype(BF16), dil_w_o=dil_w_o.astype(BF16),
             mem_w_q=mem_w_q.astype(BF16), mem_w_o=mem_w_o.astype(BF16),
             w_k_shared=w_k_shared[None], w_v_shared=w_v_shared[None],
             ffn_w_gate=ffn_w_gate, ffn_w_up=ffn_w_up, ffn_w_down=ffn_w_down)

    mk_p, mv_p, mk_rows, mv_rows = shared_kv(mem_prompt.reshape(batch * n_mem, d), rows3(norm_mem_kv),
                                             mem_w_k, mem_w_v, tm=512, name="mem_kv")
    mk_p = mk_p.reshape(depth, batch, n_mem, -1)
    mv_p = mv_p.reshape(depth, batch, n_mem, -1)

    assert dec_seq == 1
    mk_s = cache_mem_k.reshape(depth, dec_batch, n_mem * MEM_HEADS, MEM_HEAD_DIM)
    mv_s = cache_mem_v.reshape(depth, dec_batch, n_mem * MEM_HEADS, MEM_HEAD_DIM)
    (y_p, pool_p, k_p, v_p), (y_s, pool_s, k_s, v_s) = _trunks(
        x_prompt.reshape(batch * seq, d), x_sample.reshape(dec_batch, d), (mk_p, mv_p), (mk_s, mv_s), w,
        batch=batch, seq=seq, pool_state=state_pool, win_k=cache_win_k, win_v=cache_win_v)

    max_window = max(wd for wd, _ in DIL_PATTERNS)
    keep_from = max(0, seq - max_window)
    kv_shape = (N_KV_HEADS, HEAD_DIM)
    mem_shape = (depth, batch, n_mem, MEM_HEADS, MEM_HEAD_DIM)
    return (y_p.reshape(batch, seq, d), y_s.reshape(dec_batch, dec_seq, d), pool_p, pool_s,
            k_p.reshape(batch, seq, *kv_shape)[:, keep_from:], v_p.reshape(batch, seq, *kv_shape)[:, keep_from:],
            k_s.reshape(dec_batch, dec_seq, *kv_shape), v_s.reshape(dec_batch, dec_seq, *kv_shape),
            mk_rows.reshape(mem_shape), mv_rows.reshape(mem_shape))
```
